```python
import numpy as np
import jax
import jax.numpy as jnp
from jax import lax

D_MODEL = 4096
BATCH = 4
SEQ = 4096
DEPTH = 1

HEAD_DIM = 128
N_HEADS = D_MODEL // HEAD_DIM
NSA_HEADS = N_HEADS // 2
MLA_HEADS = N_HEADS - NSA_HEADS
NSA_KV_GROUPS = 4
NSA_Q_PER_KV = NSA_HEADS // NSA_KV_GROUPS
CMP_LEN = 32
CMP_STRIDE = 16
CMP_HIDDEN = 256
SEL_LEN = 64
N_SEL = 16
N_LOCAL_FORCED = 2
FORCED_BONUS = 1e4
WINDOW = 512
MLA_Q_RANK = 1536
MLA_KV_RANK = 512
MLA_NOPE = 128
MLA_ROPE = 64
MLA_V = 128
ROPE_THETA = 500000.0
PARTIAL_ROT = HEAD_DIM // 4
D_FF = 4 * D_MODEL
ALPHA = (2 * DEPTH) ** 0.25
BETA = (8 * DEPTH) ** -0.25
SEL_Q_BLOCK = 32
DENSE_Q_BLOCK = 128
NEG = -1e30
N_ADA = 6
NSA_KV_WIDTH = NSA_KV_GROUPS * HEAD_DIM
IN_SPLITS = (NSA_HEADS * HEAD_DIM, 6 * NSA_KV_WIDTH, NSA_HEADS * 3, MLA_Q_RANK, MLA_KV_RANK, MLA_ROPE)
D_IN = sum(IN_SPLITS)
D_MIX_OUT = NSA_HEADS * HEAD_DIM + MLA_HEADS * MLA_V

kernel_name = 'hybrid_nsa_mla_deepnorm_adaln_block'


def _layernorm(x, g, b, eps=1e-5):
    xf = x.astype(jnp.float32)
    mu = jnp.mean(xf, axis=-1, keepdims=True)
    var = jnp.mean(jnp.square(xf - mu), axis=-1, keepdims=True)
    return ((xf - mu) * lax.rsqrt(var + eps) * g.astype(jnp.float32) + b.astype(jnp.float32)).astype(x.dtype)


def _rmsnorm(x, g, eps=1e-6):
    xf = x.astype(jnp.float32)
    y = xf * lax.rsqrt(jnp.mean(jnp.square(xf), axis=-1, keepdims=True) + eps)
    return (y * g.astype(jnp.float32)).astype(x.dtype)


def _rope_tables(positions, rot_dim):
    inv = jnp.power(ROPE_THETA, -jnp.arange(0, rot_dim, 2, dtype=jnp.float32) / rot_dim)
    ang = positions.astype(jnp.float32)[..., None] * inv
    return jnp.cos(ang)[:, :, None, :], jnp.sin(ang)[:, :, None, :]


def _apply_rope(x, cos, sin):
    half = cos.shape[-1]
    rot = 2 * half
    cos = cos.astype(x.dtype)
    sin = sin.astype(x.dtype)
    x1, x2, rest = x[..., :half], x[..., half:rot], x[..., rot:]
    return jnp.concatenate([x1 * cos - x2 * sin, x2 * cos + x1 * sin, rest], axis=-1)


def _nsa(q, k_cmp, v_cmp, k_sel, v_sel, k_win, v_win, gates, pos_k, pos_v, w_k1, w_k2, w_v1, w_v2):
    B, S = q.shape[0], q.shape[1]
    G, R, Dh = NSA_KV_GROUPS, NSA_Q_PER_KV, HEAD_DIM
    scale = Dh ** -0.5
    qg = q.reshape(B, S, G, R, Dh)
    t = jnp.arange(S)

    ncmp = (S - CMP_LEN) // CMP_STRIDE + 1
    blk = np.arange(ncmp)[:, None] * CMP_STRIDE + np.arange(CMP_LEN)[None, :]

    def compress(kv, pos, w1, w2):
        blocks = kv[:, blk] + pos[:, None, :]
        flat = blocks.transpose(0, 3, 1, 2, 4).reshape(B, G, ncmp, CMP_LEN * Dh)
        return jax.nn.gelu(flat @ w1) @ w2

    kc = compress(k_cmp, pos_k, w_k1, w_k2)
    vc = compress(v_cmp, pos_v, w_v1, w_v2)
    cmp_end = np.arange(ncmp) * CMP_STRIDE + CMP_LEN - 1
    mask_cmp = cmp_end[None, :] <= t[:, None]
    s_cmp = jnp.einsum('bsgrd,bgnd->bgrsn', qg, kc).astype(jnp.float32) * scale
    p_cmp = jax.nn.softmax(jnp.where(mask_cmp, s_cmp, NEG), axis=-1) * mask_cmp
    o_cmp = jnp.einsum('bgrsn,bgnd->bsgrd', p_cmp.astype(vc.dtype), vc)

    nb = S // SEL_LEN
    n_sel = min(N_SEL, nb)
    c_start = np.arange(ncmp) * CMP_STRIDE
    b_start = np.arange(nb) * SEL_LEN
    overlap = ((c_start[:, None] < b_start[None, :] + SEL_LEN) &
               (c_start[:, None] + CMP_LEN > b_start[None, :])).astype(np.float32)
    imp = jnp.einsum('bgrsn,nj->bgsj', p_cmp, jnp.asarray(overlap))
    cur = (t // SEL_LEN)[:, None]
    j = jnp.arange(nb)[None, :]
    forced = (j == 0) | ((j <= cur) & (j > cur - N_LOCAL_FORCED))
    imp = jnp.where(j > cur, NEG, imp + jnp.where(forced, FORCED_BONUS, 0.0))
    _, sel_idx = lax.top_k(imp, n_sel)

    QC = min(SEL_Q_BLOCK, S)
    nqc = S // QC
    k_sel_b = k_sel.transpose(0, 2, 1, 3).reshape(B, G, nb, SEL_LEN, Dh)
    v_sel_b = v_sel.transpose(0, 2, 1, 3).reshape(B, G, nb, SEL_LEN, Dh)
    pad = ((0, 0), (0, 0), (WINDOW, 0), (0, 0))
    k_win_p = jnp.pad(k_win.transpose(0, 2, 1, 3), pad)
    v_win_p = jnp.pad(v_win.transpose(0, 2, 1, 3), pad)
    q_chunks = qg.reshape(B, nqc, QC, G, R, Dh).transpose(1, 0, 2, 3, 4, 5)
    idx_chunks = sel_idx.reshape(B, G, nqc, QC, n_sel).transpose(2, 0, 1, 3, 4)
    gather_blocks = jax.vmap(jax.vmap(lambda kb, ix: kb[ix]))

    def chunk(args):
        qc, ic, ci = args
        start = ci * QC
        tq = start + jnp.arange(QC)
        ks = gather_blocks(k_sel_b, ic)
        vs = gather_blocks(v_sel_b, ic)
        kpos = ic[..., None] * SEL_LEN + jnp.arange(SEL_LEN)
        m_sel = (kpos <= tq[None, None, :, None, None]).reshape(B, G, 1, QC, n_sel * SEL_LEN)
        s_sel = jnp.einsum('bqgrd,bgqnld->bgrqnl', qc, ks).astype(jnp.float32) * scale
        s_sel = s_sel.reshape(B, G, R, QC, n_sel * SEL_LEN)
        p_sel = jax.nn.softmax(jnp.where(m_sel, s_sel, NEG), axis=-1).reshape(B, G, R, QC, n_sel, SEL_LEN)
        o_sel = jnp.einsum('bgrqnl,bgqnld->bqgrd', p_sel.astype(vs.dtype), vs)

        kw = lax.dynamic_slice_in_dim(k_win_p, start, WINDOW + QC, axis=2)
        vw = lax.dynamic_slice_in_dim(v_win_p, start, WINDOW + QC, axis=2)
        wpos = start - WINDOW + jnp.arange(WINDOW + QC)
        diff = tq[:, None] - wpos[None, :]
        m_win = (diff >= 0) & (diff < WINDOW) & (wpos[None, :] >= 0)
        s_win = jnp.einsum('bqgrd,bgkd->bgrqk', qc, kw).astype(jnp.float32) * scale
        p_win = jax.nn.softmax(jnp.where(m_win, s_win, NEG), axis=-1)
        o_win = jnp.einsum('bgrqk,bgkd->bqgrd', p_win.astype(vw.dtype), vw)
        return o_sel, o_win

    o_sel, o_win = lax.map(chunk, (q_chunks, idx_chunks, jnp.arange(nqc)))
    o_sel = o_sel.transpose(1, 0, 2, 3, 4, 5).reshape(B, S, G, R, Dh)
    o_win = o_win.transpose(1, 0, 2, 3, 4, 5).reshape(B, S, G, R, Dh)

    g = gates.reshape(B, S, G, R, 3)
    out = g[..., 0:1] * o_cmp + g[..., 1:2] * o_sel + g[..., 2:3] * o_win
    return out.reshape(B, S, NSA_HEADS * Dh)


def _causal_attention(q, k, v, scale):
    B, S, H, Dk = q.shape
    QB = min(DENSE_Q_BLOCK, S)
    nqb = S // QB
    q_blocks = q.reshape(B, nqb, QB, H, Dk).transpose(1, 0, 2, 3, 4)
    kpos = jnp.arange(S)

    def block(args):
        qb, bi = args
        tq = bi * QB + jnp.arange(QB)
        s = jnp.einsum('bqhd,bkhd->bhqk', qb, k).astype(jnp.float32) * scale
        p = jax.nn.softmax(jnp.where(kpos[None, :] <= tq[:, None], s, NEG), axis=-1)
        return jnp.einsum('bhqk,bkhd->bqhd', p.astype(v.dtype), v)

    o = lax.map(block, (q_blocks, jnp.arange(nqb)))
    return o.transpose(1, 0, 2, 3, 4).reshape(B, S, H * v.shape[-1])


def _mla(c_q, c_kv, k_rope, q_norm, kv_norm, w_uq, w_ukv, cos, sin):
    B, S = c_q.shape[0], c_q.shape[1]
    H = MLA_HEADS
    q = (_rmsnorm(c_q, q_norm) @ w_uq).reshape(B, S, H, MLA_NOPE + MLA_ROPE)
    q = jnp.concatenate([q[..., :MLA_NOPE], _apply_rope(q[..., MLA_NOPE:], cos, sin)], axis=-1)
    kv = (_rmsnorm(c_kv, kv_norm) @ w_ukv).reshape(B, S, H, MLA_NOPE + MLA_V)
    k_nope, v = kv[..., :MLA_NOPE], kv[..., MLA_NOPE:]
    k_pe = _apply_rope(k_rope[:, :, None, :], cos, sin)
    k = jnp.concatenate([k_nope, jnp.broadcast_to(k_pe, (B, S, H, MLA_ROPE))], axis=-1)
    return _causal_attention(q, k, v, (MLA_NOPE + MLA_ROPE) ** -0.5)


def _mixer(h, w_in, pos_k, pos_v, k1, k2, v1, v2, q_norm, kv_norm, w_uq, w_ukv, w_out,
           cos_p, sin_p, cos_m, sin_m):
    B, S = h.shape[0], h.shape[1]
    proj = h @ w_in
    q_nsa, kv_nsa, gate_logits, c_q, c_kv, k_rope = jnp.split(
        proj, np.cumsum(IN_SPLITS)[:-1].tolist(), axis=-1)
    q_nsa = _apply_rope(q_nsa.reshape(B, S, NSA_HEADS, HEAD_DIM), cos_p, sin_p)
    kv = kv_nsa.reshape(B, S, 6, NSA_KV_GROUPS, HEAD_DIM)
    k_cmp = _apply_rope(kv[:, :, 0], cos_p, sin_p)
    k_sel = _apply_rope(kv[:, :, 2], cos_p, sin_p)
    k_win = _apply_rope(kv[:, :, 4], cos_p, sin_p)
    gates = jax.nn.sigmoid(gate_logits).reshape(B, S, NSA_HEADS, 3)
    o_nsa = _nsa(q_nsa, k_cmp, kv[:, :, 1], k_sel, kv[:, :, 3], k_win, kv[:, :, 5], gates,
                 pos_k, pos_v, k1, k2, v1, v2)
    o_mla = _mla(c_q, c_kv, k_rope, q_norm, kv_norm, w_uq, w_ukv, cos_m, sin_m)
    return jnp.concatenate([o_nsa, o_mla], axis=-1) @ w_out


def setup_inputs(seed: int = 0) -> dict:
    key = jax.random.key(seed)
    ks = jax.random.split(key, 24)
    f32 = jnp.float32
    L = DEPTH

    def nrm(k, shape, scale):
        return jax.random.normal(k, shape, f32) * scale

    return {
        'x': nrm(ks[0], (BATCH, SEQ, D_MODEL), 1.0),
        'c': nrm(ks[1], (BATCH, D_MODEL), 1.0),
        'positions': jnp.tile(jnp.arange(SEQ, dtype=jnp.int32)[None, :], (BATCH, 1)),
        'w_ada': nrm(ks[2], (L, D_MODEL, N_ADA * D_MODEL), 0.5 * D_MODEL ** -0.5),
        'b_ada': nrm(ks[3], (L, N_ADA * D_MODEL), 0.02),
        'w_in': nrm(ks[4], (L, D_MODEL, D_IN), D_MODEL ** -0.5),
        'nsa_pos_k': nrm(ks[5], (L, CMP_LEN, HEAD_DIM), 0.1),
        'nsa_pos_v': nrm(ks[6], (L, CMP_LEN, HEAD_DIM), 0.1),
        'nsa_cmp_k1': nrm(ks[7], (L, CMP_LEN * HEAD_DIM, CMP_HIDDEN), (CMP_LEN * HEAD_DIM) ** -0.5),
        'nsa_cmp_k2': nrm(ks[8], (L, CMP_HIDDEN, HEAD_DIM), CMP_HIDDEN ** -0.5),
        'nsa_cmp_v1': nrm(ks[9], (L, CMP_LEN * HEAD_DIM, CMP_HIDDEN), (CMP_LEN * HEAD_DIM) ** -0.5),
        'nsa_cmp_v2': nrm(ks[10], (L, CMP_HIDDEN, HEAD_DIM), CMP_HIDDEN ** -0.5),
        'mla_q_norm': 1.0 + nrm(ks[11], (L, MLA_Q_RANK), 0.02),
        'mla_kv_norm': 1.0 + nrm(ks[12], (L, MLA_KV_RANK), 0.02),
        'mla_w_uq': nrm(ks[13], (L, MLA_Q_RANK, MLA_HEADS * (MLA_NOPE + MLA_ROPE)), MLA_Q_RANK ** -0.5),
        'mla_w_ukv': nrm(ks[14], (L, MLA_KV_RANK, MLA_HEADS * (MLA_NOPE + MLA_V)), MLA_KV_RANK ** -0.5),
        'w_out': nrm(ks[15], (L, D_MIX_OUT, D_MODEL), BETA * D_MIX_OUT ** -0.5),
        'ln1_g': 1.0 + nrm(ks[16], (L, D_MODEL), 0.02),
        'ln1_b': nrm(ks[17], (L, D_MODEL), 0.02),
        'w_ff1': nrm(ks[18], (L, D_MODEL, D_FF), D_MODEL ** -0.5),
        'w_ff2': nrm(ks[19], (L, D_FF, D_MODEL), BETA * D_FF ** -0.5),
        'ln2_g': 1.0 + nrm(ks[20], (L, D_MODEL), 0.02),
        'ln2_b': nrm(ks[21], (L, D_MODEL), 0.02),
    }


def reference(x, c, positions, w_ada, b_ada, w_in, nsa_pos_k, nsa_pos_v, nsa_cmp_k1, nsa_cmp_k2,
              nsa_cmp_v1, nsa_cmp_v2, mla_q_norm, mla_kv_norm, mla_w_uq, mla_w_ukv, w_out,
              ln1_g, ln1_b, w_ff1, w_ff2, ln2_g, ln2_b):
    cos_p, sin_p = _rope_tables(positions, PARTIAL_ROT)
    cos_m, sin_m = _rope_tables(positions, MLA_ROPE)
    cond = jax.nn.silu(c)
    for layer in range(DEPTH):
        mod = (cond @ w_ada[layer] + b_ada[layer])[:, None, :]
        sh_a, sc_a, g_a, sh_m, sc_m, g_m = jnp.split(mod, N_ADA, axis=-1)
        h = x * (1.0 + sc_a) + sh_a
        a = _mixer(h, w_in[layer], nsa_pos_k[layer], nsa_pos_v[layer], nsa_cmp_k1[layer],
                   nsa_cmp_k2[layer], nsa_cmp_v1[layer], nsa_cmp_v2[layer], mla_q_norm[layer],
                   mla_kv_norm[layer], mla_w_uq[layer], mla_w_ukv[layer], w_out[layer],
                   cos_p, sin_p, cos_m, sin_m)
        x = _layernorm(ALPHA * x + (1.0 + g_a) * a, ln1_g[layer], ln1_b[layer])
        h = x * (1.0 + sc_m) + sh_m
        f = jnp.square(jax.nn.relu(h @ w_ff1[layer])) @ w_ff2[layer]
        x = _layernorm(ALPHA * x + (1.0 + g_m) * f, ln2_g[layer], ln2_b[layer])
    return x
```

```python
import functools
import math

import numpy as np
import jax
import jax.numpy as jnp
from jax import lax
from jax.experimental import pallas as pl
from jax.experimental.pallas import tpu as pltpu

HEAD_DIM = 128
CMP_LEN = 32
CMP_STRIDE = 16
SEL_LEN = 64
N_SEL = 16
N_LOCAL_FORCED = 2
FORCED_BONUS = 1e4
WINDOW = 512
MLA_NOPE = 128
MLA_ROPE = 64
MLA_V = 128
ROPE_THETA = 500000.0
PARTIAL_ROT = HEAD_DIM // 4
NEG = -1e30
N_ADA = 6
LANES = 128
MLA_QK_PAD = 2 * LANES
VMEM_LIMIT = 56 * 1024 * 1024

F32 = jnp.float32
CDT = jnp.bfloat16


def _params(sem):
    return pltpu.CompilerParams(dimension_semantics=sem, vmem_limit_bytes=VMEM_LIMIT)


def _tile(n, cands):
    for c in cands:
        if n % c == 0:
            return c
    return n


def _ada_body(c_ref, w_ref, b_ref, o_ref):
    c = c_ref[...]
    cond = c * (1.0 / (1.0 + jnp.exp(-c)))
    o_ref[...] = jnp.dot(cond, w_ref[...], preferred_element_type=F32,
                         precision=lax.Precision.HIGHEST) + b_ref[...]


def _ada(c, w, b):
    bsz, d = c.shape
    n = w.shape[1]
    tn = _tile(n, (512, 256, 128))
    return pl.pallas_call(
        _ada_body,
        grid=(n // tn,),
        in_specs=[pl.BlockSpec((bsz, d), lambda j: (0, 0)),
                  pl.BlockSpec((d, tn), lambda j: (0, j)),
                  pl.BlockSpec((1, tn), lambda j: (0, j))],
        out_specs=pl.BlockSpec((bsz, tn), lambda j: (0, j)),
        out_shape=jax.ShapeDtypeStruct((bsz, n), F32),
        compiler_params=_params(("arbitrary",)),
        name="ada",
    )(c, w, b.reshape(1, n))


def _modcast_body(x_ref, sc_ref, sh_ref, o_ref):
    o_ref[...] = (x_ref[...] * (1.0 + sc_ref[0]) + sh_ref[0]).astype(o_ref.dtype)


def _modcast(x2, sc, sh, seq):
    t, d = x2.shape
    ts = _tile(seq, (256, 128, 64, 8))
    per = seq // ts
    vec = pl.BlockSpec((1, 1, d), lambda i: (i // per, 0, 0))
    return pl.pallas_call(
        _modcast_body,
        grid=(t // ts,),
        in_specs=[pl.BlockSpec((ts, d), lambda i: (i, 0)), vec, vec],
        out_specs=pl.BlockSpec((ts, d), lambda i: (i, 0)),
        out_shape=jax.ShapeDtypeStruct((t, d), CDT),
        compiler_params=_params(("arbitrary",)),
        name="modcast",
    )(x2, sc[:, None, :], sh[:, None, :])


def _mm_body(*refs, na, nex, prologue, epilogue):
    a_refs = refs[:na]
    b_ref = refs[na]
    ex = refs[na + 1:na + 1 + nex]
    o_ref = refs[na + 1 + nex]
    j = pl.program_id(1)
    if prologue is not None:
        a_sc = refs[na + 2 + nex]

        @pl.when(j == 0)
        def _():
            a_sc[...] = prologue(a_refs[0][...], ex).astype(a_sc.dtype)

        acc = jnp.dot(a_sc[...], b_ref[...], preferred_element_type=F32)
    else:
        acc = None
        off = 0
        for r in a_refs:
            kr = r.shape[1]
            part = jnp.dot(r[...], b_ref[off:off + kr, :], preferred_element_type=F32)
            acc = part if acc is None else acc + part
            off += kr
    if epilogue is None:
        o_ref[...] = acc.astype(o_ref.dtype)
    else:
        epilogue(acc, j, ex, o_ref)


def _mm(a_list, b, *, tm, tn, out_dtype, name, a_specs=None, prologue=None, epilogue=None,
        extras=()):
    m = a_list[0].shape[0]
    k, n = b.shape
    if a_specs is None:
        a_specs = [pl.BlockSpec((tm, a.shape[1]), lambda i, j: (i, 0)) for a in a_list]
    ex_arrays = [e[0] for e in extras]
    ex_specs = [e[1] for e in extras]
    scratch = [pltpu.VMEM((tm, k), CDT)] if prologue is not None else []
    body = functools.partial(_mm_body, na=len(a_list), nex=len(extras), prologue=prologue,
                             epilogue=epilogue)
    return pl.pallas_call(
        body,
        grid=(m // tm, n // tn),
        in_specs=a_specs + [pl.BlockSpec((k, tn), lambda i, j: (0, j))] + ex_specs,
        out_specs=pl.BlockSpec((tm, tn), lambda i, j: (i, j)),
        out_shape=jax.ShapeDtypeStruct((m, n), out_dtype),
        scratch_shapes=scratch,
        compiler_params=_params(("arbitrary", "arbitrary")),
        name=name,
    )(*a_list, b, *ex_arrays)


def _rot(acc, cos, sa, sb, shift):
    tn = acc.shape[1]
    reps = tn // cos.shape[1]

    def rep(t):
        return t if reps == 1 else jnp.concatenate([t] * reps, axis=1)

    return (acc * rep(cos) + pltpu.roll(acc, tn - shift, 1) * rep(sa)
            + pltpu.roll(acc, shift, 1) * rep(sb))


def _rope_epilogue(shift, n_rope, n_scaled, scale):
    def epi(acc, j, ex, o_ref):
        @pl.when(j < n_rope)
        def _():
            y = _rot(acc, ex[0][...], ex[1][...], ex[2][...], shift)
            y = y * jnp.where(j < n_scaled, scale, 1.0).astype(F32)
            o_ref[...] = y.astype(o_ref.dtype)

        @pl.when(j >= n_rope)
        def _():
            o_ref[...] = acc.astype(o_ref.dtype)

    return epi


def _krope_gate_epilogue(acc, j, ex, o_ref):
    y = _rot(acc, ex[0][...], ex[1][...], ex[2][...], MLA_ROPE // 2)
    lane = lax.broadcasted_iota(jnp.int32, acc.shape, 1)
    o_ref[...] = jnp.where(lane < LANES, y, 1.0 / (1.0 + jnp.exp(-acc))).astype(o_ref.dtype)


def _rms_prologue(eps):
    def pro(a, ex):
        g = ex[-1][...]
        y = a * lax.rsqrt(jnp.mean(a * a, axis=-1, keepdims=True) + eps)
        return y * g

    return pro


def _relu2_epilogue(acc, j, ex, o_ref):
    r = jnp.maximum(acc, 0.0)
    o_ref[...] = (r * r).astype(o_ref.dtype)


def _mmk_body(a_ref, b_ref, o_ref, acc_ref):
    kk = pl.program_id(2)

    @pl.when(kk == 0)
    def _():
        acc_ref[...] = jnp.zeros_like(acc_ref)

    acc_ref[...] += jnp.dot(a_ref[...], b_ref[...], preferred_element_type=F32)

    @pl.when(kk == pl.num_programs(2) - 1)
    def _():
        o_ref[...] = acc_ref[...].astype(o_ref.dtype)


def _mmk(a, b, *, tm, tn, tk, out_dtype, name):
    m, k = a.shape
    n = b.shape[1]
    return pl.pallas_call(
        _mmk_body,
        grid=(m // tm, n // tn, k // tk),
        in_specs=[pl.BlockSpec((tm, tk), lambda i, j, kk: (i, kk)),
                  pl.BlockSpec((tk, tn), lambda i, j, kk: (kk, j))],
        out_specs=pl.BlockSpec((tm, tn), lambda i, j, kk: (i, j)),
        out_shape=jax.ShapeDtypeStruct((m, n), out_dtype),
        scratch_shapes=[pltpu.VMEM((tm, tn), F32)],
        compiler_params=_params(("arbitrary", "arbitrary", "arbitrary")),
        name=name,
    )(a, b)


def _ln_body(x_ref, a_ref, gate_ref, g_ref, b_ref, *rest, alpha, with_mod):
    y = alpha * x_ref[...] + (1.0 + gate_ref[0]) * a_ref[...]
    mu = jnp.mean(y, axis=-1, keepdims=True)
    yc = y - mu
    var = jnp.mean(yc * yc, axis=-1, keepdims=True)
    out = yc * lax.rsqrt(var + 1e-5) * g_ref[...] + b_ref[...]
    if with_mod:
        sc_ref, sh_ref, o_ref, h_ref = rest
        o_ref[...] = out
        h_ref[...] = (out * (1.0 + sc_ref[0]) + sh_ref[0]).astype(h_ref.dtype)
    else:
        (o_ref,) = rest
        o_ref[...] = out


def _res_ln(x2, a2, gate, g, b, seq, alpha, mod=None):
    t, d = x2.shape
    ts = _tile(seq, (256, 128, 64, 8))
    per = seq // ts
    row = pl.BlockSpec((ts, d), lambda i: (i, 0))
    vec = pl.BlockSpec((1, 1, d), lambda i: (i // per, 0, 0))
    par = pl.BlockSpec((1, d), lambda i: (0, 0))
    ins = [x2, a2, gate[:, None, :], g.reshape(1, d), b.reshape(1, d)]
    specs = [row, row, vec, par, par]
    if mod is None:
        out_shape = jax.ShapeDtypeStruct((t, d), F32)
        out_specs = row
    else:
        ins += [mod[0][:, None, :], mod[1][:, None, :]]
        specs += [vec, vec]
        out_shape = (jax.ShapeDtypeStruct((t, d), F32), jax.ShapeDtypeStruct((t, d), CDT))
        out_specs = (row, row)
    return pl.pallas_call(
        functools.partial(_ln_body, alpha=alpha, with_mod=mod is not None),
        grid=(t // ts,),
        in_specs=specs,
        out_specs=out_specs,
        out_shape=out_shape,
        compiler_params=_params(("arbitrary",)),
        name="res_ln",
    )(*ins)


def _compress_body(x_ref, p0_ref, p1_ref, w1a_ref, w1b_ref, w2_ref, o_ref):
    x = x_ref[0, 0].astype(F32)
    nc = x.shape[0]
    xa = (x + p0_ref[...]).astype(CDT)
    xb = (x + p1_ref[...]).astype(CDT)
    a = jnp.dot(xa, w1a_ref[...], preferred_element_type=F32)
    bm = jnp.dot(xb, w1b_ref[...], preferred_element_type=F32)
    pre = a + pltpu.roll(bm, nc - 1, 0)
    hid = 0.5 * pre * (1.0 + jnp.tanh(math.sqrt(2.0 / math.pi) * (pre + 0.044715 * pre * pre * pre)))
    o_ref[0, 0] = jnp.dot(hid.astype(CDT), w2_ref[...], preferred_element_type=F32).astype(o_ref.dtype)


def _compress(xc, pos, w1, w2):
    bsz, g, nc, wid = xc.shape
    half = wid
    p0 = pos[:CMP_STRIDE].reshape(1, wid)
    p1 = pos[CMP_STRIDE:].reshape(1, wid)
    w1c = w1.astype(CDT)
    full = lambda shape: pl.BlockSpec(shape, lambda b, gi: (0,) * len(shape))
    return pl.pallas_call(
        _compress_body,
        grid=(bsz, g),
        in_specs=[pl.BlockSpec((1, 1, nc, wid), lambda b, gi: (b, gi, 0, 0)),
                  full((1, wid)), full((1, wid)),
                  full((half, w1.shape[1])), full((half, w1.shape[1])),
                  full(w2.shape)],
        out_specs=pl.BlockSpec((1, 1, nc, HEAD_DIM), lambda b, gi: (b, gi, 0, 0)),
        out_shape=jax.ShapeDtypeStruct((bsz, g, nc, HEAD_DIM), CDT),
        compiler_params=_params(("arbitrary", "arbitrary")),
        name="compress",
    )(xc, p0, p1, w1c[:half], w1c[half:], w2.astype(CDT))


def _cmp_body(q_ref, kc_ref, vc_ref, ov_ref, gate_ref, o_ref, bias_ref, *, tq, r_heads, nb, ncmp):
    gi = pl.program_id(1)
    i = pl.program_id(2)
    kc = kc_ref[0, 0]
    vc = vc_ref[0, 0]
    nc = kc.shape[0]
    t = i * tq + lax.broadcasted_iota(jnp.int32, (tq, nc), 0)
    cidx = lax.broadcasted_iota(jnp.int32, (tq, nc), 1)
    valid = (cidx * CMP_STRIDE + (CMP_LEN - 1) <= t) & (cidx < ncmp)
    validf = valid.astype(F32)
    real = cidx < ncmp
    gates = gate_ref[0]
    glane = lax.broadcasted_iota(jnp.int32, gates.shape, 1)
    psum = jnp.zeros((tq, nc), F32)
    for r in range(r_heads):
        q = q_ref[0, :, r * HEAD_DIM:(r + 1) * HEAD_DIM]
        s = lax.dot_general(q, kc, (((1,), (1,)), ((), ())), preferred_element_type=F32)
        s = jnp.where(valid, s, jnp.where(real, NEG, -jnp.inf))
        m = jnp.max(s, axis=-1, keepdims=True)
        e = jnp.exp(s - m)
        p = e / jnp.sum(e, axis=-1, keepdims=True) * validf
        psum = psum + p
        o = jnp.dot(p.astype(CDT), vc, preferred_element_type=F32)
        col = ((gi * r_heads + r) * 3 + 0)
        gcol = jnp.sum(jnp.where(glane == col, gates, 0.0), axis=-1, keepdims=True)
        o_ref[0, :, r * HEAD_DIM:(r + 1) * HEAD_DIM] = (gcol * o).astype(o_ref.dtype)

    imp = jnp.dot(psum, ov_ref[...], preferred_element_type=F32, precision=lax.Precision.HIGHEST)
    tt = i * tq + lax.broadcasted_iota(jnp.int32, (tq, LANES), 0)
    jj = lax.broadcasted_iota(jnp.int32, (tq, LANES), 1)
    cur = tt // SEL_LEN
    forced = (jj == 0) | ((jj <= cur) & (jj > cur - N_LOCAL_FORCED))
    imp = jnp.where(jj > cur, NEG, imp + jnp.where(forced, FORCED_BONUS, 0.0))
    v = imp.T[:nb]
    ridx = lax.broadcasted_iota(jnp.int32, (nb, tq), 0)
    sel = jnp.zeros((nb, tq), F32)
    for _ in range(min(N_SEL, nb)):
        mx = jnp.max(v, axis=0, keepdims=True)
        first = jnp.min(jnp.where(v == mx, ridx, nb), axis=0, keepdims=True)
        hit = ridx == first
        sel = jnp.where(hit, 1.0, sel)
        v = jnp.where(hit, -jnp.inf, v)
    bias_t = jnp.where(sel > 0.0, 0.0, NEG)
    if nb < LANES:
        bias_t = jnp.concatenate([bias_t, jnp.zeros((LANES - nb, tq), F32)], axis=0)
    bias_ref[0, 0] = bias_t.T.astype(bias_ref.dtype)


def _cmp_attn(qkv, kc, vc, overlap, gates, *, g_groups, r_heads, seq, ncmp):
    bsz = qkv.shape[0]
    nc = kc.shape[2]
    nb = seq // SEL_LEN
    tq = _tile(seq, (256, 128))
    qw = r_heads * HEAD_DIM
    body = functools.partial(_cmp_body, tq=tq, r_heads=r_heads, nb=nb, ncmp=ncmp)
    return pl.pallas_call(
        body,
        grid=(bsz, g_groups, seq // tq),
        in_specs=[pl.BlockSpec((1, tq, qw), lambda b, g, i: (b, i, g)),
                  pl.BlockSpec((1, 1, nc, HEAD_DIM), lambda b, g, i: (b, g, 0, 0)),
                  pl.BlockSpec((1, 1, nc, HEAD_DIM), lambda b, g, i: (b, g, 0, 0)),
                  pl.BlockSpec((nc, LANES), lambda b, g, i: (0, 0)),
                  pl.BlockSpec((1, tq, LANES), lambda b, g, i: (b, i, 1))],
        out_specs=(pl.BlockSpec((1, tq, qw), lambda b, g, i: (b, i, g)),
                   pl.BlockSpec((1, 1, tq, LANES), lambda b, g, i: (b, g, i, 0))),
        out_shape=(jax.ShapeDtypeStruct((bsz, seq, g_groups * qw), F32),
                   jax.ShapeDtypeStruct((bsz, g_groups, seq, LANES), CDT)),
        compiler_params=_params(("arbitrary", "arbitrary", "arbitrary")),
        name="cmp_attn",
    )(qkv, kc, vc, overlap, gates)


def _flash_update(s, v, m_ref, l_ref, acc_ref, idx):
    m_prev = m_ref[idx]
    m_new = jnp.maximum(m_prev, jnp.max(s, axis=-1, keepdims=True))
    alpha = jnp.exp(m_prev - m_new)
    p = jnp.exp(s - m_new)
    l_ref[idx] = alpha * l_ref[idx] + jnp.sum(p, axis=-1, keepdims=True)
    acc_ref[idx] = alpha * acc_ref[idx] + jnp.dot(p.astype(v.dtype), v, preferred_element_type=F32)
    m_ref[idx] = m_new


def _flash_init(m_ref, l_ref, acc_ref):
    m_ref[...] = jnp.full(m_ref.shape, -jnp.inf, F32)
    l_ref[...] = jnp.zeros(l_ref.shape, F32)
    acc_ref[...] = jnp.zeros(acc_ref.shape, F32)


def _nsa_flash_body(*refs, mode, tq, tk, r_heads, nsteps, branch):
    if mode == "sel":
        q_ref, k_ref, v_ref, e_ref, bias_ref, gate_ref, prev_ref, o_ref, m_ref, l_ref, acc_ref = refs
    else:
        q_ref, k_ref, v_ref, gate_ref, prev_ref, o_ref, m_ref, l_ref, acc_ref = refs
    gi = pl.program_id(1)
    i = pl.program_id(2)
    kk = pl.program_id(3)

    @pl.when(kk == 0)
    def _():
        _flash_init(m_ref, l_ref, acc_ref)

    if mode == "sel":
        kv = kk
        run = kv * tk <= i * tq + (tq - 1)
    else:
        kv = i - (nsteps - 1) + kk
        run = kv >= 0

    @pl.when(run)
    def _():
        row = i * tq + lax.broadcasted_iota(jnp.int32, (tq, tk), 0)
        col = kv * tk + lax.broadcasted_iota(jnp.int32, (tq, tk), 1)
        if mode == "sel":
            mask = col <= row
            kmat = jnp.concatenate([k_ref[0], e_ref[...]], axis=1)
        else:
            d = row - col
            mask = (d >= 0) & (d < WINDOW)
            kmat = k_ref[0]
        v = v_ref[0]
        for r in range(r_heads):
            q = q_ref[0, :, r * HEAD_DIM:(r + 1) * HEAD_DIM]
            if mode == "sel":
                q = jnp.concatenate([q, bias_ref[0, 0]], axis=1)
            s = lax.dot_general(q, kmat, (((1,), (1,)), ((), ())), preferred_element_type=F32)
            s = jnp.where(mask, s, NEG)
            _flash_update(s, v, m_ref, l_ref, acc_ref, r)

    @pl.when(kk == nsteps - 1)
    def _():
        gates = gate_ref[0]
        glane = lax.broadcasted_iota(jnp.int32, gates.shape, 1)
        for r in range(r_heads):
            o = acc_ref[r] / l_ref[r]
            col = (gi * r_heads + r) * 3 + branch
            gcol = jnp.sum(jnp.where(glane == col, gates, 0.0), axis=-1, keepdims=True)
            sl = slice(r * HEAD_DIM, (r + 1) * HEAD_DIM)
            o_ref[0, :, sl] = (prev_ref[0, :, sl] + gcol * o).astype(o_ref.dtype)


def _nsa_flash(mode, qkv, k_blk0, v_blk0, gates, prev, out_dtype, *, g_groups, r_heads, seq,
               e_mat=None, bias=None):
    bsz = qkv.shape[0]
    qw = r_heads * HEAD_DIM
    if mode == "sel":
        tq = _tile(seq, (256, 128))
        tk = _tile(seq, (512, 256, 128))
        nsteps = seq // tk
        kv_idx = lambda i, kk: jnp.minimum(kk, (i * tq + tq - 1) // tk)
        branch = 1
    else:
        tq = tk = _tile(seq, (256, 128))
        nsteps = (WINDOW + tk - 1) // tk + 1
        kv_idx = lambda i, kk: jnp.maximum(i - (nsteps - 1) + kk, 0)
        branch = 2
    in_specs = [pl.BlockSpec((1, tq, qw), lambda b, g, i, kk: (b, i, g)),
                pl.BlockSpec((1, tk, HEAD_DIM), lambda b, g, i, kk: (b, kv_idx(i, kk), k_blk0 + g)),
                pl.BlockSpec((1, tk, HEAD_DIM), lambda b, g, i, kk: (b, kv_idx(i, kk), v_blk0 + g))]
    ins = [qkv, qkv, qkv]
    if mode == "sel":
        in_specs += [pl.BlockSpec((tk, LANES), lambda b, g, i, kk: (kv_idx(i, kk), 0)),
                     pl.BlockSpec((1, 1, tq, LANES), lambda b, g, i, kk: (b, g, i, 0))]
        ins += [e_mat, bias]
    in_specs += [pl.BlockSpec((1, tq, LANES), lambda b, g, i, kk: (b, i, 1)),
                 pl.BlockSpec((1, tq, qw), lambda b, g, i, kk: (b, i, g))]
    ins += [gates, prev]
    body = functools.partial(_nsa_flash_body, mode=mode, tq=tq, tk=tk, r_heads=r_heads,
                             nsteps=nsteps, branch=branch)
    return pl.pallas_call(
        body,
        grid=(bsz, g_groups, seq // tq, nsteps),
        in_specs=in_specs,
        out_specs=pl.BlockSpec((1, tq, qw), lambda b, g, i, kk: (b, i, g)),
        out_shape=jax.ShapeDtypeStruct((bsz, seq, g_groups * qw), out_dtype),
        scratch_shapes=[pltpu.VMEM((r_heads, tq, 1), F32), pltpu.VMEM((r_heads, tq, 1), F32),
                        pltpu.VMEM((r_heads, tq, HEAD_DIM), F32)],
        compiler_params=_params(("arbitrary",) * 4),
        name="nsa_" + mode,
    )(*ins)


def _mla_body(q_ref, kn_ref, kr_ref, v_ref, o_ref, m_ref, l_ref, acc_ref, *, tq, tk, nsteps):
    i = pl.program_id(2)
    kk = pl.program_id(3)

    @pl.when(kk == 0)
    def _():
        _flash_init(m_ref, l_ref, acc_ref)

    @pl.when(kk * tk <= i * tq + (tq - 1))
    def _():
        row = i * tq + lax.broadcasted_iota(jnp.int32, (tq, tk), 0)
        col = kk * tk + lax.broadcasted_iota(jnp.int32, (tq, tk), 1)
        kmat = jnp.concatenate([kn_ref[0], kr_ref[0].astype(CDT)], axis=1)
        s = lax.dot_general(q_ref[0], kmat, (((1,), (1,)), ((), ())), preferred_element_type=F32)
        s = jnp.where(col <= row, s, NEG)
        _flash_update(s, v_ref[0], m_ref, l_ref, acc_ref, 0)

    @pl.when(kk == nsteps - 1)
    def _():
        o_ref[0] = (acc_ref[0] / l_ref[0]).astype(o_ref.dtype)


def _mla_attn(q, kv, kr, *, heads, seq):
    bsz = q.shape[0]
    tq = tk = _tile(seq, (512, 256, 128))
    nsteps = seq // tk
    kv_idx = lambda i, kk: jnp.minimum(kk, (i * tq + tq - 1) // tk)
    body = functools.partial(_mla_body, tq=tq, tk=tk, nsteps=nsteps)
    return pl.pallas_call(
        body,
        grid=(bsz, heads, seq // tq, nsteps),
        in_specs=[pl.BlockSpec((1, tq, MLA_QK_PAD), lambda b, h, i, kk: (b, i, h)),
                  pl.BlockSpec((1, tk, MLA_NOPE), lambda b, h, i, kk: (b, kv_idx(i, kk), h)),
                  pl.BlockSpec((1, tk, LANES), lambda b, h, i, kk: (b, kv_idx(i, kk), 0)),
                  pl.BlockSpec((1, tk, MLA_V), lambda b, h, i, kk: (b, kv_idx(i, kk), heads + h))],
        out_specs=pl.BlockSpec((1, tq, MLA_V), lambda b, h, i, kk: (b, i, h)),
        out_shape=jax.ShapeDtypeStruct((bsz, seq, heads * MLA_V), CDT),
        scratch_shapes=[pltpu.VMEM((1, tq, 1), F32), pltpu.VMEM((1, tq, 1), F32),
                        pltpu.VMEM((1, tq, MLA_V), F32)],
        compiler_params=_params(("arbitrary",) * 4),
        name="mla_attn",
    )(q, kv, kr, kv)


def _rope_tables(positions, rot_dim, period, offset):
    half = rot_dim // 2
    inv = jnp.power(ROPE_THETA, -jnp.arange(0, rot_dim, 2, dtype=F32) / rot_dim)
    ang = positions.astype(F32).reshape(-1)[:, None] * inv
    cos, sin = jnp.cos(ang), jnp.sin(ang)
    t = ang.shape[0]
    ones_l = jnp.ones((t, offset), F32)
    zeros_l = jnp.zeros((t, offset), F32)
    rest = period - offset - rot_dim
    ct = jnp.concatenate([ones_l, cos, cos, jnp.ones((t, rest), F32)], axis=1)
    sa = jnp.concatenate([zeros_l, -sin, jnp.zeros((t, half + rest), F32)], axis=1)
    sb = jnp.concatenate([zeros_l, jnp.zeros((t, half), F32), sin, jnp.zeros((t, rest), F32)], axis=1)
    return ct, sa, sb


def _table_extras(tables, tm):
    return [(tb, pl.BlockSpec((tm, tb.shape[1]), lambda i, j: (i, 0))) for tb in tables]


def _layer(x2, mod, positions, w_in, pos_k, pos_v, k1, k2, v1, v2, q_norm, kv_norm, w_uq, w_ukv,
           w_out, ln1_g, ln1_b, w_ff1, w_ff2, ln2_g, ln2_b, *, bsz, seq, alpha):
    t, d = x2.shape
    n_heads = d // HEAD_DIM
    nsa_h = n_heads // 2
    mla_h = n_heads - nsa_h
    q_rank = q_norm.shape[0]
    kv_rank = kv_norm.shape[0]
    d_in = w_in.shape[1]
    g_groups = (d_in - nsa_h * HEAD_DIM - nsa_h * 3 - q_rank - kv_rank - MLA_ROPE) // (6 * HEAD_DIM)
    r_heads = nsa_h // g_groups
    gw = g_groups * HEAD_DIM
    qw = nsa_h * HEAD_DIM
    sh_a, sc_a, g_a, sh_m, sc_m, g_m = jnp.split(mod, N_ADA, axis=-1)

    o_q, o_kv = 0, qw
    o_gate = o_kv + 6 * gw
    o_cq = o_gate + nsa_h * 3
    o_ckv = o_cq + q_rank
    o_kr = o_ckv + kv_rank
    kvcol = lambda idx: w_in[:, o_kv + idx * gw:o_kv + (idx + 1) * gw]
    w1c = jnp.concatenate([w_in[:, o_q:o_q + qw], kvcol(0), kvcol(2), kvcol(4), kvcol(3), kvcol(5)],
                          axis=1).astype(CDT)
    w2c = jnp.concatenate([w_in[:, o_cq:o_cq + q_rank], w_in[:, o_ckv:o_ckv + kv_rank], kvcol(1)],
                          axis=1).astype(CDT)
    zpad = lambda n: jnp.zeros((d, n), w_in.dtype)
    w3c = jnp.concatenate([w_in[:, o_kr:o_kr + MLA_ROPE], zpad(LANES - MLA_ROPE),
                           w_in[:, o_gate:o_gate + nsa_h * 3], zpad(LANES - nsa_h * 3)],
                          axis=1).astype(CDT)

    h = _modcast(x2, sc_a, sh_a, seq)

    tm = _tile(t, (1024, 512, 256, 128))
    nsa_tabs = _rope_tables(positions, PARTIAL_ROT, LANES, 0)
    n1 = w1c.shape[1]
    tn1 = _tile(gw, (512, 256, 128))
    n_rope_cols = qw + 3 * gw
    scale = HEAD_DIM ** -0.5
    p1 = _mm([h], w1c, tm=tm, tn=tn1, out_dtype=CDT, name="in_proj_rot",
             epilogue=_rope_epilogue(PARTIAL_ROT // 2, n_rope_cols // tn1, qw // tn1, scale),
             extras=_table_extras(nsa_tabs, tm))
    n2 = w2c.shape[1]
    tn2 = _tile(n2, (512, 256, 128))
    p2 = _mm([h], w2c, tm=tm, tn=tn2, out_dtype=F32, name="in_proj_lat")
    kr_tabs = _rope_tables(positions, MLA_ROPE, 2 * LANES, 0)
    p3 = _mm([h], w3c, tm=tm, tn=2 * LANES, out_dtype=F32, name="in_proj_kr_gate",
             epilogue=_krope_gate_epilogue, extras=_table_extras(kr_tabs, tm))

    qkv = p1.reshape(bsz, seq, n1)
    gates = p3.reshape(bsz, seq, 2 * LANES)

    nchunk = seq // CMP_STRIDE
    ncmp = (seq - CMP_LEN) // CMP_STRIDE + 1
    def chunked(a):
        return (a.reshape(bsz, nchunk, CMP_STRIDE, g_groups, HEAD_DIM).transpose(0, 3, 1, 2, 4)
                .reshape(bsz, g_groups, nchunk, CMP_STRIDE * HEAD_DIM))
    kc = _compress(chunked(qkv[:, :, qw:qw + gw]), pos_k, k1, k2)
    v_cmp = p2.reshape(bsz, seq, n2)[:, :, q_rank + kv_rank:]
    vc = _compress(chunked(v_cmp), pos_v, v1, v2)

    nb = seq // SEL_LEN
    c_start = np.arange(nchunk) * CMP_STRIDE
    b_start = np.arange(LANES) * SEL_LEN
    overlap = ((c_start[:, None] < b_start[None, :] + SEL_LEN) &
               (c_start[:, None] + CMP_LEN > b_start[None, :]) &
               (np.arange(nchunk)[:, None] < ncmp) & (np.arange(LANES)[None, :] < nb))
    overlap = jnp.asarray(overlap.astype(np.float32))
    o_cmp, bias = _cmp_attn(qkv, kc, vc, overlap, gates, g_groups=g_groups, r_heads=r_heads,
                            seq=seq, ncmp=ncmp)

    e_mat = jnp.asarray((np.arange(seq)[:, None] // SEL_LEN == np.arange(LANES)[None, :])
                        .astype(np.float32)).astype(CDT)
    blk = lambda col: col // HEAD_DIM
    o_sel = _nsa_flash("sel", qkv, blk(qw + gw), blk(qw + 3 * gw), gates, o_cmp, F32,
                       g_groups=g_groups, r_heads=r_heads, seq=seq, e_mat=e_mat, bias=bias)
    o_nsa = _nsa_flash("win", qkv, blk(qw + 2 * gw), blk(qw + 4 * gw), gates, o_sel, CDT,
                       g_groups=g_groups, r_heads=r_heads, seq=seq)

    qk_dim = MLA_NOPE + MLA_ROPE
    wq = w_uq.reshape(q_rank, mla_h, qk_dim)
    wq = jnp.concatenate([wq, jnp.zeros((q_rank, mla_h, MLA_QK_PAD - qk_dim), wq.dtype)], axis=-1)
    wq = wq.reshape(q_rank, mla_h * MLA_QK_PAD).astype(CDT)
    wkv = w_ukv.reshape(kv_rank, mla_h, MLA_NOPE + MLA_V)
    wkv = jnp.concatenate([wkv[:, :, :MLA_NOPE].reshape(kv_rank, mla_h * MLA_NOPE),
                           wkv[:, :, MLA_NOPE:].reshape(kv_rank, mla_h * MLA_V)], axis=1).astype(CDT)
    q_tabs = _rope_tables(positions, MLA_ROPE, MLA_QK_PAD, MLA_NOPE)
    tnq = _tile(mla_h * MLA_QK_PAD, (512, 256))
    q_mla = _mm([p2], wq, tm=tm, tn=tnq, out_dtype=CDT, name="mla_uq",
                a_specs=[pl.BlockSpec((tm, q_rank), lambda i, j: (i, 0))],
                prologue=_rms_prologue(1e-6),
                epilogue=_rope_epilogue(MLA_ROPE // 2, mla_h * MLA_QK_PAD // tnq,
                                        mla_h * MLA_QK_PAD // tnq, qk_dim ** -0.5),
                extras=_table_extras(q_tabs, tm)
                + [(q_norm.reshape(1, q_rank), pl.BlockSpec((1, q_rank), lambda i, j: (0, 0)))])
    tnkv = _tile(wkv.shape[1], (512, 256, 128))
    kv_mla = _mm([p2], wkv, tm=tm, tn=tnkv, out_dtype=CDT, name="mla_ukv",
                 a_specs=[pl.BlockSpec((tm, kv_rank), lambda i, j: (i, q_rank // kv_rank))],
                 prologue=_rms_prologue(1e-6),
                 extras=[(kv_norm.reshape(1, kv_rank), pl.BlockSpec((1, kv_rank), lambda i, j: (0, 0)))])
    o_mla = _mla_attn(q_mla.reshape(bsz, seq, -1), kv_mla.reshape(bsz, seq, -1), gates,
                      heads=mla_h, seq=seq)

    tno = _tile(d, (1024, 512, 256, 128))
    a = _mm([o_nsa.reshape(t, qw), o_mla.reshape(t, mla_h * MLA_V)], w_out.astype(CDT), tm=tm, tn=tno,
            out_dtype=F32, name="out_proj")
    x1, h2 = _res_ln(x2, a, g_a, ln1_g, ln1_b, seq, alpha, mod=(sc_m, sh_m))
    d_ff = w_ff1.shape[1]
    f1 = _mm([h2], w_ff1.astype(CDT), tm=tm, tn=_tile(d_ff, (1024, 512, 256, 128)), out_dtype=CDT,
             name="ff1", epilogue=_relu2_epilogue)
    f2 = _mmk(f1, w_ff2.astype(CDT), tm=tm, tn=tno, tk=_tile(d_ff, (4096, 2048, 1024, 512)),
              out_dtype=F32, name="ff2")
    return _res_ln(x1, f2, g_m, ln2_g, ln2_b, seq, alpha)


def kernel(x, c, positions, w_ada, b_ada, w_in, nsa_pos_k, nsa_pos_v, nsa_cmp_k1, nsa_cmp_k2, nsa_cmp_v1, nsa_cmp_v2, mla_q_norm, mla_kv_norm, mla_w_uq, mla_w_ukv, w_out, ln1_g, ln1_b, w_ff1, w_ff2, ln2_g, ln2_b):
    bsz, seq, d = x.shape
    depth = w_ada.shape[0]
    alpha = (2 * depth) ** 0.25
    x2 = x.reshape(bsz * seq, d)
    for layer in range(depth):
        mod = _ada(c, w_ada[layer], b_ada[layer])
        x2 = _layer(x2, mod, positions, w_in[layer], nsa_pos_k[layer], nsa_pos_v[layer],
                    nsa_cmp_k1[layer], nsa_cmp_k2[layer], nsa_cmp_v1[layer], nsa_cmp_v2[layer],
                    mla_q_norm[layer], mla_kv_norm[layer], mla_w_uq[layer], mla_w_ukv[layer],
                    w_out[layer], ln1_g[layer], ln1_b[layer], w_ff1[layer], w_ff2[layer],
                    ln2_g[layer], ln2_b[layer], bsz=bsz, seq=seq, alpha=alpha)
    return x2.reshape(bsz, seq, d)
```

```python
import functools
import math

import numpy as np
import jax
import jax.numpy as jnp
from jax import lax
from jax.experimental import pallas as pl
from jax.experimental.pallas import tpu as pltpu

HEAD_DIM = 128
CMP_LEN = 32
CMP_STRIDE = 16
SEL_LEN = 64
N_SEL = 16
N_LOCAL_FORCED = 2
FORCED_BONUS = 1e4
WINDOW = 512
MLA_NOPE = 128
MLA_ROPE = 64
MLA_V = 128
ROPE_THETA = 500000.0
PARTIAL_ROT = HEAD_DIM // 4
NEG = -1e30
LOG2E = math.log2(math.e)
N_ADA = 6
LANES = 128
MLA_QK_PAD = 2 * LANES
VMEM_LIMIT = 56 * 1024 * 1024
MM_SUB_COLS = 256

F32 = jnp.float32
CDT = jnp.bfloat16


def _params(sem):
    return pltpu.CompilerParams(dimension_semantics=sem, vmem_limit_bytes=VMEM_LIMIT)


def _tile(n, cands):
    for c in cands:
        if n % c == 0:
            return c
    return n


def _ada_body(c_ref, w_ref, b_ref, o_ref):
    c = c_ref[...]
    cond = c * (1.0 / (1.0 + jnp.exp(-c)))
    o_ref[...] = jnp.dot(cond, w_ref[...], preferred_element_type=F32,
                         precision=lax.Precision.HIGHEST) + b_ref[...]


def _ada(c, w, b):
    bsz, d = c.shape
    n = w.shape[1]
    tn = _tile(n, (512, 256, 128))
    return pl.pallas_call(
        _ada_body,
        grid=(n // tn,),
        in_specs=[pl.BlockSpec((bsz, d), lambda j: (0, 0)),
                  pl.BlockSpec((d, tn), lambda j: (0, j)),
                  pl.BlockSpec((1, tn), lambda j: (0, j))],
        out_specs=pl.BlockSpec((bsz, tn), lambda j: (0, j)),
        out_shape=jax.ShapeDtypeStruct((bsz, n), F32),
        compiler_params=_params(("arbitrary",)),
        name="ada",
    )(c, w, b.reshape(1, n))


def _modcast_body(x_ref, sc_ref, sh_ref, o_ref):
    o_ref[...] = (x_ref[...] * (1.0 + sc_ref[0]) + sh_ref[0]).astype(o_ref.dtype)


def _modcast(x2, sc, sh, seq):
    t, d = x2.shape
    ts = _tile(seq, (256, 128, 64, 8))
    per = seq // ts
    vec = pl.BlockSpec((1, 1, d), lambda i: (i // per, 0, 0))
    return pl.pallas_call(
        _modcast_body,
        grid=(t // ts,),
        in_specs=[pl.BlockSpec((ts, d), lambda i: (i, 0)), vec, vec],
        out_specs=pl.BlockSpec((ts, d), lambda i: (i, 0)),
        out_shape=jax.ShapeDtypeStruct((t, d), CDT),
        compiler_params=_params(("arbitrary",)),
        name="modcast",
    )(x2, sc[:, None, :], sh[:, None, :])


def _mm_body(*refs, na, nex, prologue, epilogue):
    a_refs = refs[:na]
    b_ref = refs[na]
    ex = refs[na + 1:na + 1 + nex]
    o_ref = refs[na + 1 + nex]
    j = pl.program_id(1)
    if prologue is not None:
        a_sc = refs[na + 2 + nex]

        @pl.when(j == 0)
        def _():
            a_sc[...] = prologue(a_refs[0][...], ex).astype(a_sc.dtype)

        a_refs = (a_sc,)
    tn = o_ref.shape[1]
    sub = MM_SUB_COLS if tn % MM_SUB_COLS == 0 else tn

    def run(fn):
        for s in range(tn // sub):
            cols = slice(s * sub, (s + 1) * sub)
            acc = None
            off = 0
            for r in a_refs:
                kr = r.shape[1]
                part = jnp.dot(r[...], b_ref[off:off + kr, cols], preferred_element_type=F32)
                acc = part if acc is None else acc + part
                off += kr
            o_ref[:, cols] = fn(acc, j, ex).astype(o_ref.dtype)

    if epilogue is None:
        run(lambda acc, j, ex: acc)
    else:
        for cond, fn in epilogue(j):
            pl.when(cond)(functools.partial(run, fn))


def _mm(a_list, b, *, tm, tn, out_dtype, name, a_specs=None, prologue=None, epilogue=None,
        extras=()):
    m = a_list[0].shape[0]
    k, n = b.shape
    if a_specs is None:
        a_specs = [pl.BlockSpec((tm, a.shape[1]), lambda i, j: (i, 0)) for a in a_list]
    ex_arrays = [e[0] for e in extras]
    ex_specs = [e[1] for e in extras]
    scratch = [pltpu.VMEM((tm, k), CDT)] if prologue is not None else []
    body = functools.partial(_mm_body, na=len(a_list), nex=len(extras), prologue=prologue,
                             epilogue=epilogue)
    return pl.pallas_call(
        body,
        grid=(m // tm, n // tn),
        in_specs=a_specs + [pl.BlockSpec((k, tn), lambda i, j: (0, j))] + ex_specs,
        out_specs=pl.BlockSpec((tm, tn), lambda i, j: (i, j)),
        out_shape=jax.ShapeDtypeStruct((m, n), out_dtype),
        scratch_shapes=scratch,
        compiler_params=_params(("arbitrary", "arbitrary")),
        name=name,
    )(*a_list, b, *ex_arrays)


def _rot(acc, cos, sa, sb, shift):
    tn = acc.shape[1]
    reps = tn // cos.shape[1]

    def rep(t):
        return t if reps == 1 else jnp.concatenate([t] * reps, axis=1)

    return (acc * rep(cos) + pltpu.roll(acc, tn - shift, 1) * rep(sa)
            + pltpu.roll(acc, shift, 1) * rep(sb))


def _rope_epilogue(shift, n_rope, n_scaled, scale):
    def roped(acc, j, ex):
        y = _rot(acc, ex[0][...], ex[1][...], ex[2][...], shift)
        return y * jnp.where(j < n_scaled, scale, 1.0).astype(F32)

    def epi(j):
        return [(j < n_rope, roped), (j >= n_rope, lambda acc, j, ex: acc)]

    return epi


def _krope_gate_epilogue(j):
    def fn(acc, j, ex):
        y = _rot(acc, ex[0][...], ex[1][...], ex[2][...], MLA_ROPE // 2)
        lane = lax.broadcasted_iota(jnp.int32, acc.shape, 1)
        return jnp.where(lane < LANES, y, 1.0 / (1.0 + jnp.exp(-acc)))

    return [(j >= 0, fn)]


def _rms_prologue(eps):
    def pro(a, ex):
        g = ex[-1][...]
        y = a * lax.rsqrt(jnp.mean(a * a, axis=-1, keepdims=True) + eps)
        return y * g

    return pro


def _relu2_epilogue(j):
    def fn(acc, j, ex):
        r = jnp.maximum(acc, 0.0)
        return r * r

    return [(j >= 0, fn)]


def _mmk_body(a_ref, b_ref, o_ref, acc_ref):
    kk = pl.program_id(2)

    @pl.when(kk == 0)
    def _():
        acc_ref[...] = jnp.zeros_like(acc_ref)

    acc_ref[...] += jnp.dot(a_ref[...], b_ref[...], preferred_element_type=F32)

    @pl.when(kk == pl.num_programs(2) - 1)
    def _():
        o_ref[...] = acc_ref[...].astype(o_ref.dtype)


def _mmk(a, b, *, tm, tn, tk, out_dtype, name):
    m, k = a.shape
    n = b.shape[1]
    return pl.pallas_call(
        _mmk_body,
        grid=(m // tm, n // tn, k // tk),
        in_specs=[pl.BlockSpec((tm, tk), lambda i, j, kk: (i, kk)),
                  pl.BlockSpec((tk, tn), lambda i, j, kk: (kk, j))],
        out_specs=pl.BlockSpec((tm, tn), lambda i, j, kk: (i, j)),
        out_shape=jax.ShapeDtypeStruct((m, n), out_dtype),
        scratch_shapes=[pltpu.VMEM((tm, tn), F32)],
        compiler_params=_params(("arbitrary", "arbitrary", "arbitrary")),
        name=name,
    )(a, b)


def _ln_body(x_ref, a_ref, gate_ref, g_ref, b_ref, *rest, alpha, with_mod):
    y = alpha * x_ref[...] + (1.0 + gate_ref[0]) * a_ref[...]
    mu = jnp.mean(y, axis=-1, keepdims=True)
    yc = y - mu
    var = jnp.mean(yc * yc, axis=-1, keepdims=True)
    out = yc * lax.rsqrt(var + 1e-5) * g_ref[...] + b_ref[...]
    if with_mod:
        sc_ref, sh_ref, o_ref, h_ref = rest
        o_ref[...] = out
        h_ref[...] = (out * (1.0 + sc_ref[0]) + sh_ref[0]).astype(h_ref.dtype)
    else:
        (o_ref,) = rest
        o_ref[...] = out


def _res_ln(x2, a2, gate, g, b, seq, alpha, mod=None):
    t, d = x2.shape
    ts = _tile(seq, (256, 128, 64, 8))
    per = seq // ts
    row = pl.BlockSpec((ts, d), lambda i: (i, 0))
    vec = pl.BlockSpec((1, 1, d), lambda i: (i // per, 0, 0))
    par = pl.BlockSpec((1, d), lambda i: (0, 0))
    ins = [x2, a2, gate[:, None, :], g.reshape(1, d), b.reshape(1, d)]
    specs = [row, row, vec, par, par]
    if mod is None:
        out_shape = jax.ShapeDtypeStruct((t, d), F32)
        out_specs = row
    else:
        ins += [mod[0][:, None, :], mod[1][:, None, :]]
        specs += [vec, vec]
        out_shape = (jax.ShapeDtypeStruct((t, d), F32), jax.ShapeDtypeStruct((t, d), CDT))
        out_specs = (row, row)
    return pl.pallas_call(
        functools.partial(_ln_body, alpha=alpha, with_mod=mod is not None),
        grid=(t // ts,),
        in_specs=specs,
        out_specs=out_specs,
        out_shape=out_shape,
        compiler_params=_params(("arbitrary",)),
        name="res_ln",
    )(*ins)


def _compress_body(x_ref, p0_ref, p1_ref, w1a_ref, w1b_ref, w2_ref, o_ref):
    x = x_ref[0, 0].astype(F32)
    nc = x.shape[0]
    xa = (x + p0_ref[...]).astype(CDT)
    xb = (x + p1_ref[...]).astype(CDT)
    a = jnp.dot(xa, w1a_ref[...], preferred_element_type=F32)
    bm = jnp.dot(xb, w1b_ref[...], preferred_element_type=F32)
    pre = a + pltpu.roll(bm, nc - 1, 0)
    hid = 0.5 * pre * (1.0 + jnp.tanh(math.sqrt(2.0 / math.pi) * (pre + 0.044715 * pre * pre * pre)))
    o_ref[0, 0] = jnp.dot(hid.astype(CDT), w2_ref[...], preferred_element_type=F32).astype(o_ref.dtype)


def _compress(xc, pos, w1, w2):
    bsz, g, nc, wid = xc.shape
    half = wid
    p0 = pos[:CMP_STRIDE].reshape(1, wid)
    p1 = pos[CMP_STRIDE:].reshape(1, wid)
    w1c = w1.astype(CDT)
    full = lambda shape: pl.BlockSpec(shape, lambda b, gi: (0,) * len(shape))
    return pl.pallas_call(
        _compress_body,
        grid=(bsz, g),
        in_specs=[pl.BlockSpec((1, 1, nc, wid), lambda b, gi: (b, gi, 0, 0)),
                  full((1, wid)), full((1, wid)),
                  full((half, w1.shape[1])), full((half, w1.shape[1])),
                  full(w2.shape)],
        out_specs=pl.BlockSpec((1, 1, nc, HEAD_DIM), lambda b, gi: (b, gi, 0, 0)),
        out_shape=jax.ShapeDtypeStruct((bsz, g, nc, HEAD_DIM), CDT),
        compiler_params=_params(("arbitrary", "arbitrary")),
        name="compress",
    )(xc, p0, p1, w1c[:half], w1c[half:], w2.astype(CDT))


def _cmp_body(q_ref, kc_ref, vc_ref, ov_ref, gate_ref, o_ref, bias_ref, *, tq, r_heads, nb, ncmp):
    gi = pl.program_id(1)
    i = pl.program_id(2)
    kc = kc_ref[0, 0]
    vc = vc_ref[0, 0]
    nc = kc.shape[0]
    t = i * tq + lax.broadcasted_iota(jnp.int32, (tq, nc), 0)
    cidx = lax.broadcasted_iota(jnp.int32, (tq, nc), 1)
    valid = (cidx * CMP_STRIDE + (CMP_LEN - 1) <= t) & (cidx < ncmp)
    validf = valid.astype(F32)
    real = cidx < ncmp
    gates = gate_ref[0]
    glane = lax.broadcasted_iota(jnp.int32, gates.shape, 1)
    psum = jnp.zeros((tq, nc), F32)
    for r in range(r_heads):
        q = q_ref[0, :, r * HEAD_DIM:(r + 1) * HEAD_DIM]
        s = lax.dot_general(q, kc, (((1,), (1,)), ((), ())), preferred_element_type=F32)
        s = jnp.where(valid, s, jnp.where(real, NEG, -jnp.inf))
        m = jnp.max(s, axis=-1, keepdims=True)
        e = jnp.exp2(s - m)
        p = e / jnp.sum(e, axis=-1, keepdims=True) * validf
        psum = psum + p
        o = jnp.dot(p.astype(CDT), vc, preferred_element_type=F32)
        col = ((gi * r_heads + r) * 3 + 0)
        gcol = jnp.sum(jnp.where(glane == col, gates, 0.0), axis=-1, keepdims=True)
        o_ref[0, :, r * HEAD_DIM:(r + 1) * HEAD_DIM] = (gcol * o).astype(o_ref.dtype)

    imp = jnp.dot(psum, ov_ref[...], preferred_element_type=F32, precision=lax.Precision.HIGHEST)
    tt = i * tq + lax.broadcasted_iota(jnp.int32, (tq, LANES), 0)
    jj = lax.broadcasted_iota(jnp.int32, (tq, LANES), 1)
    cur = tt // SEL_LEN
    forced = (jj == 0) | ((jj <= cur) & (jj > cur - N_LOCAL_FORCED))
    imp = jnp.where(jj > cur, NEG, imp + jnp.where(forced, FORCED_BONUS, 0.0))
    v = imp.T[:nb]
    ridx = lax.broadcasted_iota(jnp.int32, (nb, tq), 0)
    sel = jnp.zeros((nb, tq), F32)
    for _ in range(min(N_SEL, nb)):
        mx = jnp.max(v, axis=0, keepdims=True)
        first = jnp.min(jnp.where(v == mx, ridx, nb), axis=0, keepdims=True)
        hit = ridx == first
        sel = jnp.where(hit, 1.0, sel)
        v = jnp.where(hit, -jnp.inf, v)
    bias_t = jnp.where(sel > 0.0, 0.0, NEG)
    if nb < LANES:
        bias_t = jnp.concatenate([bias_t, jnp.zeros((LANES - nb, tq), F32)], axis=0)
    bias_ref[0, 0] = bias_t.T.astype(bias_ref.dtype)


def _cmp_attn(qkv, kc, vc, overlap, gates, *, g_groups, r_heads, seq, ncmp):
    bsz = qkv.shape[0]
    nc = kc.shape[2]
    nb = seq // SEL_LEN
    tq = _tile(seq, (256, 128))
    qw = r_heads * HEAD_DIM
    body = functools.partial(_cmp_body, tq=tq, r_heads=r_heads, nb=nb, ncmp=ncmp)
    return pl.pallas_call(
        body,
        grid=(bsz, g_groups, seq // tq),
        in_specs=[pl.BlockSpec((1, tq, qw), lambda b, g, i: (b, i, g)),
                  pl.BlockSpec((1, 1, nc, HEAD_DIM), lambda b, g, i: (b, g, 0, 0)),
                  pl.BlockSpec((1, 1, nc, HEAD_DIM), lambda b, g, i: (b, g, 0, 0)),
                  pl.BlockSpec((nc, LANES), lambda b, g, i: (0, 0)),
                  pl.BlockSpec((1, tq, LANES), lambda b, g, i: (b, i, 1))],
        out_specs=(pl.BlockSpec((1, tq, qw), lambda b, g, i: (b, i, g)),
                   pl.BlockSpec((1, 1, tq, LANES), lambda b, g, i: (b, g, i, 0))),
        out_shape=(jax.ShapeDtypeStruct((bsz, seq, g_groups * qw), F32),
                   jax.ShapeDtypeStruct((bsz, g_groups, seq, LANES), CDT)),
        compiler_params=_params(("arbitrary", "arbitrary", "arbitrary")),
        name="cmp_attn",
    )(qkv, kc, vc, overlap, gates)


ACC_ROWS = HEAD_DIM + 16


def _kv_step(kmat, q, vt, m_ref, acc_ref, mask):
    s = lax.dot_general(kmat, q, (((1,), (1,)), ((), ())), preferred_element_type=F32)
    if mask is not None:
        s = jnp.where(mask, s, NEG)
    m_prev = m_ref[...]
    m_new = jnp.maximum(m_prev, jnp.max(s, axis=0, keepdims=True))
    p = jnp.exp2(s - m_new).astype(CDT)
    ones_row = jnp.where(lax.broadcasted_iota(jnp.int32, (ACC_ROWS - HEAD_DIM, vt.shape[1]), 0) == 0,
                         1.0, 0.0).astype(CDT)
    vaug = jnp.concatenate([vt, ones_row], axis=0)
    acc_ref[...] = (jnp.exp2(m_prev - m_new) * acc_ref[...]
                    + jnp.dot(vaug, p, preferred_element_type=F32))
    m_ref[...] = m_new


def _flash_init(m_ref, acc_ref):
    m_ref[...] = jnp.full(m_ref.shape, -jnp.inf, F32)
    acc_ref[...] = jnp.zeros(acc_ref.shape, F32)


NSA_CHAIN_HEADS = 2


def _nsa_finalize(acc_ref, gate_ref, prev_ref, o_ref, gi, tq, r_heads, hpc, branch):
    gates_t = gate_ref[0].T
    rid = lax.broadcasted_iota(jnp.int32, gates_t.shape, 0)
    for r in range(r_heads):
        acc = acc_ref[r // hpc]
        cs = slice((r % hpc) * tq, (r % hpc + 1) * tq)
        col = (gi * r_heads + r) * 3 + branch
        grow = jnp.sum(jnp.where(rid == col, gates_t, 0.0), axis=0, keepdims=True)
        o = acc[:HEAD_DIM, cs] * (grow / acc[HEAD_DIM:HEAD_DIM + 1, cs])
        sl = slice(r * HEAD_DIM, (r + 1) * HEAD_DIM)
        o_ref[0, :, sl] = (prev_ref[0, :, sl] + o.T).astype(o_ref.dtype)


def _sel_body(q_ref, bias_ref, k_ref, e_ref, vt_ref, gate_ref, prev_ref, o_ref, qc_ref, m_ref, acc_ref,
              *, tq, tk, r_heads, hpc):
    gi = pl.program_id(1)
    i = pl.program_id(2)
    cols = hpc * tq
    for r in range(r_heads):
        qc_ref[r * tq:(r + 1) * tq, :] = jnp.concatenate(
            [q_ref[0, :, r * HEAD_DIM:(r + 1) * HEAD_DIM], bias_ref[0, 0]], axis=1)
    _flash_init(m_ref, acc_ref)
    n_full = (i * tq) // tk

    def step(j, mask):
        off = pl.multiple_of(j * tk, tk)
        kmat = jnp.concatenate([k_ref[0, pl.ds(off, tk), :], e_ref[pl.ds(off, tk), :]], axis=1)
        for c in range(r_heads // hpc):
            _kv_step(kmat, qc_ref[c * cols:(c + 1) * cols, :], vt_ref[0, 0, j], m_ref.at[c],
                     acc_ref.at[c], mask)

    def full_tile(j, carry):
        step(j, None)
        return carry

    lax.fori_loop(0, n_full, full_tile, 0)
    key = lax.broadcasted_iota(jnp.int32, (tk, cols), 0)
    tok = lax.broadcasted_iota(jnp.int32, (tk, cols), 1) & (tq - 1)
    for dd in range(max(1, tq // tk)):
        j = n_full + dd
        step(j, key + (j * tk - i * tq) <= tok)
    _nsa_finalize(acc_ref, gate_ref, prev_ref, o_ref, gi, tq, r_heads, hpc, 1)


def _win_body(q_ref, k_ref, vt_ref, gate_ref, prev_ref, o_ref, qc_ref, m_ref, acc_ref,
              *, tq, r_heads, hpc, n_tiles):
    gi = pl.program_id(1)
    i = pl.program_id(2)
    tk = tq
    cols = hpc * tq
    for r in range(r_heads):
        qc_ref[r * tq:(r + 1) * tq, :] = q_ref[0, :, r * HEAD_DIM:(r + 1) * HEAD_DIM]
    _flash_init(m_ref, acc_ref)
    key = lax.broadcasted_iota(jnp.int32, (tk, cols), 0)
    tok = lax.broadcasted_iota(jnp.int32, (tk, cols), 1) & (tq - 1)
    for dd in range(n_tiles):
        back = n_tiles - 1 - dd

        @pl.when(i >= back)
        def _():
            j = i - back
            off = pl.multiple_of(j * tk, tk)
            dist = tok - key + back * tk
            mask = (dist >= 0) & (dist < WINDOW)
            for c in range(r_heads // hpc):
                _kv_step(k_ref[0, pl.ds(off, tk), :], qc_ref[c * cols:(c + 1) * cols, :], vt_ref[0, 0, j],
                         m_ref.at[c], acc_ref.at[c], mask)

    _nsa_finalize(acc_ref, gate_ref, prev_ref, o_ref, gi, tq, r_heads, hpc, 2)


def _nsa_attn(mode, qkv, k_blk0, vt, gates, prev, out_dtype, *, g_groups, r_heads, seq,
              e_mat=None, bias=None):
    bsz = qkv.shape[0]
    qw = r_heads * HEAD_DIM
    tk = vt.shape[-1]
    kspec = pl.BlockSpec((1, seq, HEAD_DIM), lambda b, g, i: (b, 0, k_blk0 + g))
    vspec = pl.BlockSpec((1, 1) + vt.shape[2:], lambda b, g, i: (b, g, 0, 0, 0))
    hpc = NSA_CHAIN_HEADS if r_heads % NSA_CHAIN_HEADS == 0 else 1
    if mode == "sel":
        tq = _tile(seq, (256, 128))
        qk_w = 2 * HEAD_DIM
        body = functools.partial(_sel_body, tq=tq, tk=tk, r_heads=r_heads, hpc=hpc)
    else:
        tq = tk
        qk_w = HEAD_DIM
        body = functools.partial(_win_body, tq=tq, r_heads=r_heads, hpc=hpc,
                                 n_tiles=(WINDOW + tk - 1) // tk + 1)
    qspec = pl.BlockSpec((1, tq, qw), lambda b, g, i: (b, i, g))
    gspec = pl.BlockSpec((1, tq, LANES), lambda b, g, i: (b, i, 1))
    if mode == "sel":
        in_specs = [qspec, pl.BlockSpec((1, 1, tq, LANES), lambda b, g, i: (b, g, i, 0)), kspec,
                    pl.BlockSpec((seq, LANES), lambda b, g, i: (0, 0)), vspec, gspec, qspec]
        ins = [qkv, bias, qkv, e_mat, vt, gates, prev]
    else:
        in_specs = [qspec, kspec, vspec, gspec, qspec]
        ins = [qkv, qkv, vt, gates, prev]
    return pl.pallas_call(
        body,
        grid=(bsz, g_groups, seq // tq),
        in_specs=in_specs,
        out_specs=qspec,
        out_shape=jax.ShapeDtypeStruct((bsz, seq, g_groups * qw), out_dtype),
        scratch_shapes=[pltpu.VMEM((r_heads * tq, qk_w), CDT),
                        pltpu.VMEM((r_heads // hpc, 1, hpc * tq), F32),
                        pltpu.VMEM((r_heads // hpc, ACC_ROWS, hpc * tq), F32)],
        compiler_params=_params(("arbitrary",) * 3),
        name="nsa_" + mode,
    )(*ins)


def _mla_body(q_ref, kn_ref, kr_ref, vt_ref, o_ref, m_ref, acc_ref, *, tq, tk, hg):
    i = pl.program_id(2)
    _flash_init(m_ref, acc_ref)
    n_full = (i * tq) // tk

    def step(j, mask):
        off = pl.multiple_of(j * tk, tk)
        kr = kr_ref[0, pl.ds(off, tk), :]
        for c in range(hg):
            kmat = jnp.concatenate([kn_ref[0, pl.ds(off, tk), c * MLA_NOPE:(c + 1) * MLA_NOPE], kr], axis=1)
            _kv_step(kmat, q_ref[0, :, c * MLA_QK_PAD:(c + 1) * MLA_QK_PAD], vt_ref[0, c, j],
                     m_ref.at[c], acc_ref.at[c], mask)

    def full_tile(j, carry):
        step(j, None)
        return carry

    lax.fori_loop(0, n_full, full_tile, 0)
    key = lax.broadcasted_iota(jnp.int32, (tk, tq), 0)
    tok = lax.broadcasted_iota(jnp.int32, (tk, tq), 1)
    for dd in range(max(1, tq // tk)):
        j = n_full + dd
        step(j, key + (j * tk - i * tq) <= tok)
    for c in range(hg):
        acc = acc_ref[c]
        o = acc[:HEAD_DIM] * (1.0 / acc[HEAD_DIM:HEAD_DIM + 1])
        o_ref[0, :, c * MLA_V:(c + 1) * MLA_V] = o.T.astype(o_ref.dtype)


def _mla_attn(q, kv, kr, vt, *, heads, seq):
    bsz = q.shape[0]
    tk = vt.shape[-1]
    tq = _tile(seq, (512, 256, 128))
    hg = 4 if heads % 4 == 0 else 1
    body = functools.partial(_mla_body, tq=tq, tk=tk, hg=hg)
    return pl.pallas_call(
        body,
        grid=(bsz, heads // hg, seq // tq),
        in_specs=[pl.BlockSpec((1, tq, hg * MLA_QK_PAD), lambda b, h, i: (b, i, h)),
                  pl.BlockSpec((1, seq, hg * MLA_NOPE), lambda b, h, i: (b, 0, h)),
                  pl.BlockSpec((1, seq, LANES), lambda b, h, i: (b, 0, 0)),
                  pl.BlockSpec((1, hg) + vt.shape[2:], lambda b, h, i: (b, h, 0, 0, 0))],
        out_specs=pl.BlockSpec((1, tq, hg * MLA_V), lambda b, h, i: (b, i, h)),
        out_shape=jax.ShapeDtypeStruct((bsz, seq, heads * MLA_V), CDT),
        scratch_shapes=[pltpu.VMEM((hg, 1, tq), F32), pltpu.VMEM((hg, ACC_ROWS, tq), F32)],
        compiler_params=_params(("arbitrary",) * 3),
        name="mla_attn",
    )(q, kv, kr, vt)


def _rope_tables(positions, rot_dim, period, offset):
    half = rot_dim // 2
    inv = jnp.power(ROPE_THETA, -jnp.arange(0, rot_dim, 2, dtype=F32) / rot_dim)
    ang = positions.astype(F32).reshape(-1)[:, None] * inv
    cos, sin = jnp.cos(ang), jnp.sin(ang)
    t = ang.shape[0]
    ones_l = jnp.ones((t, offset), F32)
    zeros_l = jnp.zeros((t, offset), F32)
    rest = period - offset - rot_dim
    ct = jnp.concatenate([ones_l, cos, cos, jnp.ones((t, rest), F32)], axis=1)
    sa = jnp.concatenate([zeros_l, -sin, jnp.zeros((t, half + rest), F32)], axis=1)
    sb = jnp.concatenate([zeros_l, jnp.zeros((t, half), F32), sin, jnp.zeros((t, rest), F32)], axis=1)
    return ct, sa, sb


def _table_extras(tables, tm):
    return [(tb, pl.BlockSpec((tm, tb.shape[1]), lambda i, j: (i, 0))) for tb in tables]


def _layer(x2, mod, positions, w_in, pos_k, pos_v, k1, k2, v1, v2, q_norm, kv_norm, w_uq, w_ukv,
           w_out, ln1_g, ln1_b, w_ff1, w_ff2, ln2_g, ln2_b, *, bsz, seq, alpha):
    t, d = x2.shape
    n_heads = d // HEAD_DIM
    nsa_h = n_heads // 2
    mla_h = n_heads - nsa_h
    q_rank = q_norm.shape[0]
    kv_rank = kv_norm.shape[0]
    d_in = w_in.shape[1]
    g_groups = (d_in - nsa_h * HEAD_DIM - nsa_h * 3 - q_rank - kv_rank - MLA_ROPE) // (6 * HEAD_DIM)
    r_heads = nsa_h // g_groups
    gw = g_groups * HEAD_DIM
    qw = nsa_h * HEAD_DIM
    sh_a, sc_a, g_a, sh_m, sc_m, g_m = jnp.split(mod, N_ADA, axis=-1)

    o_q, o_kv = 0, qw
    o_gate = o_kv + 6 * gw
    o_cq = o_gate + nsa_h * 3
    o_ckv = o_cq + q_rank
    o_kr = o_ckv + kv_rank
    kvcol = lambda idx: w_in[:, o_kv + idx * gw:o_kv + (idx + 1) * gw]
    w1c = jnp.concatenate([w_in[:, o_q:o_q + qw], kvcol(0), kvcol(2), kvcol(4), kvcol(3), kvcol(5)],
                          axis=1).astype(CDT)
    w2c = jnp.concatenate([w_in[:, o_cq:o_cq + q_rank], w_in[:, o_ckv:o_ckv + kv_rank], kvcol(1)],
                          axis=1).astype(CDT)
    zpad = lambda n: jnp.zeros((d, n), w_in.dtype)
    w3c = jnp.concatenate([w_in[:, o_kr:o_kr + MLA_ROPE], zpad(LANES - MLA_ROPE),
                           w_in[:, o_gate:o_gate + nsa_h * 3], zpad(LANES - nsa_h * 3)],
                          axis=1).astype(CDT)

    h = _modcast(x2, sc_a, sh_a, seq)

    tm = _tile(t, (1024, 512, 256, 128))
    nsa_tabs = _rope_tables(positions, PARTIAL_ROT, LANES, 0)
    n1 = w1c.shape[1]
    tn1 = _tile(gw, (512, 256, 128))
    n_rope_cols = qw + 3 * gw
    scale = HEAD_DIM ** -0.5 * LOG2E
    p1 = _mm([h], w1c, tm=tm, tn=tn1, out_dtype=CDT, name="in_proj_rot",
             epilogue=_rope_epilogue(PARTIAL_ROT // 2, n_rope_cols // tn1, qw // tn1, scale),
             extras=_table_extras(nsa_tabs, tm))
    n2 = w2c.shape[1]
    tn2 = _tile(n2, (512, 256, 128))
    p2 = _mm([h], w2c, tm=tm, tn=tn2, out_dtype=F32, name="in_proj_lat")
    kr_tabs = _rope_tables(positions, MLA_ROPE, 2 * LANES, 0)
    p3 = _mm([h], w3c, tm=tm, tn=2 * LANES, out_dtype=F32, name="in_proj_kr_gate",
             epilogue=_krope_gate_epilogue, extras=_table_extras(kr_tabs, tm))

    qkv = p1.reshape(bsz, seq, n1)
    gates = p3.reshape(bsz, seq, 2 * LANES)

    nchunk = seq // CMP_STRIDE
    ncmp = (seq - CMP_LEN) // CMP_STRIDE + 1
    def chunked(a):
        return (a.reshape(bsz, nchunk, CMP_STRIDE, g_groups, HEAD_DIM).transpose(0, 3, 1, 2, 4)
                .reshape(bsz, g_groups, nchunk, CMP_STRIDE * HEAD_DIM))
    kc = _compress(chunked(qkv[:, :, qw:qw + gw]), pos_k, k1, k2)
    v_cmp = p2.reshape(bsz, seq, n2)[:, :, q_rank + kv_rank:]
    vc = _compress(chunked(v_cmp), pos_v, v1, v2)

    nb = seq // SEL_LEN
    c_start = np.arange(nchunk) * CMP_STRIDE
    b_start = np.arange(LANES) * SEL_LEN
    overlap = ((c_start[:, None] < b_start[None, :] + SEL_LEN) &
               (c_start[:, None] + CMP_LEN > b_start[None, :]) &
               (np.arange(nchunk)[:, None] < ncmp) & (np.arange(LANES)[None, :] < nb))
    overlap = jnp.asarray(overlap.astype(np.float32))
    o_cmp, bias = _cmp_attn(qkv, kc, vc, overlap, gates, g_groups=g_groups, r_heads=r_heads,
                            seq=seq, ncmp=ncmp)

    e_mat = jnp.asarray((np.arange(seq)[:, None] // SEL_LEN == np.arange(LANES)[None, :])
                        .astype(np.float32)).astype(CDT)
    blk = lambda col: col // HEAD_DIM
    tk = _tile(seq, (256, 128))
    def transposed_tiles(a, heads, tk=tk):
        return a.reshape(bsz, seq // tk, tk, heads, HEAD_DIM).transpose(0, 3, 1, 4, 2)
    tk_big = _tile(seq, (512, 256, 128))
    vt_sel = transposed_tiles(qkv[:, :, qw + 3 * gw:qw + 4 * gw], g_groups, tk_big)
    vt_win = transposed_tiles(qkv[:, :, qw + 4 * gw:qw + 5 * gw], g_groups)
    o_sel = _nsa_attn("sel", qkv, blk(qw + gw), vt_sel, gates, o_cmp, F32,
                      g_groups=g_groups, r_heads=r_heads, seq=seq, e_mat=e_mat, bias=bias)
    o_nsa = _nsa_attn("win", qkv, blk(qw + 2 * gw), vt_win, gates, o_sel, CDT,
                      g_groups=g_groups, r_heads=r_heads, seq=seq)

    qk_dim = MLA_NOPE + MLA_ROPE
    wq = w_uq.reshape(q_rank, mla_h, qk_dim)
    wq = jnp.concatenate([wq, jnp.zeros((q_rank, mla_h, MLA_QK_PAD - qk_dim), wq.dtype)], axis=-1)
    wq = wq.reshape(q_rank, mla_h * MLA_QK_PAD).astype(CDT)
    wkv = w_ukv.reshape(kv_rank, mla_h, MLA_NOPE + MLA_V)
    wkv = jnp.concatenate([wkv[:, :, :MLA_NOPE].reshape(kv_rank, mla_h * MLA_NOPE),
                           wkv[:, :, MLA_NOPE:].reshape(kv_rank, mla_h * MLA_V)], axis=1).astype(CDT)
    q_tabs = _rope_tables(positions, MLA_ROPE, MLA_QK_PAD, MLA_NOPE)
    tnq = _tile(mla_h * MLA_QK_PAD, (512, 256))
    q_mla = _mm([p2], wq, tm=tm, tn=tnq, out_dtype=CDT, name="mla_uq",
                a_specs=[pl.BlockSpec((tm, q_rank), lambda i, j: (i, 0))],
                prologue=_rms_prologue(1e-6),
                epilogue=_rope_epilogue(MLA_ROPE // 2, mla_h * MLA_QK_PAD // tnq,
                                        mla_h * MLA_QK_PAD // tnq, qk_dim ** -0.5 * LOG2E),
                extras=_table_extras(q_tabs, tm)
                + [(q_norm.reshape(1, q_rank), pl.BlockSpec((1, q_rank), lambda i, j: (0, 0)))])
    tnkv = _tile(wkv.shape[1], (512, 256, 128))
    kv_mla = _mm([p2], wkv, tm=tm, tn=tnkv, out_dtype=CDT, name="mla_ukv",
                 a_specs=[pl.BlockSpec((tm, kv_rank), lambda i, j: (i, q_rank // kv_rank))],
                 prologue=_rms_prologue(1e-6),
                 extras=[(kv_norm.reshape(1, kv_rank), pl.BlockSpec((1, kv_rank), lambda i, j: (0, 0)))])
    kv3 = kv_mla.reshape(bsz, seq, -1)
    o_mla = _mla_attn(q_mla.reshape(bsz, seq, -1), kv3, gates[:, :, :LANES].astype(CDT),
                      transposed_tiles(kv3[:, :, mla_h * MLA_NOPE:], mla_h, tk_big),
                      heads=mla_h, seq=seq)

    tno = _tile(d, (1024, 512, 256, 128))
    a = _mm([o_nsa.reshape(t, qw), o_mla.reshape(t, mla_h * MLA_V)], w_out.astype(CDT), tm=tm, tn=tno,
            out_dtype=F32, name="out_proj")
    x1, h2 = _res_ln(x2, a, g_a, ln1_g, ln1_b, seq, alpha, mod=(sc_m, sh_m))
    d_ff = w_ff1.shape[1]
    f1 = _mm([h2], w_ff1.astype(CDT), tm=tm, tn=_tile(d_ff, (1024, 512, 256, 128)), out_dtype=CDT,
             name="ff1", epilogue=_relu2_epilogue)
    f2 = _mmk(f1, w_ff2.astype(CDT), tm=tm, tn=tno, tk=_tile(d_ff, (4096, 2048, 1024, 512)),
              out_dtype=F32, name="ff2")
    return _res_ln(x1, f2, g_m, ln2_g, ln2_b, seq, alpha)


def kernel(x, c, positions, w_ada, b_ada, w_in, nsa_pos_k, nsa_pos_v, nsa_cmp_k1, nsa_cmp_k2, nsa_cmp_v1, nsa_cmp_v2, mla_q_norm, mla_kv_norm, mla_w_uq, mla_w_ukv, w_out, ln1_g, ln1_b, w_ff1, w_ff2, ln2_g, ln2_b):
    bsz, seq, d = x.shape
    depth = w_ada.shape[0]
    alpha = (2 * depth) ** 0.25
    x2 = x.reshape(bsz * seq, d)
    for layer in range(depth):
        mod = _ada(c, w_ada[layer], b_ada[layer])
        x2 = _layer(x2, mod, positions, w_in[layer], nsa_pos_k[layer], nsa_pos_v[layer],
                    nsa_cmp_k1[layer], nsa_cmp_k2[layer], nsa_cmp_v1[layer], nsa_cmp_v2[layer],
                    mla_q_norm[layer], mla_kv_norm[layer], mla_w_uq[layer], mla_w_ukv[layer],
                    w_out[layer], ln1_g[layer], ln1_b[layer], w_ff1[layer], w_ff2[layer],
                    ln2_g[layer], ln2_b[layer], bsz=bsz, seq=seq, alpha=alpha)
    return x2.reshape(bsz, seq, d)
```

```python
import functools
import math

import numpy as np
import jax
import jax.numpy as jnp
from jax import lax
from jax.experimental import pallas as pl
from jax.experimental.pallas import tpu as pltpu

HEAD_DIM = 128
CMP_LEN = 32
CMP_STRIDE = 16
SEL_LEN = 64
N_SEL = 16
N_LOCAL_FORCED = 2
FORCED_BONUS = 1e4
WINDOW = 512
MLA_NOPE = 128
MLA_ROPE = 64
MLA_V = 128
ROPE_THETA = 500000.0
PARTIAL_ROT = HEAD_DIM // 4
NEG = -1e30
LOG2E = math.log2(math.e)
N_ADA = 6
LANES = 128
MLA_QK_PAD = 2 * LANES
VMEM_LIMIT = 56 * 1024 * 1024
MM_SUB_COLS = 256

F32 = jnp.float32
CDT = jnp.bfloat16


def _params(sem):
    return pltpu.CompilerParams(dimension_semantics=sem, vmem_limit_bytes=VMEM_LIMIT)


def _tile(n, cands):
    for c in cands:
        if n % c == 0:
            return c
    return n


def _ada_body(c_ref, w_ref, b_ref, o_ref):
    c = c_ref[...]
    cond = c * (1.0 / (1.0 + jnp.exp(-c)))
    o_ref[...] = jnp.dot(cond, w_ref[...], preferred_element_type=F32,
                         precision=lax.Precision.HIGHEST) + b_ref[...]


def _ada(c, w, b):
    bsz, d = c.shape
    n = w.shape[1]
    tn = _tile(n, (512, 256, 128))
    return pl.pallas_call(
        _ada_body,
        grid=(n // tn,),
        in_specs=[pl.BlockSpec((bsz, d), lambda j: (0, 0)),
                  pl.BlockSpec((d, tn), lambda j: (0, j)),
                  pl.BlockSpec((1, tn), lambda j: (0, j))],
        out_specs=pl.BlockSpec((bsz, tn), lambda j: (0, j)),
        out_shape=jax.ShapeDtypeStruct((bsz, n), F32),
        compiler_params=_params(("arbitrary",)),
        name="ada",
    )(c, w, b.reshape(1, n))


def _modcast_body(x_ref, sc_ref, sh_ref, o_ref):
    o_ref[...] = (x_ref[...] * (1.0 + sc_ref[0]) + sh_ref[0]).astype(o_ref.dtype)


def _modcast(x2, sc, sh, seq):
    t, d = x2.shape
    ts = _tile(seq, (256, 128, 64, 8))
    per = seq // ts
    vec = pl.BlockSpec((1, 1, d), lambda i: (i // per, 0, 0))
    return pl.pallas_call(
        _modcast_body,
        grid=(t // ts,),
        in_specs=[pl.BlockSpec((ts, d), lambda i: (i, 0)), vec, vec],
        out_specs=pl.BlockSpec((ts, d), lambda i: (i, 0)),
        out_shape=jax.ShapeDtypeStruct((t, d), CDT),
        compiler_params=_params(("arbitrary",)),
        name="modcast",
    )(x2, sc[:, None, :], sh[:, None, :])


def _mm_body(*refs, na, nex, prologue, epilogue):
    a_refs = refs[:na]
    b_ref = refs[na]
    ex = refs[na + 1:na + 1 + nex]
    o_ref = refs[na + 1 + nex]
    j = pl.program_id(1)
    if prologue is not None:
        a_sc = refs[na + 2 + nex]

        @pl.when(j == 0)
        def _():
            a_sc[...] = prologue(a_refs[0][...], ex).astype(a_sc.dtype)

        a_refs = (a_sc,)
    tn = o_ref.shape[1]
    sub = MM_SUB_COLS if tn % MM_SUB_COLS == 0 else tn

    def run(fn):
        for s in range(tn // sub):
            cols = slice(s * sub, (s + 1) * sub)
            acc = None
            off = 0
            for r in a_refs:
                kr = r.shape[1]
                part = jnp.dot(r[...], b_ref[off:off + kr, cols], preferred_element_type=F32)
                acc = part if acc is None else acc + part
                off += kr
            o_ref[:, cols] = fn(acc, j, ex).astype(o_ref.dtype)

    if epilogue is None:
        run(lambda acc, j, ex: acc)
    else:
        for cond, fn in epilogue(j):
            pl.when(cond)(functools.partial(run, fn))


def _mm(a_list, b, *, tm, tn, out_dtype, name, a_specs=None, prologue=None, epilogue=None,
        extras=()):
    m = a_list[0].shape[0]
    k, n = b.shape
    if a_specs is None:
        a_specs = [pl.BlockSpec((tm, a.shape[1]), lambda i, j: (i, 0)) for a in a_list]
    ex_arrays = [e[0] for e in extras]
    ex_specs = [e[1] for e in extras]
    scratch = [pltpu.VMEM((tm, k), CDT)] if prologue is not None else []
    body = functools.partial(_mm_body, na=len(a_list), nex=len(extras), prologue=prologue,
                             epilogue=epilogue)
    return pl.pallas_call(
        body,
        grid=(m // tm, n // tn),
        in_specs=a_specs + [pl.BlockSpec((k, tn), lambda i, j: (0, j))] + ex_specs,
        out_specs=pl.BlockSpec((tm, tn), lambda i, j: (i, j)),
        out_shape=jax.ShapeDtypeStruct((m, n), out_dtype),
        scratch_shapes=scratch,
        compiler_params=_params(("arbitrary", "arbitrary")),
        name=name,
    )(*a_list, b, *ex_arrays)


def _rot(acc, cos, sa, sb, shift):
    tn = acc.shape[1]
    reps = tn // cos.shape[1]

    def rep(t):
        return t if reps == 1 else jnp.concatenate([t] * reps, axis=1)

    return (acc * rep(cos) + pltpu.roll(acc, tn - shift, 1) * rep(sa)
            + pltpu.roll(acc, shift, 1) * rep(sb))


def _rope_epilogue(shift, n_rope, n_scaled, scale):
    def roped(acc, j, ex):
        y = _rot(acc, ex[0][...], ex[1][...], ex[2][...], shift)
        return y * jnp.where(j < n_scaled, scale, 1.0).astype(F32)

    def epi(j):
        return [(j < n_rope, roped), (j >= n_rope, lambda acc, j, ex: acc)]

    return epi


def _krope_gate_epilogue(j):
    def fn(acc, j, ex):
        y = _rot(acc, ex[0][...], ex[1][...], ex[2][...], MLA_ROPE // 2)
        lane = lax.broadcasted_iota(jnp.int32, acc.shape, 1)
        return jnp.where(lane < LANES, y, 1.0 / (1.0 + jnp.exp(-acc)))

    return [(j >= 0, fn)]


def _rms_prologue(eps):
    def pro(a, ex):
        g = ex[-1][...]
        y = a * lax.rsqrt(jnp.mean(a * a, axis=-1, keepdims=True) + eps)
        return y * g

    return pro


def _relu2_epilogue(j):
    def fn(acc, j, ex):
        r = jnp.maximum(acc, 0.0)
        return r * r

    return [(j >= 0, fn)]


def _mmk_body(a_ref, b_ref, o_ref, acc_ref):
    kk = pl.program_id(2)

    @pl.when(kk == 0)
    def _():
        acc_ref[...] = jnp.zeros_like(acc_ref)

    acc_ref[...] += jnp.dot(a_ref[...], b_ref[...], preferred_element_type=F32)

    @pl.when(kk == pl.num_programs(2) - 1)
    def _():
        o_ref[...] = acc_ref[...].astype(o_ref.dtype)


def _mmk(a, b, *, tm, tn, tk, out_dtype, name):
    m, k = a.shape
    n = b.shape[1]
    return pl.pallas_call(
        _mmk_body,
        grid=(m // tm, n // tn, k // tk),
        in_specs=[pl.BlockSpec((tm, tk), lambda i, j, kk: (i, kk)),
                  pl.BlockSpec((tk, tn), lambda i, j, kk: (kk, j))],
        out_specs=pl.BlockSpec((tm, tn), lambda i, j, kk: (i, j)),
        out_shape=jax.ShapeDtypeStruct((m, n), out_dtype),
        scratch_shapes=[pltpu.VMEM((tm, tn), F32)],
        compiler_params=_params(("arbitrary", "arbitrary", "arbitrary")),
        name=name,
    )(a, b)


def _ln_body(x_ref, a_ref, gate_ref, g_ref, b_ref, *rest, alpha, with_mod):
    y = alpha * x_ref[...] + (1.0 + gate_ref[0]) * a_ref[...]
    mu = jnp.mean(y, axis=-1, keepdims=True)
    yc = y - mu
    var = jnp.mean(yc * yc, axis=-1, keepdims=True)
    out = yc * lax.rsqrt(var + 1e-5) * g_ref[...] + b_ref[...]
    if with_mod:
        sc_ref, sh_ref, o_ref, h_ref = rest
        o_ref[...] = out
        h_ref[...] = (out * (1.0 + sc_ref[0]) + sh_ref[0]).astype(h_ref.dtype)
    else:
        (o_ref,) = rest
        o_ref[...] = out


def _res_ln(x2, a2, gate, g, b, seq, alpha, mod=None):
    t, d = x2.shape
    ts = _tile(seq, (256, 128, 64, 8))
    per = seq // ts
    row = pl.BlockSpec((ts, d), lambda i: (i, 0))
    vec = pl.BlockSpec((1, 1, d), lambda i: (i // per, 0, 0))
    par = pl.BlockSpec((1, d), lambda i: (0, 0))
    ins = [x2, a2, gate[:, None, :], g.reshape(1, d), b.reshape(1, d)]
    specs = [row, row, vec, par, par]
    if mod is None:
        out_shape = jax.ShapeDtypeStruct((t, d), F32)
        out_specs = row
    else:
        ins += [mod[0][:, None, :], mod[1][:, None, :]]
        specs += [vec, vec]
        out_shape = (jax.ShapeDtypeStruct((t, d), F32), jax.ShapeDtypeStruct((t, d), CDT))
        out_specs = (row, row)
    return pl.pallas_call(
        functools.partial(_ln_body, alpha=alpha, with_mod=mod is not None),
        grid=(t // ts,),
        in_specs=specs,
        out_specs=out_specs,
        out_shape=out_shape,
        compiler_params=_params(("arbitrary",)),
        name="res_ln",
    )(*ins)


def _compress_body(x_ref, p0_ref, p1_ref, w1a_ref, w1b_ref, w2_ref, o_ref):
    x = x_ref[0, 0].astype(F32)
    nc = x.shape[0]
    xa = (x + p0_ref[...]).astype(CDT)
    xb = (x + p1_ref[...]).astype(CDT)
    a = jnp.dot(xa, w1a_ref[...], preferred_element_type=F32)
    bm = jnp.dot(xb, w1b_ref[...], preferred_element_type=F32)
    pre = a + pltpu.roll(bm, nc - 1, 0)
    hid = 0.5 * pre * (1.0 + jnp.tanh(math.sqrt(2.0 / math.pi) * (pre + 0.044715 * pre * pre * pre)))
    o_ref[0, 0] = jnp.dot(hid.astype(CDT), w2_ref[...], preferred_element_type=F32).astype(o_ref.dtype)


def _compress(xc, pos, w1, w2):
    bsz, g, nc, wid = xc.shape
    half = wid
    p0 = pos[:CMP_STRIDE].reshape(1, wid)
    p1 = pos[CMP_STRIDE:].reshape(1, wid)
    w1c = w1.astype(CDT)
    full = lambda shape: pl.BlockSpec(shape, lambda b, gi: (0,) * len(shape))
    return pl.pallas_call(
        _compress_body,
        grid=(bsz, g),
        in_specs=[pl.BlockSpec((1, 1, nc, wid), lambda b, gi: (b, gi, 0, 0)),
                  full((1, wid)), full((1, wid)),
                  full((half, w1.shape[1])), full((half, w1.shape[1])),
                  full(w2.shape)],
        out_specs=pl.BlockSpec((1, 1, nc, HEAD_DIM), lambda b, gi: (b, gi, 0, 0)),
        out_shape=jax.ShapeDtypeStruct((bsz, g, nc, HEAD_DIM), CDT),
        compiler_params=_params(("arbitrary", "arbitrary")),
        name="compress",
    )(xc, p0, p1, w1c[:half], w1c[half:], w2.astype(CDT))


def _cmp_body(q_ref, kc_ref, vc_ref, ov_ref, gate_ref, o_ref, bias_ref, *, tq, r_heads, nb, ncmp):
    gi = pl.program_id(1)
    i = pl.program_id(2)
    kc = kc_ref[0, 0]
    vc = vc_ref[0, 0]
    nc = kc.shape[0]
    t = i * tq + lax.broadcasted_iota(jnp.int32, (tq, nc), 0)
    cidx = lax.broadcasted_iota(jnp.int32, (tq, nc), 1)
    valid = (cidx * CMP_STRIDE + (CMP_LEN - 1) <= t) & (cidx < ncmp)
    validf = valid.astype(F32)
    real = cidx < ncmp
    gates = gate_ref[0]
    glane = lax.broadcasted_iota(jnp.int32, gates.shape, 1)
    psum = jnp.zeros((tq, nc), F32)
    for r in range(r_heads):
        q = q_ref[0, :, r * HEAD_DIM:(r + 1) * HEAD_DIM]
        s = lax.dot_general(q, kc, (((1,), (1,)), ((), ())), preferred_element_type=F32)
        s = jnp.where(valid, s, jnp.where(real, NEG, -jnp.inf))
        m = jnp.max(s, axis=-1, keepdims=True)
        e = jnp.exp2(s - m)
        p = e / jnp.sum(e, axis=-1, keepdims=True) * validf
        psum = psum + p
        o = jnp.dot(p.astype(CDT), vc, preferred_element_type=F32)
        col = ((gi * r_heads + r) * 3 + 0)
        gcol = jnp.sum(jnp.where(glane == col, gates, 0.0), axis=-1, keepdims=True)
        o_ref[0, :, r * HEAD_DIM:(r + 1) * HEAD_DIM] = (gcol * o).astype(o_ref.dtype)

    imp = jnp.dot(psum, ov_ref[...], preferred_element_type=F32, precision=lax.Precision.HIGHEST)
    tt = i * tq + lax.broadcasted_iota(jnp.int32, (tq, LANES), 0)
    jj = lax.broadcasted_iota(jnp.int32, (tq, LANES), 1)
    cur = tt // SEL_LEN
    forced = (jj == 0) | ((jj <= cur) & (jj > cur - N_LOCAL_FORCED))
    imp = jnp.where(jj > cur, NEG, imp + jnp.where(forced, FORCED_BONUS, 0.0))
    v = imp.T[:nb]
    ridx = lax.broadcasted_iota(jnp.int32, (nb, tq), 0)
    sel = jnp.zeros((nb, tq), F32)
    for _ in range(min(N_SEL, nb)):
        mx = jnp.max(v, axis=0, keepdims=True)
        first = jnp.min(jnp.where(v == mx, ridx, nb), axis=0, keepdims=True)
        hit = ridx == first
        sel = jnp.where(hit, 1.0, sel)
        v = jnp.where(hit, -jnp.inf, v)
    bias_t = jnp.where(sel > 0.0, 0.0, NEG)
    if nb < LANES:
        bias_t = jnp.concatenate([bias_t, jnp.zeros((LANES - nb, tq), F32)], axis=0)
    bias_ref[0, 0] = bias_t.T.astype(bias_ref.dtype)


def _cmp_attn(qkv, kc, vc, overlap, gates, *, g_groups, r_heads, seq, ncmp):
    bsz = qkv.shape[0]
    nc = kc.shape[2]
    nb = seq // SEL_LEN
    tq = _tile(seq, (512, 256, 128))
    qw = r_heads * HEAD_DIM
    body = functools.partial(_cmp_body, tq=tq, r_heads=r_heads, nb=nb, ncmp=ncmp)
    return pl.pallas_call(
        body,
        grid=(bsz, g_groups, seq // tq),
        in_specs=[pl.BlockSpec((1, tq, qw), lambda b, g, i: (b, i, g)),
                  pl.BlockSpec((1, 1, nc, HEAD_DIM), lambda b, g, i: (b, g, 0, 0)),
                  pl.BlockSpec((1, 1, nc, HEAD_DIM), lambda b, g, i: (b, g, 0, 0)),
                  pl.BlockSpec((nc, LANES), lambda b, g, i: (0, 0)),
                  pl.BlockSpec((1, tq, LANES), lambda b, g, i: (b, i, 1))],
        out_specs=(pl.BlockSpec((1, tq, qw), lambda b, g, i: (b, i, g)),
                   pl.BlockSpec((1, 1, tq, LANES), lambda b, g, i: (b, g, i, 0))),
        out_shape=(jax.ShapeDtypeStruct((bsz, seq, g_groups * qw), F32),
                   jax.ShapeDtypeStruct((bsz, g_groups, seq, LANES), CDT)),
        compiler_params=_params(("arbitrary", "arbitrary", "arbitrary")),
        name="cmp_attn",
    )(qkv, kc, vc, overlap, gates)


ACC_ROWS = HEAD_DIM + 16


def _kv_step(kmat, q, vt, m_ref, acc_ref, mask):
    s = lax.dot_general(kmat, q, (((1,), (1,)), ((), ())), preferred_element_type=F32)
    if mask is not None:
        s = jnp.where(mask, s, NEG)
    m_prev = m_ref[...]
    m_new = jnp.maximum(m_prev, jnp.max(s, axis=0, keepdims=True))
    p = jnp.exp2(s - m_new).astype(CDT)
    ones_row = jnp.where(lax.broadcasted_iota(jnp.int32, (ACC_ROWS - HEAD_DIM, vt.shape[1]), 0) == 0,
                         1.0, 0.0).astype(CDT)
    vaug = jnp.concatenate([vt, ones_row], axis=0)
    acc_ref[...] = (jnp.exp2(m_prev - m_new) * acc_ref[...]
                    + jnp.dot(vaug, p, preferred_element_type=F32))
    m_ref[...] = m_new


def _flash_init(m_ref, acc_ref):
    m_ref[...] = jnp.full(m_ref.shape, -jnp.inf, F32)
    acc_ref[...] = jnp.zeros(acc_ref.shape, F32)


NSA_CHAIN_HEADS = 4


def _nsa_finalize(acc_ref, gate_ref, prev_ref, o_ref, gi, tq, r_heads, hpc, branch):
    gates_t = gate_ref[0].T
    rid = lax.broadcasted_iota(jnp.int32, gates_t.shape, 0)
    for r in range(r_heads):
        acc = acc_ref[r // hpc]
        cs = slice((r % hpc) * tq, (r % hpc + 1) * tq)
        col = (gi * r_heads + r) * 3 + branch
        grow = jnp.sum(jnp.where(rid == col, gates_t, 0.0), axis=0, keepdims=True)
        o = acc[:HEAD_DIM, cs] * (grow / acc[HEAD_DIM:HEAD_DIM + 1, cs])
        sl = slice(r * HEAD_DIM, (r + 1) * HEAD_DIM)
        o_ref[0, :, sl] = (prev_ref[0, :, sl] + o.T).astype(o_ref.dtype)


def _sel_body(q_ref, bias_ref, k_ref, e_ref, vt_ref, gate_ref, prev_ref, o_ref, qc_ref, m_ref, acc_ref,
              *, tq, tk, r_heads, hpc):
    gi = pl.program_id(1)
    i = pl.program_id(2)
    cols = hpc * tq
    for r in range(r_heads):
        qc_ref[r * tq:(r + 1) * tq, :] = jnp.concatenate(
            [q_ref[0, :, r * HEAD_DIM:(r + 1) * HEAD_DIM], bias_ref[0, 0]], axis=1)
    _flash_init(m_ref, acc_ref)
    n_full = (i * tq) // tk

    def step(j, mask):
        off = pl.multiple_of(j * tk, tk)
        kmat = jnp.concatenate([k_ref[0, pl.ds(off, tk), :], e_ref[pl.ds(off, tk), :]], axis=1)
        for c in range(r_heads // hpc):
            _kv_step(kmat, qc_ref[c * cols:(c + 1) * cols, :], vt_ref[0, 0, j], m_ref.at[c],
                     acc_ref.at[c], mask)

    def full_tile(j, carry):
        step(j, None)
        return carry

    lax.fori_loop(0, n_full, full_tile, 0)
    key = lax.broadcasted_iota(jnp.int32, (tk, cols), 0)
    tok = lax.broadcasted_iota(jnp.int32, (tk, cols), 1) & (tq - 1)
    for dd in range(max(1, tq // tk)):
        j = n_full + dd
        step(j, key + (j * tk - i * tq) <= tok)
    _nsa_finalize(acc_ref, gate_ref, prev_ref, o_ref, gi, tq, r_heads, hpc, 1)


def _win_body(q_ref, k_ref, vt_ref, gate_ref, prev_ref, o_ref, qc_ref, m_ref, acc_ref,
              *, tq, r_heads, hpc, n_tiles):
    gi = pl.program_id(1)
    i = pl.program_id(2)
    tk = tq
    cols = hpc * tq
    for r in range(r_heads):
        qc_ref[r * tq:(r + 1) * tq, :] = q_ref[0, :, r * HEAD_DIM:(r + 1) * HEAD_DIM]
    _flash_init(m_ref, acc_ref)
    key = lax.broadcasted_iota(jnp.int32, (tk, cols), 0)
    tok = lax.broadcasted_iota(jnp.int32, (tk, cols), 1) & (tq - 1)
    for dd in range(n_tiles):
        back = n_tiles - 1 - dd

        @pl.when(i >= back)
        def _():
            j = i - back
            off = pl.multiple_of(j * tk, tk)
            dist = tok - key + back * tk
            mask = (dist >= 0) & (dist < WINDOW)
            for c in range(r_heads // hpc):
                _kv_step(k_ref[0, pl.ds(off, tk), :], qc_ref[c * cols:(c + 1) * cols, :], vt_ref[0, 0, j],
                         m_ref.at[c], acc_ref.at[c], mask)

    _nsa_finalize(acc_ref, gate_ref, prev_ref, o_ref, gi, tq, r_heads, hpc, 2)


def _nsa_attn(mode, qkv, k_blk0, vt, gates, prev, out_dtype, *, g_groups, r_heads, seq,
              e_mat=None, bias=None):
    bsz = qkv.shape[0]
    qw = r_heads * HEAD_DIM
    tk = vt.shape[-1]
    kspec = pl.BlockSpec((1, seq, HEAD_DIM), lambda b, g, i: (b, 0, k_blk0 + g))
    vspec = pl.BlockSpec((1, 1) + vt.shape[2:], lambda b, g, i: (b, g, 0, 0, 0))
    hpc = NSA_CHAIN_HEADS if r_heads % NSA_CHAIN_HEADS == 0 else 1
    if mode == "sel":
        tq = _tile(seq, (512, 256, 128))
        qk_w = 2 * HEAD_DIM
        body = functools.partial(_sel_body, tq=tq, tk=tk, r_heads=r_heads, hpc=hpc)
    else:
        tq = tk
        qk_w = HEAD_DIM
        body = functools.partial(_win_body, tq=tq, r_heads=r_heads, hpc=hpc,
                                 n_tiles=(WINDOW + tk - 1) // tk + 1)
    qspec = pl.BlockSpec((1, tq, qw), lambda b, g, i: (b, i, g))
    gspec = pl.BlockSpec((1, tq, LANES), lambda b, g, i: (b, i, 1))
    if mode == "sel":
        in_specs = [qspec, pl.BlockSpec((1, 1, tq, LANES), lambda b, g, i: (b, g, i, 0)), kspec,
                    pl.BlockSpec((seq, LANES), lambda b, g, i: (0, 0)), vspec, gspec, qspec]
        ins = [qkv, bias, qkv, e_mat, vt, gates, prev]
    else:
        in_specs = [qspec, kspec, vspec, gspec, qspec]
        ins = [qkv, qkv, vt, gates, prev]
    return pl.pallas_call(
        body,
        grid=(bsz, g_groups, seq // tq),
        in_specs=in_specs,
        out_specs=qspec,
        out_shape=jax.ShapeDtypeStruct((bsz, seq, g_groups * qw), out_dtype),
        scratch_shapes=[pltpu.VMEM((r_heads * tq, qk_w), CDT),
                        pltpu.VMEM((r_heads // hpc, 1, hpc * tq), F32),
                        pltpu.VMEM((r_heads // hpc, ACC_ROWS, hpc * tq), F32)],
        compiler_params=_params(("arbitrary",) * 3),
        name="nsa_" + mode,
    )(*ins)


def _mla_body(q_ref, kn_ref, kr_ref, vt_ref, o_ref, m_ref, acc_ref, *, tq, tk, hg):
    i = pl.program_id(2)
    _flash_init(m_ref, acc_ref)
    n_full = (i * tq) // tk

    def step(j, mask, c0):
        off = pl.multiple_of(j * tk, tk)
        kr = kr_ref[0, pl.ds(off, tk), :]
        for c in range(hg):
            kmat = jnp.concatenate([kn_ref[0, pl.ds(off, tk), c * MLA_NOPE:(c + 1) * MLA_NOPE], kr], axis=1)
            _kv_step(kmat, q_ref[0, c0:, c * MLA_QK_PAD:(c + 1) * MLA_QK_PAD], vt_ref[0, c, j],
                     m_ref.at[c, :, c0:], acc_ref.at[c, :, c0:], mask)

    def full_tile(j, carry):
        step(j, None, 0)
        return carry

    lax.fori_loop(0, n_full, full_tile, 0)
    for dd in range(tq // tk):
        w = tq - dd * tk
        mask = (lax.broadcasted_iota(jnp.int32, (tk, w), 0) <= lax.broadcasted_iota(jnp.int32, (tk, w), 1))
        step(n_full + dd, mask, dd * tk)
    for c in range(hg):
        acc = acc_ref[c]
        o = acc[:HEAD_DIM] * (1.0 / acc[HEAD_DIM:HEAD_DIM + 1])
        o_ref[0, :, c * MLA_V:(c + 1) * MLA_V] = o.T.astype(o_ref.dtype)


def _mla_attn(q, kv, kr, vt, *, heads, seq):
    bsz = q.shape[0]
    tk = vt.shape[-1]
    tq = _tile(seq, (1024, 512, 256, 128))
    hg = 4 if heads % 4 == 0 else 1
    body = functools.partial(_mla_body, tq=tq, tk=tk, hg=hg)
    return pl.pallas_call(
        body,
        grid=(bsz, heads // hg, seq // tq),
        in_specs=[pl.BlockSpec((1, tq, hg * MLA_QK_PAD), lambda b, h, i: (b, i, h)),
                  pl.BlockSpec((1, seq, hg * MLA_NOPE), lambda b, h, i: (b, 0, h)),
                  pl.BlockSpec((1, seq, LANES), lambda b, h, i: (b, 0, 0)),
                  pl.BlockSpec((1, hg) + vt.shape[2:], lambda b, h, i: (b, h, 0, 0, 0))],
        out_specs=pl.BlockSpec((1, tq, hg * MLA_V), lambda b, h, i: (b, i, h)),
        out_shape=jax.ShapeDtypeStruct((bsz, seq, heads * MLA_V), CDT),
        scratch_shapes=[pltpu.VMEM((hg, 1, tq), F32), pltpu.VMEM((hg, ACC_ROWS, tq), F32)],
        compiler_params=_params(("arbitrary",) * 3),
        name="mla_attn",
    )(q, kv, kr, vt)


def _rope_tables(positions, rot_dim, period, offset):
    half = rot_dim // 2
    inv = jnp.power(ROPE_THETA, -jnp.arange(0, rot_dim, 2, dtype=F32) / rot_dim)
    ang = positions.astype(F32).reshape(-1)[:, None] * inv
    cos, sin = jnp.cos(ang), jnp.sin(ang)
    t = ang.shape[0]
    ones_l = jnp.ones((t, offset), F32)
    zeros_l = jnp.zeros((t, offset), F32)
    rest = period - offset - rot_dim
    ct = jnp.concatenate([ones_l, cos, cos, jnp.ones((t, rest), F32)], axis=1)
    sa = jnp.concatenate([zeros_l, -sin, jnp.zeros((t, half + rest), F32)], axis=1)
    sb = jnp.concatenate([zeros_l, jnp.zeros((t, half), F32), sin, jnp.zeros((t, rest), F32)], axis=1)
    return ct, sa, sb


def _table_extras(tables, tm):
    return [(tb, pl.BlockSpec((tm, tb.shape[1]), lambda i, j: (i, 0))) for tb in tables]


def _layer(x2, mod, positions, w_in, pos_k, pos_v, k1, k2, v1, v2, q_norm, kv_norm, w_uq, w_ukv,
           w_out, ln1_g, ln1_b, w_ff1, w_ff2, ln2_g, ln2_b, *, bsz, seq, alpha):
    t, d = x2.shape
    n_heads = d // HEAD_DIM
    nsa_h = n_heads // 2
    mla_h = n_heads - nsa_h
    q_rank = q_norm.shape[0]
    kv_rank = kv_norm.shape[0]
    d_in = w_in.shape[1]
    g_groups = (d_in - nsa_h * HEAD_DIM - nsa_h * 3 - q_rank - kv_rank - MLA_ROPE) // (6 * HEAD_DIM)
    r_heads = nsa_h // g_groups
    gw = g_groups * HEAD_DIM
    qw = nsa_h * HEAD_DIM
    sh_a, sc_a, g_a, sh_m, sc_m, g_m = jnp.split(mod, N_ADA, axis=-1)

    o_q, o_kv = 0, qw
    o_gate = o_kv + 6 * gw
    o_cq = o_gate + nsa_h * 3
    o_ckv = o_cq + q_rank
    o_kr = o_ckv + kv_rank
    kvcol = lambda idx: w_in[:, o_kv + idx * gw:o_kv + (idx + 1) * gw]
    w1c = jnp.concatenate([w_in[:, o_q:o_q + qw], kvcol(0), kvcol(2), kvcol(4), kvcol(3), kvcol(5)],
                          axis=1).astype(CDT)
    w2c = jnp.concatenate([w_in[:, o_cq:o_cq + q_rank], w_in[:, o_ckv:o_ckv + kv_rank], kvcol(1)],
                          axis=1).astype(CDT)
    zpad = lambda n: jnp.zeros((d, n), w_in.dtype)
    w3c = jnp.concatenate([w_in[:, o_kr:o_kr + MLA_ROPE], zpad(LANES - MLA_ROPE),
                           w_in[:, o_gate:o_gate + nsa_h * 3], zpad(LANES - nsa_h * 3)],
                          axis=1).astype(CDT)

    h = _modcast(x2, sc_a, sh_a, seq)

    tm = _tile(t, (1024, 512, 256, 128))
    nsa_tabs = _rope_tables(positions, PARTIAL_ROT, LANES, 0)
    n1 = w1c.shape[1]
    tn1 = _tile(gw, (512, 256, 128))
    n_rope_cols = qw + 3 * gw
    scale = HEAD_DIM ** -0.5 * LOG2E
    p1 = _mm([h], w1c, tm=tm, tn=tn1, out_dtype=CDT, name="in_proj_rot",
             epilogue=_rope_epilogue(PARTIAL_ROT // 2, n_rope_cols // tn1, qw // tn1, scale),
             extras=_table_extras(nsa_tabs, tm))
    n2 = w2c.shape[1]
    tn2 = _tile(n2, (512, 256, 128))
    p2 = _mm([h], w2c, tm=tm, tn=tn2, out_dtype=F32, name="in_proj_lat")
    kr_tabs = _rope_tables(positions, MLA_ROPE, 2 * LANES, 0)
    p3 = _mm([h], w3c, tm=tm, tn=2 * LANES, out_dtype=F32, name="in_proj_kr_gate",
             epilogue=_krope_gate_epilogue, extras=_table_extras(kr_tabs, tm))

    qkv = p1.reshape(bsz, seq, n1)
    gates = p3.reshape(bsz, seq, 2 * LANES)

    nchunk = seq // CMP_STRIDE
    ncmp = (seq - CMP_LEN) // CMP_STRIDE + 1
    def chunked(a):
        return (a.reshape(bsz, nchunk, CMP_STRIDE, g_groups, HEAD_DIM).transpose(0, 3, 1, 2, 4)
                .reshape(bsz, g_groups, nchunk, CMP_STRIDE * HEAD_DIM))
    kc = _compress(chunked(qkv[:, :, qw:qw + gw]), pos_k, k1, k2)
    v_cmp = p2.reshape(bsz, seq, n2)[:, :, q_rank + kv_rank:]
    vc = _compress(chunked(v_cmp), pos_v, v1, v2)

    nb = seq // SEL_LEN
    c_start = np.arange(nchunk) * CMP_STRIDE
    b_start = np.arange(LANES) * SEL_LEN
    overlap = ((c_start[:, None] < b_start[None, :] + SEL_LEN) &
               (c_start[:, None] + CMP_LEN > b_start[None, :]) &
               (np.arange(nchunk)[:, None] < ncmp) & (np.arange(LANES)[None, :] < nb))
    overlap = jnp.asarray(overlap.astype(np.float32))
    o_cmp, bias = _cmp_attn(qkv, kc, vc, overlap, gates, g_groups=g_groups, r_heads=r_heads,
                            seq=seq, ncmp=ncmp)

    e_mat = jnp.asarray((np.arange(seq)[:, None] // SEL_LEN == np.arange(LANES)[None, :])
                        .astype(np.float32)).astype(CDT)
    blk = lambda col: col // HEAD_DIM
    tk = _tile(seq, (512, 256, 128))
    def transposed_tiles(a, heads, tk=tk):
        return a.reshape(bsz, seq // tk, tk, heads, HEAD_DIM).transpose(0, 3, 1, 4, 2)
    tk_big = _tile(seq, (512, 256, 128))
    vt_sel = transposed_tiles(qkv[:, :, qw + 3 * gw:qw + 4 * gw], g_groups, tk_big)
    vt_win = transposed_tiles(qkv[:, :, qw + 4 * gw:qw + 5 * gw], g_groups)
    o_sel = _nsa_attn("sel", qkv, blk(qw + gw), vt_sel, gates, o_cmp, F32,
                      g_groups=g_groups, r_heads=r_heads, seq=seq, e_mat=e_mat, bias=bias)
    o_nsa = _nsa_attn("win", qkv, blk(qw + 2 * gw), vt_win, gates, o_sel, CDT,
                      g_groups=g_groups, r_heads=r_heads, seq=seq)

    qk_dim = MLA_NOPE + MLA_ROPE
    wq = w_uq.reshape(q_rank, mla_h, qk_dim)
    wq = jnp.concatenate([wq, jnp.zeros((q_rank, mla_h, MLA_QK_PAD - qk_dim), wq.dtype)], axis=-1)
    wq = wq.reshape(q_rank, mla_h * MLA_QK_PAD).astype(CDT)
    wkv = w_ukv.reshape(kv_rank, mla_h, MLA_NOPE + MLA_V)
    wkv = jnp.concatenate([wkv[:, :, :MLA_NOPE].reshape(kv_rank, mla_h * MLA_NOPE),
                           wkv[:, :, MLA_NOPE:].reshape(kv_rank, mla_h * MLA_V)], axis=1).astype(CDT)
    q_tabs = _rope_tables(positions, MLA_ROPE, MLA_QK_PAD, MLA_NOPE)
    tnq = _tile(mla_h * MLA_QK_PAD, (512, 256))
    q_mla = _mm([p2], wq, tm=tm, tn=tnq, out_dtype=CDT, name="mla_uq",
                a_specs=[pl.BlockSpec((tm, q_rank), lambda i, j: (i, 0))],
                prologue=_rms_prologue(1e-6),
                epilogue=_rope_epilogue(MLA_ROPE // 2, mla_h * MLA_QK_PAD // tnq,
                                        mla_h * MLA_QK_PAD // tnq, qk_dim ** -0.5 * LOG2E),
                extras=_table_extras(q_tabs, tm)
                + [(q_norm.reshape(1, q_rank), pl.BlockSpec((1, q_rank), lambda i, j: (0, 0)))])
    tnkv = _tile(wkv.shape[1], (512, 256, 128))
    kv_mla = _mm([p2], wkv, tm=tm, tn=tnkv, out_dtype=CDT, name="mla_ukv",
                 a_specs=[pl.BlockSpec((tm, kv_rank), lambda i, j: (i, q_rank // kv_rank))],
                 prologue=_rms_prologue(1e-6),
                 extras=[(kv_norm.reshape(1, kv_rank), pl.BlockSpec((1, kv_rank), lambda i, j: (0, 0)))])
    kv3 = kv_mla.reshape(bsz, seq, -1)
    o_mla = _mla_attn(q_mla.reshape(bsz, seq, -1), kv3, gates[:, :, :LANES].astype(CDT),
                      transposed_tiles(kv3[:, :, mla_h * MLA_NOPE:], mla_h, tk_big),
                      heads=mla_h, seq=seq)

    tno = _tile(d, (1024, 512, 256, 128))
    a = _mm([o_nsa.reshape(t, qw), o_mla.reshape(t, mla_h * MLA_V)], w_out.astype(CDT), tm=tm, tn=tno,
            out_dtype=F32, name="out_proj")
    x1, h2 = _res_ln(x2, a, g_a, ln1_g, ln1_b, seq, alpha, mod=(sc_m, sh_m))
    d_ff = w_ff1.shape[1]
    f1 = _mm([h2], w_ff1.astype(CDT), tm=tm, tn=_tile(d_ff, (1024, 512, 256, 128)), out_dtype=CDT,
             name="ff1", epilogue=_relu2_epilogue)
    f2 = _mmk(f1, w_ff2.astype(CDT), tm=tm, tn=tno, tk=_tile(d_ff, (4096, 2048, 1024, 512)),
              out_dtype=F32, name="ff2")
    return _res_ln(x1, f2, g_m, ln2_g, ln2_b, seq, alpha)


def kernel(x, c, positions, w_ada, b_ada, w_in, nsa_pos_k, nsa_pos_v, nsa_cmp_k1, nsa_cmp_k2, nsa_cmp_v1, nsa_cmp_v2, mla_q_norm, mla_kv_norm, mla_w_uq, mla_w_ukv, w_out, ln1_g, ln1_b, w_ff1, w_ff2, ln2_g, ln2_b):
    bsz, seq, d = x.shape
    depth = w_ada.shape[0]
    alpha = (2 * depth) ** 0.25
    x2 = x.reshape(bsz * seq, d)
    for layer in range(depth):
        mod = _ada(c, w_ada[layer], b_ada[layer])
        x2 = _layer(x2, mod, positions, w_in[layer], nsa_pos_k[layer], nsa_pos_v[layer],
                    nsa_cmp_k1[layer], nsa_cmp_k2[layer], nsa_cmp_v1[layer], nsa_cmp_v2[layer],
                    mla_q_norm[layer], mla_kv_norm[layer], mla_w_uq[layer], mla_w_ukv[layer],
                    w_out[layer], ln1_g[layer], ln1_b[layer], w_ff1[layer], w_ff2[layer],
                    ln2_g[layer], ln2_b[layer], bsz=bsz, seq=seq, alpha=alpha)
    return x2.reshape(bsz, seq, d)
```

```python
import functools
import math

import numpy as np
import jax
import jax.numpy as jnp
from jax import lax
from jax.experimental import pallas as pl
from jax.experimental.pallas import tpu as pltpu

HEAD_DIM = 128
CMP_LEN = 32
CMP_STRIDE = 16
SEL_LEN = 64
N_SEL = 16
N_LOCAL_FORCED = 2
FORCED_BONUS = 1e4
WINDOW = 512
MLA_NOPE = 128
MLA_ROPE = 64
MLA_V = 128
ROPE_THETA = 500000.0
PARTIAL_ROT = HEAD_DIM // 4
NEG = -1e30
LOG2E = math.log2(math.e)
N_ADA = 6
LANES = 128
MLA_QK_PAD = 2 * LANES
VMEM_LIMIT = 56 * 1024 * 1024
MM_SUB_COLS = 256

F32 = jnp.float32
CDT = jnp.bfloat16


def _params(sem):
    return pltpu.CompilerParams(dimension_semantics=sem, vmem_limit_bytes=VMEM_LIMIT)


def _tile(n, cands):
    for c in cands:
        if n % c == 0:
            return c
    return n


def _ada_body(c_ref, w_ref, b_ref, o_ref):
    c = c_ref[...]
    cond = c * (1.0 / (1.0 + jnp.exp(-c)))
    o_ref[...] = jnp.dot(cond, w_ref[...], preferred_element_type=F32,
                         precision=lax.Precision.HIGHEST) + b_ref[...]


def _ada(c, w, b):
    bsz, d = c.shape
    n = w.shape[1]
    tn = _tile(n, (512, 256, 128))
    return pl.pallas_call(
        _ada_body,
        grid=(n // tn,),
        in_specs=[pl.BlockSpec((bsz, d), lambda j: (0, 0)),
                  pl.BlockSpec((d, tn), lambda j: (0, j)),
                  pl.BlockSpec((1, tn), lambda j: (0, j))],
        out_specs=pl.BlockSpec((bsz, tn), lambda j: (0, j)),
        out_shape=jax.ShapeDtypeStruct((bsz, n), F32),
        compiler_params=_params(("arbitrary",)),
        name="ada",
    )(c, w, b.reshape(1, n))


def _modcast_body(x_ref, sc_ref, sh_ref, o_ref):
    o_ref[...] = (x_ref[...] * (1.0 + sc_ref[0]) + sh_ref[0]).astype(o_ref.dtype)


def _modcast(x2, sc, sh, seq):
    t, d = x2.shape
    ts = _tile(seq, (256, 128, 64, 8))
    per = seq // ts
    vec = pl.BlockSpec((1, 1, d), lambda i: (i // per, 0, 0))
    return pl.pallas_call(
        _modcast_body,
        grid=(t // ts,),
        in_specs=[pl.BlockSpec((ts, d), lambda i: (i, 0)), vec, vec],
        out_specs=pl.BlockSpec((ts, d), lambda i: (i, 0)),
        out_shape=jax.ShapeDtypeStruct((t, d), CDT),
        compiler_params=_params(("arbitrary",)),
        name="modcast",
    )(x2, sc[:, None, :], sh[:, None, :])


def _mm_body(*refs, na, nex, prologue, epilogue):
    a_refs = refs[:na]
    b_ref = refs[na]
    ex = refs[na + 1:na + 1 + nex]
    o_ref = refs[na + 1 + nex]
    j = pl.program_id(1)
    if prologue is not None:
        a_sc = refs[na + 2 + nex]

        @pl.when(j == 0)
        def _():
            a_sc[...] = prologue(a_refs[0][...], ex).astype(a_sc.dtype)

        a_refs = (a_sc,)
    tn = o_ref.shape[1]
    sub = MM_SUB_COLS if tn % MM_SUB_COLS == 0 else tn

    def run(fn):
        for s in range(tn // sub):
            cols = slice(s * sub, (s + 1) * sub)
            acc = None
            off = 0
            for r in a_refs:
                kr = r.shape[1]
                part = jnp.dot(r[...], b_ref[off:off + kr, cols], preferred_element_type=F32)
                acc = part if acc is None else acc + part
                off += kr
            o_ref[:, cols] = fn(acc, j, ex).astype(o_ref.dtype)

    if epilogue is None:
        run(lambda acc, j, ex: acc)
    else:
        for cond, fn in epilogue(j):
            pl.when(cond)(functools.partial(run, fn))


def _mm(a_list, b, *, tm, tn, out_dtype, name, a_specs=None, prologue=None, epilogue=None,
        extras=()):
    m = a_list[0].shape[0]
    k, n = b.shape
    if a_specs is None:
        a_specs = [pl.BlockSpec((tm, a.shape[1]), lambda i, j: (i, 0)) for a in a_list]
    ex_arrays = [e[0] for e in extras]
    ex_specs = [e[1] for e in extras]
    scratch = [pltpu.VMEM((tm, k), CDT)] if prologue is not None else []
    body = functools.partial(_mm_body, na=len(a_list), nex=len(extras), prologue=prologue,
                             epilogue=epilogue)
    return pl.pallas_call(
        body,
        grid=(m // tm, n // tn),
        in_specs=a_specs + [pl.BlockSpec((k, tn), lambda i, j: (0, j))] + ex_specs,
        out_specs=pl.BlockSpec((tm, tn), lambda i, j: (i, j)),
        out_shape=jax.ShapeDtypeStruct((m, n), out_dtype),
        scratch_shapes=scratch,
        compiler_params=_params(("arbitrary", "arbitrary")),
        name=name,
    )(*a_list, b, *ex_arrays)


def _rot(acc, cos, sa, sb, shift):
    tn = acc.shape[1]
    reps = tn // cos.shape[1]

    def rep(t):
        return t if reps == 1 else jnp.concatenate([t] * reps, axis=1)

    return (acc * rep(cos) + pltpu.roll(acc, tn - shift, 1) * rep(sa)
            + pltpu.roll(acc, shift, 1) * rep(sb))


def _rope_epilogue(shift, n_rope, n_scaled, scale):
    def roped(acc, j, ex):
        y = _rot(acc, ex[0][...], ex[1][...], ex[2][...], shift)
        return y * jnp.where(j < n_scaled, scale, 1.0).astype(F32)

    def epi(j):
        return [(j < n_rope, roped), (j >= n_rope, lambda acc, j, ex: acc)]

    return epi


def _krope_gate_epilogue(j):
    def fn(acc, j, ex):
        y = _rot(acc, ex[0][...], ex[1][...], ex[2][...], MLA_ROPE // 2)
        lane = lax.broadcasted_iota(jnp.int32, acc.shape, 1)
        return jnp.where(lane < LANES, y, 1.0 / (1.0 + jnp.exp(-acc)))

    return [(j >= 0, fn)]


def _rms_prologue(eps):
    def pro(a, ex):
        g = ex[-1][...]
        y = a * lax.rsqrt(jnp.mean(a * a, axis=-1, keepdims=True) + eps)
        return y * g

    return pro


def _relu2_epilogue(j):
    def fn(acc, j, ex):
        r = jnp.maximum(acc, 0.0)
        return r * r

    return [(j >= 0, fn)]


def _mmk_body(a_ref, b_ref, o_ref, acc_ref):
    kk = pl.program_id(2)

    @pl.when(kk == 0)
    def _():
        acc_ref[...] = jnp.zeros_like(acc_ref)

    acc_ref[...] += jnp.dot(a_ref[...], b_ref[...], preferred_element_type=F32)

    @pl.when(kk == pl.num_programs(2) - 1)
    def _():
        o_ref[...] = acc_ref[...].astype(o_ref.dtype)


def _mmk(a, b, *, tm, tn, tk, out_dtype, name):
    m, k = a.shape
    n = b.shape[1]
    return pl.pallas_call(
        _mmk_body,
        grid=(m // tm, n // tn, k // tk),
        in_specs=[pl.BlockSpec((tm, tk), lambda i, j, kk: (i, kk)),
                  pl.BlockSpec((tk, tn), lambda i, j, kk: (kk, j))],
        out_specs=pl.BlockSpec((tm, tn), lambda i, j, kk: (i, j)),
        out_shape=jax.ShapeDtypeStruct((m, n), out_dtype),
        scratch_shapes=[pltpu.VMEM((tm, tn), F32)],
        compiler_params=_params(("arbitrary", "arbitrary", "arbitrary")),
        name=name,
    )(a, b)


def _ln_body(x_ref, a_ref, gate_ref, g_ref, b_ref, *rest, alpha, with_mod):
    y = alpha * x_ref[...] + (1.0 + gate_ref[0]) * a_ref[...]
    mu = jnp.mean(y, axis=-1, keepdims=True)
    yc = y - mu
    var = jnp.mean(yc * yc, axis=-1, keepdims=True)
    out = yc * lax.rsqrt(var + 1e-5) * g_ref[...] + b_ref[...]
    if with_mod:
        sc_ref, sh_ref, o_ref, h_ref = rest
        o_ref[...] = out
        h_ref[...] = (out * (1.0 + sc_ref[0]) + sh_ref[0]).astype(h_ref.dtype)
    else:
        (o_ref,) = rest
        o_ref[...] = out


def _res_ln(x2, a2, gate, g, b, seq, alpha, mod=None):
    t, d = x2.shape
    ts = _tile(seq, (256, 128, 64, 8))
    per = seq // ts
    row = pl.BlockSpec((ts, d), lambda i: (i, 0))
    vec = pl.BlockSpec((1, 1, d), lambda i: (i // per, 0, 0))
    par = pl.BlockSpec((1, d), lambda i: (0, 0))
    ins = [x2, a2, gate[:, None, :], g.reshape(1, d), b.reshape(1, d)]
    specs = [row, row, vec, par, par]
    if mod is None:
        out_shape = jax.ShapeDtypeStruct((t, d), F32)
        out_specs = row
    else:
        ins += [mod[0][:, None, :], mod[1][:, None, :]]
        specs += [vec, vec]
        out_shape = (jax.ShapeDtypeStruct((t, d), F32), jax.ShapeDtypeStruct((t, d), CDT))
        out_specs = (row, row)
    return pl.pallas_call(
        functools.partial(_ln_body, alpha=alpha, with_mod=mod is not None),
        grid=(t // ts,),
        in_specs=specs,
        out_specs=out_specs,
        out_shape=out_shape,
        compiler_params=_params(("arbitrary",)),
        name="res_ln",
    )(*ins)


def _compress_body(x_ref, p0_ref, p1_ref, w1a_ref, w1b_ref, w2_ref, o_ref):
    x = x_ref[0, 0].astype(F32)
    nc = x.shape[0]
    xa = (x + p0_ref[...]).astype(CDT)
    xb = (x + p1_ref[...]).astype(CDT)
    a = jnp.dot(xa, w1a_ref[...], preferred_element_type=F32)
    bm = jnp.dot(xb, w1b_ref[...], preferred_element_type=F32)
    pre = a + pltpu.roll(bm, nc - 1, 0)
    hid = 0.5 * pre * (1.0 + jnp.tanh(math.sqrt(2.0 / math.pi) * (pre + 0.044715 * pre * pre * pre)))
    o_ref[0, 0] = jnp.dot(hid.astype(CDT), w2_ref[...], preferred_element_type=F32).astype(o_ref.dtype)


def _compress(xc, pos, w1, w2):
    bsz, g, nc, wid = xc.shape
    half = wid
    p0 = pos[:CMP_STRIDE].reshape(1, wid)
    p1 = pos[CMP_STRIDE:].reshape(1, wid)
    w1c = w1.astype(CDT)
    full = lambda shape: pl.BlockSpec(shape, lambda b, gi: (0,) * len(shape))
    return pl.pallas_call(
        _compress_body,
        grid=(bsz, g),
        in_specs=[pl.BlockSpec((1, 1, nc, wid), lambda b, gi: (b, gi, 0, 0)),
                  full((1, wid)), full((1, wid)),
                  full((half, w1.shape[1])), full((half, w1.shape[1])),
                  full(w2.shape)],
        out_specs=pl.BlockSpec((1, 1, nc, HEAD_DIM), lambda b, gi: (b, gi, 0, 0)),
        out_shape=jax.ShapeDtypeStruct((bsz, g, nc, HEAD_DIM), CDT),
        compiler_params=_params(("arbitrary", "arbitrary")),
        name="compress",
    )(xc, p0, p1, w1c[:half], w1c[half:], w2.astype(CDT))


def _cmp_body(q_ref, kc_ref, vc_ref, ov_ref, gate_ref, o_ref, bias_ref, *, tq, r_heads, nb, ncmp):
    gi = pl.program_id(1)
    i = pl.program_id(2)
    kc = kc_ref[0, 0]
    vc = vc_ref[0, 0]
    nc = kc.shape[0]
    t = i * tq + lax.broadcasted_iota(jnp.int32, (tq, nc), 0)
    cidx = lax.broadcasted_iota(jnp.int32, (tq, nc), 1)
    valid = (cidx * CMP_STRIDE + (CMP_LEN - 1) <= t) & (cidx < ncmp)
    validf = valid.astype(F32)
    real = cidx < ncmp
    gates = gate_ref[0]
    glane = lax.broadcasted_iota(jnp.int32, gates.shape, 1)
    psum = jnp.zeros((tq, nc), F32)
    for r in range(r_heads):
        q = q_ref[0, :, r * HEAD_DIM:(r + 1) * HEAD_DIM]
        s = lax.dot_general(q, kc, (((1,), (1,)), ((), ())), preferred_element_type=F32)
        s = jnp.where(valid, s, jnp.where(real, NEG, -jnp.inf))
        m = jnp.max(s, axis=-1, keepdims=True)
        e = jnp.exp2(s - m)
        p = e / jnp.sum(e, axis=-1, keepdims=True) * validf
        psum = psum + p
        o = jnp.dot(p.astype(CDT), vc, preferred_element_type=F32)
        col = ((gi * r_heads + r) * 3 + 0)
        gcol = jnp.sum(jnp.where(glane == col, gates, 0.0), axis=-1, keepdims=True)
        o_ref[0, :, r * HEAD_DIM:(r + 1) * HEAD_DIM] = (gcol * o).astype(o_ref.dtype)

    imp = jnp.dot(psum, ov_ref[...], preferred_element_type=F32, precision=lax.Precision.HIGHEST)
    tt = i * tq + lax.broadcasted_iota(jnp.int32, (tq, LANES), 0)
    jj = lax.broadcasted_iota(jnp.int32, (tq, LANES), 1)
    cur = tt // SEL_LEN
    forced = (jj == 0) | ((jj <= cur) & (jj > cur - N_LOCAL_FORCED))
    imp = jnp.where(jj > cur, NEG, imp + jnp.where(forced, FORCED_BONUS, 0.0))
    v = imp.T[:nb]
    ridx = lax.broadcasted_iota(jnp.int32, (nb, tq), 0)
    sel = jnp.zeros((nb, tq), F32)
    for _ in range(min(N_SEL, nb)):
        mx = jnp.max(v, axis=0, keepdims=True)
        first = jnp.min(jnp.where(v == mx, ridx, nb), axis=0, keepdims=True)
        hit = ridx == first
        sel = jnp.where(hit, 1.0, sel)
        v = jnp.where(hit, -jnp.inf, v)
    bias_t = jnp.where(sel > 0.0, 0.0, NEG)
    if nb < LANES:
        bias_t = jnp.concatenate([bias_t, jnp.zeros((LANES - nb, tq), F32)], axis=0)
    bias_ref[0, 0] = bias_t.T.astype(bias_ref.dtype)


def _cmp_attn(qkv, kc, vc, overlap, gates, *, g_groups, r_heads, seq, ncmp):
    bsz = qkv.shape[0]
    nc = kc.shape[2]
    nb = seq // SEL_LEN
    tq = _tile(seq, (512, 256, 128))
    qw = r_heads * HEAD_DIM
    body = functools.partial(_cmp_body, tq=tq, r_heads=r_heads, nb=nb, ncmp=ncmp)
    return pl.pallas_call(
        body,
        grid=(bsz, g_groups, seq // tq),
        in_specs=[pl.BlockSpec((1, tq, qw), lambda b, g, i: (b, i, g)),
                  pl.BlockSpec((1, 1, nc, HEAD_DIM), lambda b, g, i: (b, g, 0, 0)),
                  pl.BlockSpec((1, 1, nc, HEAD_DIM), lambda b, g, i: (b, g, 0, 0)),
                  pl.BlockSpec((nc, LANES), lambda b, g, i: (0, 0)),
                  pl.BlockSpec((1, tq, LANES), lambda b, g, i: (b, i, 1))],
        out_specs=(pl.BlockSpec((1, tq, qw), lambda b, g, i: (b, i, g)),
                   pl.BlockSpec((1, 1, tq, LANES), lambda b, g, i: (b, g, i, 0))),
        out_shape=(jax.ShapeDtypeStruct((bsz, seq, g_groups * qw), F32),
                   jax.ShapeDtypeStruct((bsz, g_groups, seq, LANES), CDT)),
        compiler_params=_params(("arbitrary", "arbitrary", "arbitrary")),
        name="cmp_attn",
    )(qkv, kc, vc, overlap, gates)


ACC_ROWS = HEAD_DIM + 16


def _with_ones_row(vt):
    ones_row = jnp.where(lax.broadcasted_iota(jnp.int32, (ACC_ROWS - HEAD_DIM, vt.shape[1]), 0) == 0,
                         1.0, 0.0).astype(CDT)
    return jnp.concatenate([vt, ones_row], axis=0)


def _flash_init(m_ref, acc_ref):
    m_ref[...] = jnp.full(m_ref.shape, -jnp.inf, F32)
    acc_ref[...] = jnp.zeros(acc_ref.shape, F32)


def _nsa_finalize(acc_ref, gate_ref, prev_ref, o_ref, gi, tq, r_heads, branch):
    gates_t = gate_ref[0].T
    rid = lax.broadcasted_iota(jnp.int32, gates_t.shape, 0)
    for r in range(r_heads):
        acc = acc_ref[0]
        cs = slice(r * tq, (r + 1) * tq)
        col = (gi * r_heads + r) * 3 + branch
        grow = jnp.sum(jnp.where(rid == col, gates_t, 0.0), axis=0, keepdims=True)
        o = acc[:HEAD_DIM, cs] * (grow / acc[HEAD_DIM:HEAD_DIM + 1, cs])
        sl = slice(r * HEAD_DIM, (r + 1) * HEAD_DIM)
        o_ref[0, :, sl] = (prev_ref[0, :, sl] + o.T).astype(o_ref.dtype)


def _sel_body(q_ref, bias_ref, k_ref, e_ref, vt_ref, gate_ref, prev_ref, o_ref, qc_ref, s0_ref, s1_ref,
              cm_ref, m_ref, acc_ref, *, tq, r_heads):
    s_refs = (s0_ref, s1_ref)
    gi = pl.program_id(1)
    i = pl.program_id(2)
    tk = tq
    cols = r_heads * tq
    for r in range(r_heads):
        qc_ref[r * tq:(r + 1) * tq, :] = jnp.concatenate(
            [q_ref[0, :, r * HEAD_DIM:(r + 1) * HEAD_DIM], bias_ref[0, 0]], axis=1)
    _flash_init(m_ref, acc_ref)

    def scores(j, buf, masked):
        off = pl.multiple_of(j * tk, tk)
        kmat = jnp.concatenate([k_ref[0, pl.ds(off, tk), :], e_ref[pl.ds(off, tk), :]], axis=1)
        s = lax.dot_general(kmat, qc_ref[...], (((1,), (1,)), ((), ())), preferred_element_type=F32)
        if masked:
            key = lax.broadcasted_iota(jnp.int32, (tk, cols), 0)
            tok = lax.broadcasted_iota(jnp.int32, (tk, cols), 1) & (tq - 1)
            s = jnp.where(key <= tok, s, NEG)
        s_refs[buf][...] = s
        cm_ref[buf] = jnp.max(s, axis=0, keepdims=True)

    def update(j, buf):
        m_prev = m_ref[0]
        m_new = jnp.maximum(m_prev, cm_ref[buf])
        p = jnp.exp2(s_refs[buf][...] - m_new).astype(CDT)
        acc_ref[0] = (jnp.exp2(m_prev - m_new) * acc_ref[0]
                      + jnp.dot(_with_ones_row(vt_ref[0, 0, j]), p, preferred_element_type=F32))
        m_ref[0] = m_new

    @pl.when(i == 0)
    def _():
        scores(0, 0, True)

    @pl.when(i > 0)
    def _():
        scores(0, 0, False)

    n_pairs = jnp.maximum(i - 1, 0) // 2

    def body(jp, carry):
        j = 2 * jp
        scores(j + 1, 1, False)
        update(j, 0)
        scores(j + 2, 0, False)
        update(j + 1, 1)
        return carry

    lax.fori_loop(0, n_pairs, body, 0)
    j0 = 2 * n_pairs
    left = i - j0

    @pl.when(left == 1)
    def _():
        scores(i, 1, True)
        update(j0, 0)
        update(i, 1)

    @pl.when(left == 2)
    def _():
        scores(j0 + 1, 1, False)
        update(j0, 0)
        scores(i, 0, True)
        update(j0 + 1, 1)
        update(i, 0)

    @pl.when(left == 0)
    def _():
        update(0, 0)

    _nsa_finalize(acc_ref, gate_ref, prev_ref, o_ref, gi, tq, r_heads, 1)


def _win_body(q_ref, k_ref, vt_ref, gate_ref, prev_ref, o_ref, qc_ref, s0_ref, s1_ref, cm_ref, m_ref,
              acc_ref, *, tq, r_heads, n_tiles):
    s_refs = (s0_ref, s1_ref)
    gi = pl.program_id(1)
    i = pl.program_id(2)
    tk = tq
    cols = r_heads * tq
    for r in range(r_heads):
        qc_ref[r * tq:(r + 1) * tq, :] = q_ref[0, :, r * HEAD_DIM:(r + 1) * HEAD_DIM]
    _flash_init(m_ref, acc_ref)

    def scores(dd, buf):
        back = n_tiles - 1 - dd
        off = pl.multiple_of((i - back) * tk, tk)
        s = lax.dot_general(k_ref[0, pl.ds(off, tk), :], qc_ref[...], (((1,), (1,)), ((), ())),
                            preferred_element_type=F32)
        key = lax.broadcasted_iota(jnp.int32, (tk, cols), 0)
        tok = lax.broadcasted_iota(jnp.int32, (tk, cols), 1) & (tq - 1)
        dist = tok - key + back * tk
        s = jnp.where((dist >= 0) & (dist < WINDOW), s, NEG)
        s_refs[buf][...] = s
        cm_ref[buf] = jnp.max(s, axis=0, keepdims=True)

    def update(dd, buf):
        j = i - (n_tiles - 1 - dd)
        m_prev = m_ref[0]
        m_new = jnp.maximum(m_prev, cm_ref[buf])
        p = jnp.exp2(s_refs[buf][...] - m_new).astype(CDT)
        acc_ref[0] = (jnp.exp2(m_prev - m_new) * acc_ref[0]
                      + jnp.dot(_with_ones_row(vt_ref[0, 0, j]), p, preferred_element_type=F32))
        m_ref[0] = m_new

    @pl.when(i >= n_tiles - 1)
    def _():
        scores(0, 0)
        for dd in range(1, n_tiles):
            scores(dd, dd % 2)
            update(dd - 1, (dd - 1) % 2)
        update(n_tiles - 1, (n_tiles - 1) % 2)

    for dd in range(1, n_tiles):
        @pl.when((i < n_tiles - 1) & (i >= n_tiles - 1 - dd))
        def _():
            scores(dd, 0)
            update(dd, 0)

    _nsa_finalize(acc_ref, gate_ref, prev_ref, o_ref, gi, tq, r_heads, 2)


def _nsa_attn(mode, qkv, k_blk0, vt, gates, prev, out_dtype, *, g_groups, r_heads, seq,
              e_mat=None, bias=None):
    bsz = qkv.shape[0]
    qw = r_heads * HEAD_DIM
    tk = vt.shape[-1]
    kspec = pl.BlockSpec((1, seq, HEAD_DIM), lambda b, g, i: (b, 0, k_blk0 + g))
    vspec = pl.BlockSpec((1, 1) + vt.shape[2:], lambda b, g, i: (b, g, 0, 0, 0))
    tq = tk
    cols = r_heads * tq
    if mode == "sel":
        qk_w = 2 * HEAD_DIM
        body = functools.partial(_sel_body, tq=tq, r_heads=r_heads)
    else:
        qk_w = HEAD_DIM
        body = functools.partial(_win_body, tq=tq, r_heads=r_heads, n_tiles=(WINDOW + tk - 1) // tk + 1)
    qspec = pl.BlockSpec((1, tq, qw), lambda b, g, i: (b, i, g))
    gspec = pl.BlockSpec((1, tq, LANES), lambda b, g, i: (b, i, 1))
    if mode == "sel":
        in_specs = [qspec, pl.BlockSpec((1, 1, tq, LANES), lambda b, g, i: (b, g, i, 0)), kspec,
                    pl.BlockSpec((seq, LANES), lambda b, g, i: (0, 0)), vspec, gspec, qspec]
        ins = [qkv, bias, qkv, e_mat, vt, gates, prev]
    else:
        in_specs = [qspec, kspec, vspec, gspec, qspec]
        ins = [qkv, qkv, vt, gates, prev]
    return pl.pallas_call(
        body,
        grid=(bsz, g_groups, seq // tq),
        in_specs=in_specs,
        out_specs=qspec,
        out_shape=jax.ShapeDtypeStruct((bsz, seq, g_groups * qw), out_dtype),
        scratch_shapes=[pltpu.VMEM((cols, qk_w), CDT),
                        pltpu.VMEM((tk, cols), F32), pltpu.VMEM((tk, cols), F32),
                        pltpu.VMEM((2, 1, cols), F32),
                        pltpu.VMEM((1, 1, cols), F32),
                        pltpu.VMEM((1, ACC_ROWS, cols), F32)],
        compiler_params=_params(("arbitrary",) * 3),
        name="nsa_" + mode,
    )(*ins)


def _mla_body(q_ref, kn_ref, kr_ref, vt_ref, o_ref, *rest, tq, tk, hg):
    s_refs = rest[:hg]
    cm_ref, m_ref, acc_ref = rest[hg:]
    i = pl.program_id(2)
    _flash_init(m_ref, acc_ref)
    n_full = (i * tq) // tk

    def step(j, mask, c0):
        off = pl.multiple_of(j * tk, tk)
        kr = kr_ref[0, pl.ds(off, tk), :]

        def scores(c):
            kmat = jnp.concatenate([kn_ref[0, pl.ds(off, tk), c * MLA_NOPE:(c + 1) * MLA_NOPE], kr], axis=1)
            s = lax.dot_general(kmat, q_ref[0, c0:, c * MLA_QK_PAD:(c + 1) * MLA_QK_PAD],
                                (((1,), (1,)), ((), ())), preferred_element_type=F32)
            if mask is not None:
                s = jnp.where(mask, s, NEG)
            s_refs[c][:, c0:] = s
            cm_ref[c, :, c0:] = jnp.max(s, axis=0, keepdims=True)

        def update(c):
            m_prev = m_ref[c, :, c0:]
            m_new = jnp.maximum(m_prev, cm_ref[c, :, c0:])
            p = jnp.exp2(s_refs[c][:, c0:] - m_new).astype(CDT)
            acc_ref[c, :, c0:] = (jnp.exp2(m_prev - m_new) * acc_ref[c, :, c0:]
                                  + jnp.dot(_with_ones_row(vt_ref[0, c, j]), p, preferred_element_type=F32))
            m_ref[c, :, c0:] = m_new

        scores(0)
        for c in range(1, hg):
            scores(c)
            update(c - 1)
        update(hg - 1)

    def full_tile(j, carry):
        step(j, None, 0)
        return carry

    lax.fori_loop(0, n_full, full_tile, 0)
    for dd in range(tq // tk):
        w = tq - dd * tk
        mask = (lax.broadcasted_iota(jnp.int32, (tk, w), 0) <= lax.broadcasted_iota(jnp.int32, (tk, w), 1))
        step(n_full + dd, mask, dd * tk)
    for c in range(hg):
        acc = acc_ref[c]
        o = acc[:HEAD_DIM] * (1.0 / acc[HEAD_DIM:HEAD_DIM + 1])
        o_ref[0, :, c * MLA_V:(c + 1) * MLA_V] = o.T.astype(o_ref.dtype)


def _mla_attn(q, kv, kr, vt, *, heads, seq):
    bsz = q.shape[0]
    tk = vt.shape[-1]
    tq = _tile(seq, (1024, 512, 256, 128))
    hg = 4 if heads % 4 == 0 else 1
    body = functools.partial(_mla_body, tq=tq, tk=tk, hg=hg)
    return pl.pallas_call(
        body,
        grid=(bsz, heads // hg, seq // tq),
        in_specs=[pl.BlockSpec((1, tq, hg * MLA_QK_PAD), lambda b, h, i: (b, i, h)),
                  pl.BlockSpec((1, seq, hg * MLA_NOPE), lambda b, h, i: (b, 0, h)),
                  pl.BlockSpec((1, seq, LANES), lambda b, h, i: (b, 0, 0)),
                  pl.BlockSpec((1, hg) + vt.shape[2:], lambda b, h, i: (b, h, 0, 0, 0))],
        out_specs=pl.BlockSpec((1, tq, hg * MLA_V), lambda b, h, i: (b, i, h)),
        out_shape=jax.ShapeDtypeStruct((bsz, seq, heads * MLA_V), CDT),
        scratch_shapes=[pltpu.VMEM((tk, tq), F32)] * hg + [
            pltpu.VMEM((hg, 1, tq), F32), pltpu.VMEM((hg, 1, tq), F32), pltpu.VMEM((hg, ACC_ROWS, tq), F32)],
        compiler_params=_params(("arbitrary",) * 3),
        name="mla_attn",
    )(q, kv, kr, vt)


def _rope_tables(positions, rot_dim, period, offset):
    half = rot_dim // 2
    inv = jnp.power(ROPE_THETA, -jnp.arange(0, rot_dim, 2, dtype=F32) / rot_dim)
    ang = positions.astype(F32).reshape(-1)[:, None] * inv
    cos, sin = jnp.cos(ang), jnp.sin(ang)
    t = ang.shape[0]
    ones_l = jnp.ones((t, offset), F32)
    zeros_l = jnp.zeros((t, offset), F32)
    rest = period - offset - rot_dim
    ct = jnp.concatenate([ones_l, cos, cos, jnp.ones((t, rest), F32)], axis=1)
    sa = jnp.concatenate([zeros_l, -sin, jnp.zeros((t, half + rest), F32)], axis=1)
    sb = jnp.concatenate([zeros_l, jnp.zeros((t, half), F32), sin, jnp.zeros((t, rest), F32)], axis=1)
    return ct, sa, sb


def _table_extras(tables, tm):
    return [(tb, pl.BlockSpec((tm, tb.shape[1]), lambda i, j: (i, 0))) for tb in tables]


def _layer(x2, mod, positions, w_in, pos_k, pos_v, k1, k2, v1, v2, q_norm, kv_norm, w_uq, w_ukv,
           w_out, ln1_g, ln1_b, w_ff1, w_ff2, ln2_g, ln2_b, *, bsz, seq, alpha):
    t, d = x2.shape
    n_heads = d // HEAD_DIM
    nsa_h = n_heads // 2
    mla_h = n_heads - nsa_h
    q_rank = q_norm.shape[0]
    kv_rank = kv_norm.shape[0]
    d_in = w_in.shape[1]
    g_groups = (d_in - nsa_h * HEAD_DIM - nsa_h * 3 - q_rank - kv_rank - MLA_ROPE) // (6 * HEAD_DIM)
    r_heads = nsa_h // g_groups
    gw = g_groups * HEAD_DIM
    qw = nsa_h * HEAD_DIM
    sh_a, sc_a, g_a, sh_m, sc_m, g_m = jnp.split(mod, N_ADA, axis=-1)

    o_q, o_kv = 0, qw
    o_gate = o_kv + 6 * gw
    o_cq = o_gate + nsa_h * 3
    o_ckv = o_cq + q_rank
    o_kr = o_ckv + kv_rank
    kvcol = lambda idx: w_in[:, o_kv + idx * gw:o_kv + (idx + 1) * gw]
    w1c = jnp.concatenate([w_in[:, o_q:o_q + qw], kvcol(0), kvcol(2), kvcol(4), kvcol(3), kvcol(5)],
                          axis=1).astype(CDT)
    w2c = jnp.concatenate([w_in[:, o_cq:o_cq + q_rank], w_in[:, o_ckv:o_ckv + kv_rank], kvcol(1)],
                          axis=1).astype(CDT)
    zpad = lambda n: jnp.zeros((d, n), w_in.dtype)
    w3c = jnp.concatenate([w_in[:, o_kr:o_kr + MLA_ROPE], zpad(LANES - MLA_ROPE),
                           w_in[:, o_gate:o_gate + nsa_h * 3], zpad(LANES - nsa_h * 3)],
                          axis=1).astype(CDT)

    h = _modcast(x2, sc_a, sh_a, seq)

    tm = _tile(t, (1024, 512, 256, 128))
    nsa_tabs = _rope_tables(positions, PARTIAL_ROT, LANES, 0)
    n1 = w1c.shape[1]
    tn1 = _tile(gw, (512, 256, 128))
    n_rope_cols = qw + 3 * gw
    scale = HEAD_DIM ** -0.5 * LOG2E
    p1 = _mm([h], w1c, tm=tm, tn=tn1, out_dtype=CDT, name="in_proj_rot",
             epilogue=_rope_epilogue(PARTIAL_ROT // 2, n_rope_cols // tn1, qw // tn1, scale),
             extras=_table_extras(nsa_tabs, tm))
    n2 = w2c.shape[1]
    tn2 = _tile(n2, (512, 256, 128))
    p2 = _mm([h], w2c, tm=tm, tn=tn2, out_dtype=F32, name="in_proj_lat")
    kr_tabs = _rope_tables(positions, MLA_ROPE, 2 * LANES, 0)
    p3 = _mm([h], w3c, tm=tm, tn=2 * LANES, out_dtype=F32, name="in_proj_kr_gate",
             epilogue=_krope_gate_epilogue, extras=_table_extras(kr_tabs, tm))

    qkv = p1.reshape(bsz, seq, n1)
    gates = p3.reshape(bsz, seq, 2 * LANES)

    nchunk = seq // CMP_STRIDE
    ncmp = (seq - CMP_LEN) // CMP_STRIDE + 1
    def chunked(a):
        return (a.reshape(bsz, nchunk, CMP_STRIDE, g_groups, HEAD_DIM).transpose(0, 3, 1, 2, 4)
                .reshape(bsz, g_groups, nchunk, CMP_STRIDE * HEAD_DIM))
    kc = _compress(chunked(qkv[:, :, qw:qw + gw]), pos_k, k1, k2)
    v_cmp = p2.reshape(bsz, seq, n2)[:, :, q_rank + kv_rank:]
    vc = _compress(chunked(v_cmp), pos_v, v1, v2)

    nb = seq // SEL_LEN
    c_start = np.arange(nchunk) * CMP_STRIDE
    b_start = np.arange(LANES) * SEL_LEN
    overlap = ((c_start[:, None] < b_start[None, :] + SEL_LEN) &
               (c_start[:, None] + CMP_LEN > b_start[None, :]) &
               (np.arange(nchunk)[:, None] < ncmp) & (np.arange(LANES)[None, :] < nb))
    overlap = jnp.asarray(overlap.astype(np.float32))
    o_cmp, bias = _cmp_attn(qkv, kc, vc, overlap, gates, g_groups=g_groups, r_heads=r_heads,
                            seq=seq, ncmp=ncmp)

    e_mat = jnp.asarray((np.arange(seq)[:, None] // SEL_LEN == np.arange(LANES)[None, :])
                        .astype(np.float32)).astype(CDT)
    blk = lambda col: col // HEAD_DIM
    tk = _tile(seq, (512, 256, 128))
    def transposed_tiles(a, heads, tk=tk):
        return a.reshape(bsz, seq // tk, tk, heads, HEAD_DIM).transpose(0, 3, 1, 4, 2)
    tk_big = _tile(seq, (512, 256, 128))
    vt_sel = transposed_tiles(qkv[:, :, qw + 3 * gw:qw + 4 * gw], g_groups, tk_big)
    vt_win = transposed_tiles(qkv[:, :, qw + 4 * gw:qw + 5 * gw], g_groups)
    o_sel = _nsa_attn("sel", qkv, blk(qw + gw), vt_sel, gates, o_cmp, F32,
                      g_groups=g_groups, r_heads=r_heads, seq=seq, e_mat=e_mat, bias=bias)
    o_nsa = _nsa_attn("win", qkv, blk(qw + 2 * gw), vt_win, gates, o_sel, CDT,
                      g_groups=g_groups, r_heads=r_heads, seq=seq)

    qk_dim = MLA_NOPE + MLA_ROPE
    wq = w_uq.reshape(q_rank, mla_h, qk_dim)
    wq = jnp.concatenate([wq, jnp.zeros((q_rank, mla_h, MLA_QK_PAD - qk_dim), wq.dtype)], axis=-1)
    wq = wq.reshape(q_rank, mla_h * MLA_QK_PAD).astype(CDT)
    wkv = w_ukv.reshape(kv_rank, mla_h, MLA_NOPE + MLA_V)
    wkv = jnp.concatenate([wkv[:, :, :MLA_NOPE].reshape(kv_rank, mla_h * MLA_NOPE),
                           wkv[:, :, MLA_NOPE:].reshape(kv_rank, mla_h * MLA_V)], axis=1).astype(CDT)
    q_tabs = _rope_tables(positions, MLA_ROPE, MLA_QK_PAD, MLA_NOPE)
    tnq = _tile(mla_h * MLA_QK_PAD, (512, 256))
    q_mla = _mm([p2], wq, tm=tm, tn=tnq, out_dtype=CDT, name="mla_uq",
                a_specs=[pl.BlockSpec((tm, q_rank), lambda i, j: (i, 0))],
                prologue=_rms_prologue(1e-6),
                epilogue=_rope_epilogue(MLA_ROPE // 2, mla_h * MLA_QK_PAD // tnq,
                                        mla_h * MLA_QK_PAD // tnq, qk_dim ** -0.5 * LOG2E),
                extras=_table_extras(q_tabs, tm)
                + [(q_norm.reshape(1, q_rank), pl.BlockSpec((1, q_rank), lambda i, j: (0, 0)))])
    tnkv = _tile(wkv.shape[1], (512, 256, 128))
    kv_mla = _mm([p2], wkv, tm=tm, tn=tnkv, out_dtype=CDT, name="mla_ukv",
                 a_specs=[pl.BlockSpec((tm, kv_rank), lambda i, j: (i, q_rank // kv_rank))],
                 prologue=_rms_prologue(1e-6),
                 extras=[(kv_norm.reshape(1, kv_rank), pl.BlockSpec((1, kv_rank), lambda i, j: (0, 0)))])
    kv3 = kv_mla.reshape(bsz, seq, -1)
    o_mla = _mla_attn(q_mla.reshape(bsz, seq, -1), kv3, gates[:, :, :LANES].astype(CDT),
                      transposed_tiles(kv3[:, :, mla_h * MLA_NOPE:], mla_h, tk_big),
                      heads=mla_h, seq=seq)

    tno = _tile(d, (1024, 512, 256, 128))
    a = _mm([o_nsa.reshape(t, qw), o_mla.reshape(t, mla_h * MLA_V)], w_out.astype(CDT), tm=tm, tn=tno,
            out_dtype=F32, name="out_proj")
    x1, h2 = _res_ln(x2, a, g_a, ln1_g, ln1_b, seq, alpha, mod=(sc_m, sh_m))
    d_ff = w_ff1.shape[1]
    f1 = _mm([h2], w_ff1.astype(CDT), tm=tm, tn=_tile(d_ff, (1024, 512, 256, 128)), out_dtype=CDT,
             name="ff1", epilogue=_relu2_epilogue)
    f2 = _mmk(f1, w_ff2.astype(CDT), tm=tm, tn=tno, tk=_tile(d_ff, (4096, 2048, 1024, 512)),
              out_dtype=F32, name="ff2")
    return _res_ln(x1, f2, g_m, ln2_g, ln2_b, seq, alpha)


def kernel(x, c, positions, w_ada, b_ada, w_in, nsa_pos_k, nsa_pos_v, nsa_cmp_k1, nsa_cmp_k2, nsa_cmp_v1, nsa_cmp_v2, mla_q_norm, mla_kv_norm, mla_w_uq, mla_w_ukv, w_out, ln1_g, ln1_b, w_ff1, w_ff2, ln2_g, ln2_b):
    bsz, seq, d = x.shape
    depth = w_ada.shape[0]
    alpha = (2 * depth) ** 0.25
    x2 = x.reshape(bsz * seq, d)
    for layer in range(depth):
        mod = _ada(c, w_ada[layer], b_ada[layer])
        x2 = _layer(x2, mod, positions, w_in[layer], nsa_pos_k[layer], nsa_pos_v[layer],
                    nsa_cmp_k1[layer], nsa_cmp_k2[layer], nsa_cmp_v1[layer], nsa_cmp_v2[layer],
                    mla_q_norm[layer], mla_kv_norm[layer], mla_w_uq[layer], mla_w_ukv[layer],
                    w_out[layer], ln1_g[layer], ln1_b[layer], w_ff1[layer], w_ff2[layer],
                    ln2_g[layer], ln2_b[layer], bsz=bsz, seq=seq, alpha=alpha)
    return x2.reshape(bsz, seq, d)
```

```python
import functools
import math

import numpy as np
import jax
import jax.numpy as jnp
from jax import lax
from jax.experimental import pallas as pl
from jax.experimental.pallas import tpu as pltpu

HEAD_DIM = 128
CMP_LEN = 32
CMP_STRIDE = 16
SEL_LEN = 64
N_SEL = 16
N_LOCAL_FORCED = 2
FORCED_BONUS = 1e4
WINDOW = 512
MLA_NOPE = 128
MLA_ROPE = 64
MLA_V = 128
ROPE_THETA = 500000.0
PARTIAL_ROT = HEAD_DIM // 4
NEG = -1e30
LOG2E = math.log2(math.e)
N_ADA = 6
LANES = 128
MLA_QK_PAD = 2 * LANES
VMEM_LIMIT = 56 * 1024 * 1024
MM_SUB_COLS = 256

F32 = jnp.float32
CDT = jnp.bfloat16


def _params(sem):
    return pltpu.CompilerParams(dimension_semantics=sem, vmem_limit_bytes=VMEM_LIMIT)


def _tile(n, cands):
    for c in cands:
        if n % c == 0:
            return c
    return n


def _ada_body(c_ref, w_ref, b_ref, o_ref):
    c = c_ref[...]
    cond = c * (1.0 / (1.0 + jnp.exp(-c)))
    o_ref[...] = jnp.dot(cond, w_ref[...], preferred_element_type=F32,
                         precision=lax.Precision.HIGHEST) + b_ref[...]


def _ada(c, w, b):
    bsz, d = c.shape
    n = w.shape[1]
    tn = _tile(n, (512, 256, 128))
    return pl.pallas_call(
        _ada_body,
        grid=(n // tn,),
        in_specs=[pl.BlockSpec((bsz, d), lambda j: (0, 0)),
                  pl.BlockSpec((d, tn), lambda j: (0, j)),
                  pl.BlockSpec((1, tn), lambda j: (0, j))],
        out_specs=pl.BlockSpec((bsz, tn), lambda j: (0, j)),
        out_shape=jax.ShapeDtypeStruct((bsz, n), F32),
        compiler_params=_params(("arbitrary",)),
        name="ada",
    )(c, w, b.reshape(1, n))


def _modcast_body(x_ref, sc_ref, sh_ref, o_ref):
    o_ref[...] = (x_ref[...] * (1.0 + sc_ref[0]) + sh_ref[0]).astype(o_ref.dtype)


def _modcast(x2, sc, sh, seq):
    t, d = x2.shape
    ts = _tile(seq, (256, 128, 64, 8))
    per = seq // ts
    vec = pl.BlockSpec((1, 1, d), lambda i: (i // per, 0, 0))
    return pl.pallas_call(
        _modcast_body,
        grid=(t // ts,),
        in_specs=[pl.BlockSpec((ts, d), lambda i: (i, 0)), vec, vec],
        out_specs=pl.BlockSpec((ts, d), lambda i: (i, 0)),
        out_shape=jax.ShapeDtypeStruct((t, d), CDT),
        compiler_params=_params(("arbitrary",)),
        name="modcast",
    )(x2, sc[:, None, :], sh[:, None, :])


def _mm_body(*refs, na, nex, prologue, epilogue, tn, head_tiles):
    a_refs = refs[:na]
    b_ref = refs[na]
    ex = refs[na + 1:na + 1 + nex]
    o_ref = refs[na + 1 + nex]
    j = pl.program_id(1)
    if prologue is not None:
        a_sc = refs[na + 2 + nex]

        @pl.when(j == 0)
        def _():
            a_sc[...] = prologue(a_refs[0][...], ex).astype(a_sc.dtype)

        a_refs = (a_sc,)
    sub = MM_SUB_COLS if tn % MM_SUB_COLS == 0 else tn

    def store(s, y):
        if head_tiles is None:
            o_ref[:, s * sub:(s + 1) * sub] = y.astype(o_ref.dtype)
        else:
            for hh in range(sub // HEAD_DIM):
                for tt in range(y.shape[0] // head_tiles):
                    blk = y[tt * head_tiles:(tt + 1) * head_tiles, hh * HEAD_DIM:(hh + 1) * HEAD_DIM]
                    o_ref[0, s * (sub // HEAD_DIM) + hh, tt] = blk.T.astype(o_ref.dtype)

    def run(fn):
        for s in range(tn // sub):
            cols = slice(s * sub, (s + 1) * sub)
            acc = None
            off = 0
            for r in a_refs:
                kr = r.shape[1]
                part = jnp.dot(r[...], b_ref[off:off + kr, cols], preferred_element_type=F32)
                acc = part if acc is None else acc + part
                off += kr
            store(s, fn(acc, j, ex))

    if epilogue is None:
        run(lambda acc, j, ex: acc)
    else:
        for cond, fn in epilogue(j):
            pl.when(cond)(functools.partial(run, fn))


def _mm(a_list, b, *, tm, tn, out_dtype, name, a_specs=None, prologue=None, epilogue=None,
        extras=(), head_tiles=None, seq=None):
    m = a_list[0].shape[0]
    k, n = b.shape
    if a_specs is None:
        a_specs = [pl.BlockSpec((tm, a.shape[1]), lambda i, j: (i, 0)) for a in a_list]
    ex_arrays = [e[0] for e in extras]
    ex_specs = [e[1] for e in extras]
    scratch = [pltpu.VMEM((tm, k), CDT)] if prologue is not None else []
    body = functools.partial(_mm_body, na=len(a_list), nex=len(extras), prologue=prologue,
                             epilogue=epilogue, tn=tn, head_tiles=head_tiles)
    if head_tiles is None:
        out_specs = pl.BlockSpec((tm, tn), lambda i, j: (i, j))
        out_shape = jax.ShapeDtypeStruct((m, n), out_dtype)
    else:
        per = seq // tm
        out_specs = pl.BlockSpec((1, tn // HEAD_DIM, tm // head_tiles, HEAD_DIM, head_tiles),
                                 lambda i, j: (i // per, j, i % per, 0, 0))
        out_shape = jax.ShapeDtypeStruct((m // seq, n // HEAD_DIM, seq // head_tiles, HEAD_DIM, head_tiles),
                                         out_dtype)
    return pl.pallas_call(
        body,
        grid=(m // tm, n // tn),
        in_specs=a_specs + [pl.BlockSpec((k, tn), lambda i, j: (0, j))] + ex_specs,
        out_specs=out_specs,
        out_shape=out_shape,
        scratch_shapes=scratch,
        compiler_params=_params(("arbitrary", "arbitrary")),
        name=name,
    )(*a_list, b, *ex_arrays)


def _rot(acc, cos, sa, sb, shift):
    tn = acc.shape[1]
    reps = tn // cos.shape[1]

    def rep(t):
        return t if reps == 1 else jnp.concatenate([t] * reps, axis=1)

    return (acc * rep(cos) + pltpu.roll(acc, tn - shift, 1) * rep(sa)
            + pltpu.roll(acc, shift, 1) * rep(sb))


def _rope_epilogue(shift, n_rope, n_scaled, scale):
    def roped(acc, j, ex):
        y = _rot(acc, ex[0][...], ex[1][...], ex[2][...], shift)
        return y * jnp.where(j < n_scaled, scale, 1.0).astype(F32)

    def epi(j):
        return [(j < n_rope, roped), (j >= n_rope, lambda acc, j, ex: acc)]

    return epi


def _mla_q_epilogue(scale):
    assert MM_SUB_COLS == MLA_QK_PAD

    def fn(acc, j, ex):
        rot = _rot(acc[:, LANES:], ex[0][...], ex[1][...], ex[2][...], MLA_ROPE // 2)
        return jnp.concatenate([acc[:, :LANES], rot], axis=1) * scale

    return lambda j: [(j >= 0, fn)]


def _krope_gate_epilogue(j):
    def fn(acc, j, ex):
        y = _rot(acc, ex[0][...], ex[1][...], ex[2][...], MLA_ROPE // 2)
        lane = lax.broadcasted_iota(jnp.int32, acc.shape, 1)
        return jnp.where(lane < LANES, y, 1.0 / (1.0 + jnp.exp(-acc)))

    return [(j >= 0, fn)]


def _rms_prologue(eps):
    def pro(a, ex):
        g = ex[-1][...]
        y = a * lax.rsqrt(jnp.mean(a * a, axis=-1, keepdims=True) + eps)
        return y * g

    return pro


def _relu2_epilogue(j):
    def fn(acc, j, ex):
        r = jnp.maximum(acc, 0.0)
        return r * r

    return [(j >= 0, fn)]


def _mmk_body(a_ref, b_ref, o_ref, acc_ref):
    kk = pl.program_id(2)

    @pl.when(kk == 0)
    def _():
        acc_ref[...] = jnp.zeros_like(acc_ref)

    acc_ref[...] += jnp.dot(a_ref[...], b_ref[...], preferred_element_type=F32)

    @pl.when(kk == pl.num_programs(2) - 1)
    def _():
        o_ref[...] = acc_ref[...].astype(o_ref.dtype)


def _mmk(a, b, *, tm, tn, tk, out_dtype, name):
    m, k = a.shape
    n = b.shape[1]
    return pl.pallas_call(
        _mmk_body,
        grid=(m // tm, n // tn, k // tk),
        in_specs=[pl.BlockSpec((tm, tk), lambda i, j, kk: (i, kk)),
                  pl.BlockSpec((tk, tn), lambda i, j, kk: (kk, j))],
        out_specs=pl.BlockSpec((tm, tn), lambda i, j, kk: (i, j)),
        out_shape=jax.ShapeDtypeStruct((m, n), out_dtype),
        scratch_shapes=[pltpu.VMEM((tm, tn), F32)],
        compiler_params=_params(("arbitrary", "arbitrary", "arbitrary")),
        name=name,
    )(a, b)


def _ln_body(x_ref, a_ref, gate_ref, g_ref, b_ref, *rest, alpha, with_mod):
    y = alpha * x_ref[...] + (1.0 + gate_ref[0]) * a_ref[...]
    mu = jnp.mean(y, axis=-1, keepdims=True)
    yc = y - mu
    var = jnp.mean(yc * yc, axis=-1, keepdims=True)
    out = yc * lax.rsqrt(var + 1e-5) * g_ref[...] + b_ref[...]
    if with_mod:
        sc_ref, sh_ref, o_ref, h_ref = rest
        o_ref[...] = out
        h_ref[...] = (out * (1.0 + sc_ref[0]) + sh_ref[0]).astype(h_ref.dtype)
    else:
        (o_ref,) = rest
        o_ref[...] = out


def _res_ln(x2, a2, gate, g, b, seq, alpha, mod=None):
    t, d = x2.shape
    ts = _tile(seq, (256, 128, 64, 8))
    per = seq // ts
    row = pl.BlockSpec((ts, d), lambda i: (i, 0))
    vec = pl.BlockSpec((1, 1, d), lambda i: (i // per, 0, 0))
    par = pl.BlockSpec((1, d), lambda i: (0, 0))
    ins = [x2, a2, gate[:, None, :], g.reshape(1, d), b.reshape(1, d)]
    specs = [row, row, vec, par, par]
    if mod is None:
        out_shape = jax.ShapeDtypeStruct((t, d), F32)
        out_specs = row
    else:
        ins += [mod[0][:, None, :], mod[1][:, None, :]]
        specs += [vec, vec]
        out_shape = (jax.ShapeDtypeStruct((t, d), F32), jax.ShapeDtypeStruct((t, d), CDT))
        out_specs = (row, row)
    return pl.pallas_call(
        functools.partial(_ln_body, alpha=alpha, with_mod=mod is not None),
        grid=(t // ts,),
        in_specs=specs,
        out_specs=out_specs,
        out_shape=out_shape,
        compiler_params=_params(("arbitrary",)),
        name="res_ln",
    )(*ins)


def _compress_body(x_ref, p0_ref, p1_ref, w1a_ref, w1b_ref, w2_ref, o_ref):
    x = x_ref[0, 0].astype(F32)
    nc = x.shape[0]
    xa = (x + p0_ref[...]).astype(CDT)
    xb = (x + p1_ref[...]).astype(CDT)
    a = jnp.dot(xa, w1a_ref[...], preferred_element_type=F32)
    bm = jnp.dot(xb, w1b_ref[...], preferred_element_type=F32)
    pre = a + pltpu.roll(bm, nc - 1, 0)
    hid = 0.5 * pre * (1.0 + jnp.tanh(math.sqrt(2.0 / math.pi) * (pre + 0.044715 * pre * pre * pre)))
    o_ref[0, 0] = jnp.dot(hid.astype(CDT), w2_ref[...], preferred_element_type=F32).astype(o_ref.dtype)


def _compress(xc, pos, w1, w2):
    bsz, g, nc, wid = xc.shape
    half = wid
    p0 = pos[:CMP_STRIDE].reshape(1, wid)
    p1 = pos[CMP_STRIDE:].reshape(1, wid)
    w1c = w1.astype(CDT)
    full = lambda shape: pl.BlockSpec(shape, lambda b, gi: (0,) * len(shape))
    return pl.pallas_call(
        _compress_body,
        grid=(bsz, g),
        in_specs=[pl.BlockSpec((1, 1, nc, wid), lambda b, gi: (b, gi, 0, 0)),
                  full((1, wid)), full((1, wid)),
                  full((half, w1.shape[1])), full((half, w1.shape[1])),
                  full(w2.shape)],
        out_specs=pl.BlockSpec((1, 1, nc, HEAD_DIM), lambda b, gi: (b, gi, 0, 0)),
        out_shape=jax.ShapeDtypeStruct((bsz, g, nc, HEAD_DIM), CDT),
        compiler_params=_params(("arbitrary", "arbitrary")),
        name="compress",
    )(xc, p0, p1, w1c[:half], w1c[half:], w2.astype(CDT))


def _cmp_body(q_ref, kc_ref, vc_ref, ov_ref, gate_ref, o_ref, bias_ref, *, tq, r_heads, nb, ncmp):
    gi = pl.program_id(1)
    i = pl.program_id(2)
    kc = kc_ref[0, 0]
    vc = vc_ref[0, 0]
    nc = kc.shape[0]
    t = i * tq + lax.broadcasted_iota(jnp.int32, (tq, nc), 0)
    cidx = lax.broadcasted_iota(jnp.int32, (tq, nc), 1)
    valid = (cidx * CMP_STRIDE + (CMP_LEN - 1) <= t) & (cidx < ncmp)
    validf = valid.astype(F32)
    real = cidx < ncmp
    gates = gate_ref[0]
    glane = lax.broadcasted_iota(jnp.int32, gates.shape, 1)
    psum = jnp.zeros((tq, nc), F32)
    for r in range(r_heads):
        q = q_ref[0, :, r * HEAD_DIM:(r + 1) * HEAD_DIM]
        s = lax.dot_general(q, kc, (((1,), (1,)), ((), ())), preferred_element_type=F32)
        s = jnp.where(valid, s, jnp.where(real, NEG, -jnp.inf))
        m = jnp.max(s, axis=-1, keepdims=True)
        e = jnp.exp2(s - m)
        p = e / jnp.sum(e, axis=-1, keepdims=True) * validf
        psum = psum + p
        o = jnp.dot(p.astype(CDT), vc, preferred_element_type=F32)
        col = ((gi * r_heads + r) * 3 + 0)
        gcol = jnp.sum(jnp.where(glane == col, gates, 0.0), axis=-1, keepdims=True)
        o_ref[0, :, r * HEAD_DIM:(r + 1) * HEAD_DIM] = (gcol * o).astype(o_ref.dtype)

    p_hi = psum.astype(CDT)
    p_lo = (psum - p_hi.astype(F32)).astype(CDT)
    imp = (jnp.dot(p_hi, ov_ref[...], preferred_element_type=F32)
           + jnp.dot(p_lo, ov_ref[...], preferred_element_type=F32))
    tt = i * tq + lax.broadcasted_iota(jnp.int32, (tq, LANES), 0)
    jj = lax.broadcasted_iota(jnp.int32, (tq, LANES), 1)
    cur = tt // SEL_LEN
    forced = (jj == 0) | ((jj <= cur) & (jj > cur - N_LOCAL_FORCED))
    imp = jnp.where(jj > cur, NEG, imp + jnp.where(forced, FORCED_BONUS, 0.0))
    v = imp.T[:nb]
    ridx = lax.broadcasted_iota(jnp.int32, (nb, tq), 0)
    sel = jnp.zeros((nb, tq), F32)
    for _ in range(min(N_SEL, nb)):
        mx = jnp.max(v, axis=0, keepdims=True)
        first = jnp.min(jnp.where(v == mx, ridx, nb), axis=0, keepdims=True)
        hit = ridx == first
        sel = jnp.where(hit, 1.0, sel)
        v = jnp.where(hit, -jnp.inf, v)
    bias_t = jnp.where(sel > 0.0, 0.0, NEG)
    if nb < LANES:
        bias_t = jnp.concatenate([bias_t, jnp.zeros((LANES - nb, tq), F32)], axis=0)
    bias_ref[0, 0] = bias_t.T.astype(bias_ref.dtype)


def _cmp_attn(qkv, kc, vc, overlap, gates, *, g_groups, r_heads, seq, ncmp):
    bsz = qkv.shape[0]
    nc = kc.shape[2]
    nb = seq // SEL_LEN
    tq = _tile(seq, (512, 256, 128))
    qw = r_heads * HEAD_DIM
    body = functools.partial(_cmp_body, tq=tq, r_heads=r_heads, nb=nb, ncmp=ncmp)
    return pl.pallas_call(
        body,
        grid=(bsz, g_groups, seq // tq),
        in_specs=[pl.BlockSpec((1, tq, qw), lambda b, g, i: (b, i, g)),
                  pl.BlockSpec((1, 1, nc, HEAD_DIM), lambda b, g, i: (b, g, 0, 0)),
                  pl.BlockSpec((1, 1, nc, HEAD_DIM), lambda b, g, i: (b, g, 0, 0)),
                  pl.BlockSpec((nc, LANES), lambda b, g, i: (0, 0)),
                  pl.BlockSpec((1, tq, LANES), lambda b, g, i: (b, i, 1))],
        out_specs=(pl.BlockSpec((1, tq, qw), lambda b, g, i: (b, i, g)),
                   pl.BlockSpec((1, 1, tq, LANES), lambda b, g, i: (b, g, i, 0))),
        out_shape=(jax.ShapeDtypeStruct((bsz, seq, g_groups * qw), F32),
                   jax.ShapeDtypeStruct((bsz, g_groups, seq, LANES), CDT)),
        compiler_params=_params(("arbitrary", "arbitrary", "arbitrary")),
        name="cmp_attn",
    )(qkv, kc, vc, overlap, gates)


ACC_ROWS = HEAD_DIM + 16


def _with_ones_row(vt):
    ones_row = jnp.where(lax.broadcasted_iota(jnp.int32, (ACC_ROWS - HEAD_DIM, vt.shape[1]), 0) == 0,
                         1.0, 0.0).astype(CDT)
    return jnp.concatenate([vt, ones_row], axis=0)


def _flash_init(m_ref, acc_ref):
    m_ref[...] = jnp.full(m_ref.shape, -jnp.inf, F32)
    acc_ref[...] = jnp.zeros(acc_ref.shape, F32)


def _nsa_finalize(acc_ref, gate_ref, prev_ref, o_ref, gi, tq, r_heads, branch):
    gates_t = gate_ref[0].T
    rid = lax.broadcasted_iota(jnp.int32, gates_t.shape, 0)
    for r in range(r_heads):
        acc = acc_ref[0]
        cs = slice(r * tq, (r + 1) * tq)
        col = (gi * r_heads + r) * 3 + branch
        grow = jnp.sum(jnp.where(rid == col, gates_t, 0.0), axis=0, keepdims=True)
        o = acc[:HEAD_DIM, cs] * (grow / acc[HEAD_DIM:HEAD_DIM + 1, cs])
        sl = slice(r * HEAD_DIM, (r + 1) * HEAD_DIM)
        o_ref[0, :, sl] = (prev_ref[0, :, sl] + o.T).astype(o_ref.dtype)


def _sel_body(q_ref, bias_ref, k_ref, e_ref, vt_ref, gate_ref, prev_ref, o_ref, qc_ref, s0_ref, s1_ref,
              cm_ref, m_ref, acc_ref, *, tq, r_heads):
    s_refs = (s0_ref, s1_ref)
    gi = pl.program_id(1)
    i = pl.program_id(2)
    tk = tq
    cols = r_heads * tq
    for r in range(r_heads):
        qc_ref[r * tq:(r + 1) * tq, :] = jnp.concatenate(
            [q_ref[0, :, r * HEAD_DIM:(r + 1) * HEAD_DIM], bias_ref[0, 0]], axis=1)
    _flash_init(m_ref, acc_ref)

    def scores(j, buf, masked):
        off = pl.multiple_of(j * tk, tk)
        kmat = jnp.concatenate([k_ref[0, pl.ds(off, tk), :], e_ref[pl.ds(off, tk), :]], axis=1)
        s = lax.dot_general(kmat, qc_ref[...], (((1,), (1,)), ((), ())), preferred_element_type=F32)
        if masked:
            key = lax.broadcasted_iota(jnp.int32, (tk, cols), 0)
            tok = lax.broadcasted_iota(jnp.int32, (tk, cols), 1) & (tq - 1)
            s = jnp.where(key <= tok, s, NEG)
        s_refs[buf][...] = s
        cm_ref[buf] = jnp.max(s, axis=0, keepdims=True)

    def update(j, buf):
        m_prev = m_ref[0]
        m_new = jnp.maximum(m_prev, cm_ref[buf])
        p = jnp.exp2(s_refs[buf][...] - m_new).astype(CDT)
        acc_ref[0] = (jnp.exp2(m_prev - m_new) * acc_ref[0]
                      + jnp.dot(_with_ones_row(vt_ref[0, 0, j]), p, preferred_element_type=F32))
        m_ref[0] = m_new

    @pl.when(i == 0)
    def _():
        scores(0, 0, True)

    @pl.when(i > 0)
    def _():
        scores(0, 0, False)

    n_pairs = jnp.maximum(i - 1, 0) // 2

    def body(jp, carry):
        j = 2 * jp
        scores(j + 1, 1, False)
        update(j, 0)
        scores(j + 2, 0, False)
        update(j + 1, 1)
        return carry

    lax.fori_loop(0, n_pairs, body, 0)
    j0 = 2 * n_pairs
    left = i - j0

    @pl.when(left == 1)
    def _():
        scores(i, 1, True)
        update(j0, 0)
        update(i, 1)

    @pl.when(left == 2)
    def _():
        scores(j0 + 1, 1, False)
        update(j0, 0)
        scores(i, 0, True)
        update(j0 + 1, 1)
        update(i, 0)

    @pl.when(left == 0)
    def _():
        update(0, 0)

    _nsa_finalize(acc_ref, gate_ref, prev_ref, o_ref, gi, tq, r_heads, 1)


def _win_body(q_ref, k_ref, vt_ref, gate_ref, prev_ref, o_ref, qc_ref, s0_ref, s1_ref, cm_ref, m_ref,
              acc_ref, *, tq, r_heads, n_tiles):
    s_refs = (s0_ref, s1_ref)
    gi = pl.program_id(1)
    i = pl.program_id(2)
    tk = tq
    cols = r_heads * tq
    for r in range(r_heads):
        qc_ref[r * tq:(r + 1) * tq, :] = q_ref[0, :, r * HEAD_DIM:(r + 1) * HEAD_DIM]
    _flash_init(m_ref, acc_ref)

    def scores(dd, buf):
        back = n_tiles - 1 - dd
        off = pl.multiple_of((i - back) * tk, tk)
        s = lax.dot_general(k_ref[0, pl.ds(off, tk), :], qc_ref[...], (((1,), (1,)), ((), ())),
                            preferred_element_type=F32)
        key = lax.broadcasted_iota(jnp.int32, (tk, cols), 0)
        tok = lax.broadcasted_iota(jnp.int32, (tk, cols), 1) & (tq - 1)
        dist = tok - key + back * tk
        s = jnp.where((dist >= 0) & (dist < WINDOW), s, NEG)
        s_refs[buf][...] = s
        cm_ref[buf] = jnp.max(s, axis=0, keepdims=True)

    def update(dd, buf):
        j = i - (n_tiles - 1 - dd)
        m_prev = m_ref[0]
        m_new = jnp.maximum(m_prev, cm_ref[buf])
        p = jnp.exp2(s_refs[buf][...] - m_new).astype(CDT)
        acc_ref[0] = (jnp.exp2(m_prev - m_new) * acc_ref[0]
                      + jnp.dot(_with_ones_row(vt_ref[0, 0, j]), p, preferred_element_type=F32))
        m_ref[0] = m_new

    @pl.when(i >= n_tiles - 1)
    def _():
        scores(0, 0)
        for dd in range(1, n_tiles):
            scores(dd, dd % 2)
            update(dd - 1, (dd - 1) % 2)
        update(n_tiles - 1, (n_tiles - 1) % 2)

    for dd in range(1, n_tiles):
        @pl.when((i < n_tiles - 1) & (i >= n_tiles - 1 - dd))
        def _():
            scores(dd, 0)
            update(dd, 0)

    _nsa_finalize(acc_ref, gate_ref, prev_ref, o_ref, gi, tq, r_heads, 2)


def _nsa_attn(mode, qkv, k_blk0, vt, v_head0, gates, prev, out_dtype, *, g_groups, r_heads, seq,
              e_mat=None, bias=None):
    bsz = qkv.shape[0]
    qw = r_heads * HEAD_DIM
    tk = vt.shape[-1]
    kspec = pl.BlockSpec((1, seq, HEAD_DIM), lambda b, g, i: (b, 0, k_blk0 + g))
    vspec = pl.BlockSpec((1, 1) + vt.shape[2:], lambda b, g, i: (b, v_head0 + g, 0, 0, 0))
    tq = tk
    cols = r_heads * tq
    if mode == "sel":
        qk_w = 2 * HEAD_DIM
        body = functools.partial(_sel_body, tq=tq, r_heads=r_heads)
    else:
        qk_w = HEAD_DIM
        body = functools.partial(_win_body, tq=tq, r_heads=r_heads, n_tiles=(WINDOW + tk - 1) // tk + 1)
    qspec = pl.BlockSpec((1, tq, qw), lambda b, g, i: (b, i, g))
    gspec = pl.BlockSpec((1, tq, LANES), lambda b, g, i: (b, i, 1))
    if mode == "sel":
        in_specs = [qspec, pl.BlockSpec((1, 1, tq, LANES), lambda b, g, i: (b, g, i, 0)), kspec,
                    pl.BlockSpec((seq, LANES), lambda b, g, i: (0, 0)), vspec, gspec, qspec]
        ins = [qkv, bias, qkv, e_mat, vt, gates, prev]
    else:
        in_specs = [qspec, kspec, vspec, gspec, qspec]
        ins = [qkv, qkv, vt, gates, prev]
    return pl.pallas_call(
        body,
        grid=(bsz, g_groups, seq // tq),
        in_specs=in_specs,
        out_specs=qspec,
        out_shape=jax.ShapeDtypeStruct((bsz, seq, g_groups * qw), out_dtype),
        scratch_shapes=[pltpu.VMEM((cols, qk_w), CDT),
                        pltpu.VMEM((tk, cols), F32), pltpu.VMEM((tk, cols), F32),
                        pltpu.VMEM((2, 1, cols), F32),
                        pltpu.VMEM((1, 1, cols), F32),
                        pltpu.VMEM((1, ACC_ROWS, cols), F32)],
        compiler_params=_params(("arbitrary",) * 3),
        name="nsa_" + mode,
    )(*ins)


def _mla_body(q_ref, kn_ref, kr_ref, vt_ref, o_ref, *rest, tq, tk, hg):
    s_refs = rest[:hg]
    cm_ref, m_ref, acc_ref = rest[hg:]
    i = pl.program_id(2)
    _flash_init(m_ref, acc_ref)
    n_full = (i * tq) // tk

    def step(j, mask, c0):
        off = pl.multiple_of(j * tk, tk)
        kr = kr_ref[0, pl.ds(off, tk), :]

        def scores(c):
            kmat = jnp.concatenate([kn_ref[0, pl.ds(off, tk), c * MLA_NOPE:(c + 1) * MLA_NOPE], kr], axis=1)
            s = lax.dot_general(kmat, q_ref[0, c0:, c * MLA_QK_PAD:(c + 1) * MLA_QK_PAD],
                                (((1,), (1,)), ((), ())), preferred_element_type=F32)
            if mask is not None:
                s = jnp.where(mask, s, NEG)
            s_refs[c][:, c0:] = s
            cm_ref[c, :, c0:] = jnp.max(s, axis=0, keepdims=True)

        def update(c):
            m_prev = m_ref[c, :, c0:]
            m_new = jnp.maximum(m_prev, cm_ref[c, :, c0:])
            p = jnp.exp2(s_refs[c][:, c0:] - m_new).astype(CDT)
            acc_ref[c, :, c0:] = (jnp.exp2(m_prev - m_new) * acc_ref[c, :, c0:]
                                  + jnp.dot(_with_ones_row(vt_ref[0, c, j]), p, preferred_element_type=F32))
            m_ref[c, :, c0:] = m_new

        scores(0)
        for c in range(1, hg):
            scores(c)
            update(c - 1)
        update(hg - 1)

    def full_tile(j, carry):
        step(j, None, 0)
        return carry

    lax.fori_loop(0, n_full, full_tile, 0)
    for dd in range(tq // tk):
        w = tq - dd * tk
        mask = (lax.broadcasted_iota(jnp.int32, (tk, w), 0) <= lax.broadcasted_iota(jnp.int32, (tk, w), 1))
        step(n_full + dd, mask, dd * tk)
    for c in range(hg):
        acc = acc_ref[c]
        o = acc[:HEAD_DIM] * (1.0 / acc[HEAD_DIM:HEAD_DIM + 1])
        o_ref[0, :, c * MLA_V:(c + 1) * MLA_V] = o.T.astype(o_ref.dtype)


def _mla_attn(q, kv, kr, vt, *, heads, seq):
    bsz = q.shape[0]
    tk = vt.shape[-1]
    tq = _tile(seq, (1024, 512, 256, 128))
    hg = 4 if heads % 4 == 0 else 1
    body = functools.partial(_mla_body, tq=tq, tk=tk, hg=hg)
    return pl.pallas_call(
        body,
        grid=(bsz, heads // hg, seq // tq),
        in_specs=[pl.BlockSpec((1, tq, hg * MLA_QK_PAD), lambda b, h, i: (b, i, h)),
                  pl.BlockSpec((1, seq, hg * MLA_NOPE), lambda b, h, i: (b, 0, h)),
                  pl.BlockSpec((1, seq, LANES), lambda b, h, i: (b, 0, 0)),
                  pl.BlockSpec((1, hg) + vt.shape[2:], lambda b, h, i: (b, h, 0, 0, 0))],
        out_specs=pl.BlockSpec((1, tq, hg * MLA_V), lambda b, h, i: (b, i, h)),
        out_shape=jax.ShapeDtypeStruct((bsz, seq, heads * MLA_V), CDT),
        scratch_shapes=[pltpu.VMEM((tk, tq), F32)] * hg + [
            pltpu.VMEM((hg, 1, tq), F32), pltpu.VMEM((hg, 1, tq), F32), pltpu.VMEM((hg, ACC_ROWS, tq), F32)],
        compiler_params=_params(("arbitrary",) * 3),
        name="mla_attn",
    )(q, kv, kr, vt)


def _rope_tables(positions, rot_dim, period, offset):
    half = rot_dim // 2
    inv = jnp.power(ROPE_THETA, -jnp.arange(0, rot_dim, 2, dtype=F32) / rot_dim)
    ang = positions.astype(F32).reshape(-1)[:, None] * inv
    cos, sin = jnp.cos(ang), jnp.sin(ang)
    t = ang.shape[0]
    ones_l = jnp.ones((t, offset), F32)
    zeros_l = jnp.zeros((t, offset), F32)
    rest = period - offset - rot_dim
    ct = jnp.concatenate([ones_l, cos, cos, jnp.ones((t, rest), F32)], axis=1)
    sa = jnp.concatenate([zeros_l, -sin, jnp.zeros((t, half + rest), F32)], axis=1)
    sb = jnp.concatenate([zeros_l, jnp.zeros((t, half), F32), sin, jnp.zeros((t, rest), F32)], axis=1)
    return ct, sa, sb


def _table_extras(tables, tm):
    return [(tb, pl.BlockSpec((tm, tb.shape[1]), lambda i, j: (i, 0))) for tb in tables]


def _layer(x2, mod, positions, w_in, pos_k, pos_v, k1, k2, v1, v2, q_norm, kv_norm, w_uq, w_ukv,
           w_out, ln1_g, ln1_b, w_ff1, w_ff2, ln2_g, ln2_b, *, bsz, seq, alpha):
    t, d = x2.shape
    n_heads = d // HEAD_DIM
    nsa_h = n_heads // 2
    mla_h = n_heads - nsa_h
    q_rank = q_norm.shape[0]
    kv_rank = kv_norm.shape[0]
    d_in = w_in.shape[1]
    g_groups = (d_in - nsa_h * HEAD_DIM - nsa_h * 3 - q_rank - kv_rank - MLA_ROPE) // (6 * HEAD_DIM)
    r_heads = nsa_h // g_groups
    gw = g_groups * HEAD_DIM
    qw = nsa_h * HEAD_DIM
    sh_a, sc_a, g_a, sh_m, sc_m, g_m = jnp.split(mod, N_ADA, axis=-1)

    o_q, o_kv = 0, qw
    o_gate = o_kv + 6 * gw
    o_cq = o_gate + nsa_h * 3
    o_ckv = o_cq + q_rank
    o_kr = o_ckv + kv_rank
    kvcol = lambda idx: w_in[:, o_kv + idx * gw:o_kv + (idx + 1) * gw]
    w1c = jnp.concatenate([w_in[:, o_q:o_q + qw], kvcol(0), kvcol(2), kvcol(4)], axis=1).astype(CDT)
    wvc = jnp.concatenate([kvcol(3), kvcol(5)], axis=1).astype(CDT)
    w2c = jnp.concatenate([w_in[:, o_cq:o_cq + q_rank], w_in[:, o_ckv:o_ckv + kv_rank], kvcol(1)],
                          axis=1).astype(CDT)
    zpad = lambda n: jnp.zeros((d, n), w_in.dtype)
    w3c = jnp.concatenate([w_in[:, o_kr:o_kr + MLA_ROPE], zpad(LANES - MLA_ROPE),
                           w_in[:, o_gate:o_gate + nsa_h * 3], zpad(LANES - nsa_h * 3)],
                          axis=1).astype(CDT)

    h = _modcast(x2, sc_a, sh_a, seq)

    tm = _tile(t, (1024, 512, 256, 128))
    nsa_tabs = _rope_tables(positions, PARTIAL_ROT, LANES, 0)
    n1 = w1c.shape[1]
    tn1 = _tile(gw, (512, 256, 128))
    n_rope_cols = qw + 3 * gw
    scale = HEAD_DIM ** -0.5 * LOG2E
    p1 = _mm([h], w1c, tm=tm, tn=tn1, out_dtype=CDT, name="in_proj_rot",
             epilogue=_rope_epilogue(PARTIAL_ROT // 2, n_rope_cols // tn1, qw // tn1, scale),
             extras=_table_extras(nsa_tabs, tm))
    tk = _tile(seq, (512, 256, 128))
    vt_nsa = _mm([h], wvc, tm=tm, tn=_tile(2 * gw, (512, 256, 128)), out_dtype=CDT, name="in_proj_vt",
                 head_tiles=tk, seq=seq)
    n2 = w2c.shape[1]
    tn2 = _tile(n2, (512, 256, 128))
    p2 = _mm([h], w2c, tm=tm, tn=tn2, out_dtype=F32, name="in_proj_lat")
    kr_tabs = _rope_tables(positions, MLA_ROPE, LANES, 0)
    p3 = _mm([h], w3c, tm=tm, tn=2 * LANES, out_dtype=F32, name="in_proj_kr_gate",
             epilogue=_krope_gate_epilogue, extras=_table_extras(kr_tabs, tm))

    qkv = p1.reshape(bsz, seq, n1)
    gates = p3.reshape(bsz, seq, 2 * LANES)

    nchunk = seq // CMP_STRIDE
    ncmp = (seq - CMP_LEN) // CMP_STRIDE + 1
    def chunked(a):
        return (a.reshape(bsz, nchunk, CMP_STRIDE, g_groups, HEAD_DIM).transpose(0, 3, 1, 2, 4)
                .reshape(bsz, g_groups, nchunk, CMP_STRIDE * HEAD_DIM))
    kc = _compress(chunked(qkv[:, :, qw:qw + gw]), pos_k, k1, k2)
    v_cmp = p2.reshape(bsz, seq, n2)[:, :, q_rank + kv_rank:]
    vc = _compress(chunked(v_cmp), pos_v, v1, v2)

    nb = seq // SEL_LEN
    c_start = np.arange(nchunk) * CMP_STRIDE
    b_start = np.arange(LANES) * SEL_LEN
    overlap = ((c_start[:, None] < b_start[None, :] + SEL_LEN) &
               (c_start[:, None] + CMP_LEN > b_start[None, :]) &
               (np.arange(nchunk)[:, None] < ncmp) & (np.arange(LANES)[None, :] < nb))
    overlap = jnp.asarray(overlap.astype(np.float32)).astype(CDT)
    o_cmp, bias = _cmp_attn(qkv, kc, vc, overlap, gates, g_groups=g_groups, r_heads=r_heads,
                            seq=seq, ncmp=ncmp)

    e_mat = jnp.asarray((np.arange(seq)[:, None] // SEL_LEN == np.arange(LANES)[None, :])
                        .astype(np.float32)).astype(CDT)
    blk = lambda col: col // HEAD_DIM
    o_sel = _nsa_attn("sel", qkv, blk(qw + gw), vt_nsa, 0, gates, o_cmp, F32,
                      g_groups=g_groups, r_heads=r_heads, seq=seq, e_mat=e_mat, bias=bias)
    o_nsa = _nsa_attn("win", qkv, blk(qw + 2 * gw), vt_nsa, g_groups, gates, o_sel, CDT,
                      g_groups=g_groups, r_heads=r_heads, seq=seq)

    qk_dim = MLA_NOPE + MLA_ROPE
    wq = w_uq.reshape(q_rank, mla_h, qk_dim)
    wq = jnp.concatenate([wq, jnp.zeros((q_rank, mla_h, MLA_QK_PAD - qk_dim), wq.dtype)], axis=-1)
    wq = wq.reshape(q_rank, mla_h * MLA_QK_PAD).astype(CDT)
    wkv = w_ukv.reshape(kv_rank, mla_h, MLA_NOPE + MLA_V)
    wk = wkv[:, :, :MLA_NOPE].reshape(kv_rank, mla_h * MLA_NOPE).astype(CDT)
    wv = wkv[:, :, MLA_NOPE:].reshape(kv_rank, mla_h * MLA_V).astype(CDT)
    tnq = _tile(mla_h * MLA_QK_PAD, (512, 256))
    q_mla = _mm([p2], wq, tm=tm, tn=tnq, out_dtype=CDT, name="mla_uq",
                a_specs=[pl.BlockSpec((tm, q_rank), lambda i, j: (i, 0))],
                prologue=_rms_prologue(1e-6),
                epilogue=_mla_q_epilogue(qk_dim ** -0.5 * LOG2E),
                extras=_table_extras(kr_tabs, tm)
                + [(q_norm.reshape(1, q_rank), pl.BlockSpec((1, q_rank), lambda i, j: (0, 0)))])
    tnkv = _tile(wk.shape[1], (512, 256, 128))
    ckv_args = dict(tm=tm, tn=tnkv, out_dtype=CDT, prologue=_rms_prologue(1e-6),
                    a_specs=[pl.BlockSpec((tm, kv_rank), lambda i, j: (i, q_rank // kv_rank))],
                    extras=[(kv_norm.reshape(1, kv_rank), pl.BlockSpec((1, kv_rank), lambda i, j: (0, 0)))])
    k_mla = _mm([p2], wk, name="mla_uk", **ckv_args)
    vt_mla = _mm([p2], wv, name="mla_uv", head_tiles=tk, seq=seq, **ckv_args)
    o_mla = _mla_attn(q_mla.reshape(bsz, seq, -1), k_mla.reshape(bsz, seq, -1),
                      gates[:, :, :LANES].astype(CDT), vt_mla, heads=mla_h, seq=seq)

    tno = _tile(d, (1024, 512, 256, 128))
    a = _mm([o_nsa.reshape(t, qw), o_mla.reshape(t, mla_h * MLA_V)], w_out.astype(CDT), tm=tm, tn=tno,
            out_dtype=F32, name="out_proj")
    x1, h2 = _res_ln(x2, a, g_a, ln1_g, ln1_b, seq, alpha, mod=(sc_m, sh_m))
    d_ff = w_ff1.shape[1]
    f1 = _mm([h2], w_ff1.astype(CDT), tm=tm, tn=_tile(d_ff, (1024, 512, 256, 128)), out_dtype=CDT,
             name="ff1", epilogue=_relu2_epilogue)
    f2 = _mmk(f1, w_ff2.astype(CDT), tm=tm, tn=tno, tk=_tile(d_ff, (4096, 2048, 1024, 512)),
              out_dtype=F32, name="ff2")
    return _res_ln(x1, f2, g_m, ln2_g, ln2_b, seq, alpha)


def kernel(x, c, positions, w_ada, b_ada, w_in, nsa_pos_k, nsa_pos_v, nsa_cmp_k1, nsa_cmp_k2, nsa_cmp_v1, nsa_cmp_v2, mla_q_norm, mla_kv_norm, mla_w_uq, mla_w_ukv, w_out, ln1_g, ln1_b, w_ff1, w_ff2, ln2_g, ln2_b):
    bsz, seq, d = x.shape
    depth = w_ada.shape[0]
    alpha = (2 * depth) ** 0.25
    x2 = x.reshape(bsz * seq, d)
    for layer in range(depth):
        mod = _ada(c, w_ada[layer], b_ada[layer])
        x2 = _layer(x2, mod, positions, w_in[layer], nsa_pos_k[layer], nsa_pos_v[layer],
                    nsa_cmp_k1[layer], nsa_cmp_k2[layer], nsa_cmp_v1[layer], nsa_cmp_v2[layer],
                    mla_q_norm[layer], mla_kv_norm[layer], mla_w_uq[layer], mla_w_ukv[layer],
                    w_out[layer], ln1_g[layer], ln1_b[layer], w_ff1[layer], w_ff2[layer],
                    ln2_g[layer], ln2_b[layer], bsz=bsz, seq=seq, alpha=alpha)
    return x2.reshape(bsz, seq, d)
```

```python
import functools
import math

import numpy as np
import jax
import jax.numpy as jnp
from jax import lax
from jax.experimental import pallas as pl
from jax.experimental.pallas import tpu as pltpu

HEAD_DIM = 128
CMP_LEN = 32
CMP_STRIDE = 16
SEL_LEN = 64
N_SEL = 16
N_LOCAL_FORCED = 2
FORCED_BONUS = 1e4
WINDOW = 512
MLA_NOPE = 128
MLA_ROPE = 64
MLA_V = 128
ROPE_THETA = 500000.0
PARTIAL_ROT = HEAD_DIM // 4
NEG = -1e30
LOG2E = math.log2(math.e)
N_ADA = 6
LANES = 128
MLA_QK_PAD = 2 * LANES
VMEM_LIMIT = 56 * 1024 * 1024
MM_SUB_COLS = 256

F32 = jnp.float32
CDT = jnp.bfloat16


def _params(sem):
    return pltpu.CompilerParams(dimension_semantics=sem, vmem_limit_bytes=VMEM_LIMIT)


def _tile(n, cands):
    for c in cands:
        if n % c == 0:
            return c
    return n


def _ada_body(c_ref, w_ref, b_ref, o_ref):
    c = c_ref[...]
    cond = c * (1.0 / (1.0 + jnp.exp(-c)))
    o_ref[...] = jnp.dot(cond, w_ref[...], preferred_element_type=F32,
                         precision=lax.Precision.HIGHEST) + b_ref[...]


def _ada(c, w, b):
    bsz, d = c.shape
    n = w.shape[1]
    tn = _tile(n, (512, 256, 128))
    return pl.pallas_call(
        _ada_body,
        grid=(n // tn,),
        in_specs=[pl.BlockSpec((bsz, d), lambda j: (0, 0)),
                  pl.BlockSpec((d, tn), lambda j: (0, j)),
                  pl.BlockSpec((1, tn), lambda j: (0, j))],
        out_specs=pl.BlockSpec((bsz, tn), lambda j: (0, j)),
        out_shape=jax.ShapeDtypeStruct((bsz, n), F32),
        compiler_params=_params(("arbitrary",)),
        name="ada",
    )(c, w, b.reshape(1, n))


def _modcast_body(x_ref, sc_ref, sh_ref, o_ref):
    o_ref[...] = (x_ref[...] * (1.0 + sc_ref[0]) + sh_ref[0]).astype(o_ref.dtype)


def _modcast(x2, sc, sh, seq):
    t, d = x2.shape
    ts = _tile(seq, (256, 128, 64, 8))
    per = seq // ts
    vec = pl.BlockSpec((1, 1, d), lambda i: (i // per, 0, 0))
    return pl.pallas_call(
        _modcast_body,
        grid=(t // ts,),
        in_specs=[pl.BlockSpec((ts, d), lambda i: (i, 0)), vec, vec],
        out_specs=pl.BlockSpec((ts, d), lambda i: (i, 0)),
        out_shape=jax.ShapeDtypeStruct((t, d), CDT),
        compiler_params=_params(("arbitrary",)),
        name="modcast",
    )(x2, sc[:, None, :], sh[:, None, :])


def _mm_body(*refs, na, nex, prologue, epilogue, tn, head_tiles):
    a_refs = refs[:na]
    b_ref = refs[na]
    ex = refs[na + 1:na + 1 + nex]
    o_ref = refs[na + 1 + nex]
    j = pl.program_id(1)
    if prologue is not None:
        a_sc = refs[na + 2 + nex]

        @pl.when(j == 0)
        def _():
            a_sc[...] = prologue(a_refs[0][...], ex).astype(a_sc.dtype)

        a_refs = (a_sc,)
    sub = MM_SUB_COLS if tn % MM_SUB_COLS == 0 else tn

    def store(s, y):
        if head_tiles is None:
            o_ref[:, s * sub:(s + 1) * sub] = y.astype(o_ref.dtype)
        else:
            for hh in range(sub // HEAD_DIM):
                for tt in range(y.shape[0] // head_tiles):
                    blk = y[tt * head_tiles:(tt + 1) * head_tiles, hh * HEAD_DIM:(hh + 1) * HEAD_DIM]
                    o_ref[0, s * (sub // HEAD_DIM) + hh, tt] = blk.T.astype(o_ref.dtype)

    def run(fn):
        for s in range(tn // sub):
            cols = slice(s * sub, (s + 1) * sub)
            acc = None
            off = 0
            for r in a_refs:
                kr = r.shape[1]
                part = jnp.dot(r[...], b_ref[off:off + kr, cols], preferred_element_type=F32)
                acc = part if acc is None else acc + part
                off += kr
            store(s, fn(acc, j, ex))

    if epilogue is None:
        run(lambda acc, j, ex: acc)
    else:
        for cond, fn in epilogue(j):
            pl.when(cond)(functools.partial(run, fn))


def _mm(a_list, b, *, tm, tn, out_dtype, name, a_specs=None, prologue=None, epilogue=None,
        extras=(), head_tiles=None, seq=None):
    m = a_list[0].shape[0]
    k, n = b.shape
    if a_specs is None:
        a_specs = [pl.BlockSpec((tm, a.shape[1]), lambda i, j: (i, 0)) for a in a_list]
    ex_arrays = [e[0] for e in extras]
    ex_specs = [e[1] for e in extras]
    scratch = [pltpu.VMEM((tm, k), CDT)] if prologue is not None else []
    body = functools.partial(_mm_body, na=len(a_list), nex=len(extras), prologue=prologue,
                             epilogue=epilogue, tn=tn, head_tiles=head_tiles)
    if head_tiles is None:
        out_specs = pl.BlockSpec((tm, tn), lambda i, j: (i, j))
        out_shape = jax.ShapeDtypeStruct((m, n), out_dtype)
    else:
        per = seq // tm
        out_specs = pl.BlockSpec((1, tn // HEAD_DIM, tm // head_tiles, HEAD_DIM, head_tiles),
                                 lambda i, j: (i // per, j, i % per, 0, 0))
        out_shape = jax.ShapeDtypeStruct((m // seq, n // HEAD_DIM, seq // head_tiles, HEAD_DIM, head_tiles),
                                         out_dtype)
    return pl.pallas_call(
        body,
        grid=(m // tm, n // tn),
        in_specs=a_specs + [pl.BlockSpec((k, tn), lambda i, j: (0, j))] + ex_specs,
        out_specs=out_specs,
        out_shape=out_shape,
        scratch_shapes=scratch,
        compiler_params=_params(("arbitrary", "arbitrary")),
        name=name,
    )(*a_list, b, *ex_arrays)


def _rot(acc, cos, sa, sb, shift):
    tn = acc.shape[1]
    reps = tn // cos.shape[1]

    def rep(t):
        return t if reps == 1 else jnp.concatenate([t] * reps, axis=1)

    return (acc * rep(cos) + pltpu.roll(acc, tn - shift, 1) * rep(sa)
            + pltpu.roll(acc, shift, 1) * rep(sb))


def _rope_epilogue(shift, n_rope, n_scaled, scale):
    def roped(acc, j, ex):
        y = _rot(acc, ex[0][...], ex[1][...], ex[2][...], shift)
        return y * jnp.where(j < n_scaled, scale, 1.0).astype(F32)

    def epi(j):
        return [(j < n_rope, roped), (j >= n_rope, lambda acc, j, ex: acc)]

    return epi


def _mla_q_epilogue(scale):
    assert MM_SUB_COLS == MLA_QK_PAD

    def fn(acc, j, ex):
        rot = _rot(acc[:, LANES:], ex[0][...], ex[1][...], ex[2][...], MLA_ROPE // 2)
        return jnp.concatenate([acc[:, :LANES], rot], axis=1) * scale

    return lambda j: [(j >= 0, fn)]


def _krope_gate_epilogue(j):
    def fn(acc, j, ex):
        y = _rot(acc, ex[0][...], ex[1][...], ex[2][...], MLA_ROPE // 2)
        lane = lax.broadcasted_iota(jnp.int32, acc.shape, 1)
        return jnp.where(lane < LANES, y, 1.0 / (1.0 + jnp.exp(-acc)))

    return [(j >= 0, fn)]


def _rms_prologue(eps):
    def pro(a, ex):
        g = ex[-1][...]
        y = a * lax.rsqrt(jnp.mean(a * a, axis=-1, keepdims=True) + eps)
        return y * g

    return pro


def _relu2_epilogue(j):
    def fn(acc, j, ex):
        r = jnp.maximum(acc, 0.0)
        return r * r

    return [(j >= 0, fn)]


def _mmk_body(a_ref, b_ref, o_ref, acc_ref):
    kk = pl.program_id(2)

    @pl.when(kk == 0)
    def _():
        acc_ref[...] = jnp.zeros_like(acc_ref)

    acc_ref[...] += jnp.dot(a_ref[...], b_ref[...], preferred_element_type=F32)

    @pl.when(kk == pl.num_programs(2) - 1)
    def _():
        o_ref[...] = acc_ref[...].astype(o_ref.dtype)


def _mmk(a, b, *, tm, tn, tk, out_dtype, name):
    m, k = a.shape
    n = b.shape[1]
    return pl.pallas_call(
        _mmk_body,
        grid=(m // tm, n // tn, k // tk),
        in_specs=[pl.BlockSpec((tm, tk), lambda i, j, kk: (i, kk)),
                  pl.BlockSpec((tk, tn), lambda i, j, kk: (kk, j))],
        out_specs=pl.BlockSpec((tm, tn), lambda i, j, kk: (i, j)),
        out_shape=jax.ShapeDtypeStruct((m, n), out_dtype),
        scratch_shapes=[pltpu.VMEM((tm, tn), F32)],
        compiler_params=_params(("arbitrary", "arbitrary", "arbitrary")),
        name=name,
    )(a, b)


def _ln_body(x_ref, a_ref, gate_ref, g_ref, b_ref, *rest, alpha, with_mod):
    y = alpha * x_ref[...] + (1.0 + gate_ref[0]) * a_ref[...]
    mu = jnp.mean(y, axis=-1, keepdims=True)
    yc = y - mu
    var = jnp.mean(yc * yc, axis=-1, keepdims=True)
    out = yc * lax.rsqrt(var + 1e-5) * g_ref[...] + b_ref[...]
    if with_mod:
        sc_ref, sh_ref, o_ref, h_ref = rest
        o_ref[...] = out
        h_ref[...] = (out * (1.0 + sc_ref[0]) + sh_ref[0]).astype(h_ref.dtype)
    else:
        (o_ref,) = rest
        o_ref[...] = out


def _res_ln(x2, a2, gate, g, b, seq, alpha, mod=None):
    t, d = x2.shape
    ts = _tile(seq, (256, 128, 64, 8))
    per = seq // ts
    row = pl.BlockSpec((ts, d), lambda i: (i, 0))
    vec = pl.BlockSpec((1, 1, d), lambda i: (i // per, 0, 0))
    par = pl.BlockSpec((1, d), lambda i: (0, 0))
    ins = [x2, a2, gate[:, None, :], g.reshape(1, d), b.reshape(1, d)]
    specs = [row, row, vec, par, par]
    if mod is None:
        out_shape = jax.ShapeDtypeStruct((t, d), F32)
        out_specs = row
    else:
        ins += [mod[0][:, None, :], mod[1][:, None, :]]
        specs += [vec, vec]
        out_shape = (jax.ShapeDtypeStruct((t, d), F32), jax.ShapeDtypeStruct((t, d), CDT))
        out_specs = (row, row)
    return pl.pallas_call(
        functools.partial(_ln_body, alpha=alpha, with_mod=mod is not None),
        grid=(t // ts,),
        in_specs=specs,
        out_specs=out_specs,
        out_shape=out_shape,
        compiler_params=_params(("arbitrary",)),
        name="res_ln",
    )(*ins)


def _compress_body(x_ref, p0_ref, p1_ref, w1a_ref, w1b_ref, w2_ref, o_ref):
    x = x_ref[0, 0].astype(F32)
    nc = x.shape[0]
    xa = (x + p0_ref[...]).astype(CDT)
    xb = (x + p1_ref[...]).astype(CDT)
    a = jnp.dot(xa, w1a_ref[...], preferred_element_type=F32)
    bm = jnp.dot(xb, w1b_ref[...], preferred_element_type=F32)
    pre = a + pltpu.roll(bm, nc - 1, 0)
    hid = 0.5 * pre * (1.0 + jnp.tanh(math.sqrt(2.0 / math.pi) * (pre + 0.044715 * pre * pre * pre)))
    o_ref[0, 0] = jnp.dot(hid.astype(CDT), w2_ref[...], preferred_element_type=F32).astype(o_ref.dtype)


def _compress(xc, pos, w1, w2):
    bsz, g, nc, wid = xc.shape
    half = wid
    p0 = pos[:CMP_STRIDE].reshape(1, wid)
    p1 = pos[CMP_STRIDE:].reshape(1, wid)
    w1c = w1.astype(CDT)
    full = lambda shape: pl.BlockSpec(shape, lambda b, gi: (0,) * len(shape))
    return pl.pallas_call(
        _compress_body,
        grid=(bsz, g),
        in_specs=[pl.BlockSpec((1, 1, nc, wid), lambda b, gi: (b, gi, 0, 0)),
                  full((1, wid)), full((1, wid)),
                  full((half, w1.shape[1])), full((half, w1.shape[1])),
                  full(w2.shape)],
        out_specs=pl.BlockSpec((1, 1, nc, HEAD_DIM), lambda b, gi: (b, gi, 0, 0)),
        out_shape=jax.ShapeDtypeStruct((bsz, g, nc, HEAD_DIM), CDT),
        compiler_params=_params(("arbitrary", "arbitrary")),
        name="compress",
    )(xc, p0, p1, w1c[:half], w1c[half:], w2.astype(CDT))


def _cmp_body(q_ref, kc_ref, vc_ref, ov_ref, gate_ref, o_ref, bias_ref, *, tq, r_heads, nb, ncmp):
    gi = pl.program_id(1)
    i = pl.program_id(2)
    kc = kc_ref[0, 0]
    vc = vc_ref[0, 0]
    nc = kc.shape[0]
    t = i * tq + lax.broadcasted_iota(jnp.int32, (tq, nc), 0)
    cidx = lax.broadcasted_iota(jnp.int32, (tq, nc), 1)
    valid = (cidx * CMP_STRIDE + (CMP_LEN - 1) <= t) & (cidx < ncmp)
    validf = valid.astype(F32)
    real = cidx < ncmp
    gates = gate_ref[0]
    glane = lax.broadcasted_iota(jnp.int32, gates.shape, 1)
    psum = jnp.zeros((tq, nc), F32)
    for r in range(r_heads):
        q = q_ref[0, :, r * HEAD_DIM:(r + 1) * HEAD_DIM]
        s = lax.dot_general(q, kc, (((1,), (1,)), ((), ())), preferred_element_type=F32)
        s = jnp.where(valid, s, jnp.where(real, NEG, -jnp.inf))
        m = jnp.max(s, axis=-1, keepdims=True)
        e = jnp.exp2(s - m)
        p = e / jnp.sum(e, axis=-1, keepdims=True) * validf
        psum = psum + p
        o = jnp.dot(p.astype(CDT), vc, preferred_element_type=F32)
        col = ((gi * r_heads + r) * 3 + 0)
        gcol = jnp.sum(jnp.where(glane == col, gates, 0.0), axis=-1, keepdims=True)
        o_ref[0, :, r * HEAD_DIM:(r + 1) * HEAD_DIM] = (gcol * o).astype(o_ref.dtype)

    p_hi = psum.astype(CDT)
    p_lo = (psum - p_hi.astype(F32)).astype(CDT)
    imp = (jnp.dot(p_hi, ov_ref[...], preferred_element_type=F32)
           + jnp.dot(p_lo, ov_ref[...], preferred_element_type=F32))
    tt = i * tq + lax.broadcasted_iota(jnp.int32, (tq, LANES), 0)
    jj = lax.broadcasted_iota(jnp.int32, (tq, LANES), 1)
    cur = tt // SEL_LEN
    forced = (jj == 0) | ((jj <= cur) & (jj > cur - N_LOCAL_FORCED))
    imp = jnp.where(jj > cur, NEG, imp + jnp.where(forced, FORCED_BONUS, 0.0))
    v = imp.T[:nb]
    ridx = lax.broadcasted_iota(jnp.int32, (nb, tq), 0)
    sel = jnp.zeros((nb, tq), F32)
    for _ in range(min(N_SEL, nb)):
        mx = jnp.max(v, axis=0, keepdims=True)
        first = jnp.min(jnp.where(v == mx, ridx, nb), axis=0, keepdims=True)
        hit = ridx == first
        sel = jnp.where(hit, 1.0, sel)
        v = jnp.where(hit, -jnp.inf, v)
    bias_t = jnp.where(sel > 0.0, 0.0, NEG)
    if nb < LANES:
        bias_t = jnp.concatenate([bias_t, jnp.zeros((LANES - nb, tq), F32)], axis=0)
    bias_ref[0, 0] = bias_t.T.astype(bias_ref.dtype)


def _cmp_attn(qkv, kc, vc, overlap, gates, *, g_groups, r_heads, seq, ncmp):
    bsz = qkv.shape[0]
    nc = kc.shape[2]
    nb = seq // SEL_LEN
    tq = _tile(seq, (512, 256, 128))
    qw = r_heads * HEAD_DIM
    body = functools.partial(_cmp_body, tq=tq, r_heads=r_heads, nb=nb, ncmp=ncmp)
    return pl.pallas_call(
        body,
        grid=(bsz, g_groups, seq // tq),
        in_specs=[pl.BlockSpec((1, tq, qw), lambda b, g, i: (b, i, g)),
                  pl.BlockSpec((1, 1, nc, HEAD_DIM), lambda b, g, i: (b, g, 0, 0)),
                  pl.BlockSpec((1, 1, nc, HEAD_DIM), lambda b, g, i: (b, g, 0, 0)),
                  pl.BlockSpec((nc, LANES), lambda b, g, i: (0, 0)),
                  pl.BlockSpec((1, tq, LANES), lambda b, g, i: (b, i, 1))],
        out_specs=(pl.BlockSpec((1, tq, qw), lambda b, g, i: (b, i, g)),
                   pl.BlockSpec((1, 1, tq, LANES), lambda b, g, i: (b, g, i, 0))),
        out_shape=(jax.ShapeDtypeStruct((bsz, seq, g_groups * qw), F32),
                   jax.ShapeDtypeStruct((bsz, g_groups, seq, LANES), CDT)),
        compiler_params=_params(("arbitrary", "arbitrary", "arbitrary")),
        name="cmp_attn",
    )(qkv, kc, vc, overlap, gates)


ACC_ROWS = HEAD_DIM + 16


def _with_ones_row(vt):
    ones_row = jnp.where(lax.broadcasted_iota(jnp.int32, (ACC_ROWS - HEAD_DIM, vt.shape[1]), 0) == 0,
                         1.0, 0.0).astype(CDT)
    return jnp.concatenate([vt, ones_row], axis=0)


def _flash_init(m_ref, acc_ref):
    m_ref[...] = jnp.full(m_ref.shape, -jnp.inf, F32)
    acc_ref[...] = jnp.zeros(acc_ref.shape, F32)


def _nsa_body(q_ref, bias_ref, ks_ref, e_ref, vs_ref, kw_ref, vw_ref, gate_ref, prev_ref, o_ref,
              qc_ref, s0_ref, s1_ref, cm_ref, m_ref, acc_ref, *, tq, r_heads, n_win):
    s_refs = (s0_ref, s1_ref)
    gi = pl.program_id(1)
    i = pl.program_id(2)
    tk = tq
    cols = r_heads * tq
    for r in range(r_heads):
        qc_ref[r * tq:(r + 1) * tq, :] = jnp.concatenate(
            [q_ref[0, :, r * HEAD_DIM:(r + 1) * HEAD_DIM], bias_ref[0, 0]], axis=1)
    _flash_init(m_ref, acc_ref)

    def local_iotas():
        key = lax.broadcasted_iota(jnp.int32, (tk, cols), 0)
        tok = lax.broadcasted_iota(jnp.int32, (tk, cols), 1) & (tq - 1)
        return key, tok

    def put(s, buf):
        s_refs[buf][...] = s
        cm_ref[buf] = jnp.max(s, axis=0, keepdims=True)

    def update(branch, vt, buf):
        m_prev = m_ref[branch]
        m_new = jnp.maximum(m_prev, cm_ref[buf])
        p = jnp.exp2(s_refs[buf][...] - m_new).astype(CDT)
        acc_ref[branch] = (jnp.exp2(m_prev - m_new) * acc_ref[branch]
                           + jnp.dot(_with_ones_row(vt), p, preferred_element_type=F32))
        m_ref[branch] = m_new

    def sel_scores(j, buf, masked):
        off = pl.multiple_of(j * tk, tk)
        kmat = jnp.concatenate([ks_ref[0, pl.ds(off, tk), :], e_ref[pl.ds(off, tk), :]], axis=1)
        s = lax.dot_general(kmat, qc_ref[...], (((1,), (1,)), ((), ())), preferred_element_type=F32)
        if masked:
            key, tok = local_iotas()
            s = jnp.where(key <= tok, s, NEG)
        put(s, buf)

    def sel_update(j, buf):
        update(0, vs_ref[0, 0, j], buf)

    def win_tile(dd):
        back = n_win - 1 - dd
        return back, jnp.maximum(i - back, 0)

    def win_scores(dd, buf):
        back, j = win_tile(dd)
        off = pl.multiple_of(j * tk, tk)
        s = lax.dot_general(kw_ref[0, pl.ds(off, tk), :], qc_ref[:, :HEAD_DIM], (((1,), (1,)), ((), ())),
                            preferred_element_type=F32)
        key, tok = local_iotas()
        if back * tk - (tk - 1) < 0:
            s = jnp.where(key - back * tk <= tok, s, NEG)
        if back > 0 or tq - 1 + back * tk >= WINDOW:
            reach = WINDOW - back * tk
            if back > 0:
                reach = reach - jnp.where(i >= back, 0, 2 * WINDOW + tq)
            s = jnp.where(tok < key + reach, s, NEG)
        put(s, buf)

    def win_update(dd, buf):
        update(1, vw_ref[0, 0, win_tile(dd)[1]], buf)

    def finish(fb):
        wbuf = lambda dd: fb if dd % 2 else 1 - fb
        win_scores(0, wbuf(0))
        sel_update(i, fb)
        for dd in range(1, n_win):
            win_scores(dd, wbuf(dd))
            win_update(dd - 1, wbuf(dd - 1))
        win_update(n_win - 1, wbuf(n_win - 1))

    @pl.when(i == 0)
    def _():
        sel_scores(0, 0, True)

    @pl.when(i > 0)
    def _():
        sel_scores(0, 0, False)

    n_pairs = jnp.maximum(i - 1, 0) // 2

    def body(jp, carry):
        j = 2 * jp
        sel_scores(j + 1, 1, False)
        sel_update(j, 0)
        sel_scores(j + 2, 0, False)
        sel_update(j + 1, 1)
        return carry

    lax.fori_loop(0, n_pairs, body, 0)
    j0 = 2 * n_pairs
    left = i - j0

    @pl.when(left == 1)
    def _():
        sel_scores(i, 1, True)
        sel_update(j0, 0)
        finish(1)

    @pl.when(left == 2)
    def _():
        sel_scores(j0 + 1, 1, False)
        sel_update(j0, 0)
        sel_scores(i, 0, True)
        sel_update(j0 + 1, 1)
        finish(0)

    @pl.when(left == 0)
    def _():
        finish(0)

    gates_t = gate_ref[0].T
    rid = lax.broadcasted_iota(jnp.int32, gates_t.shape, 0)
    for r in range(r_heads):
        cs = slice(r * tq, (r + 1) * tq)
        o = None
        for branch in (0, 1):
            acc = acc_ref[branch]
            col = (gi * r_heads + r) * 3 + 1 + branch
            grow = jnp.sum(jnp.where(rid == col, gates_t, 0.0), axis=0, keepdims=True)
            ob = acc[:HEAD_DIM, cs] * (grow / acc[HEAD_DIM:HEAD_DIM + 1, cs])
            o = ob if o is None else o + ob
        sl = slice(r * HEAD_DIM, (r + 1) * HEAD_DIM)
        o_ref[0, :, sl] = (prev_ref[0, :, sl] + o.T).astype(o_ref.dtype)


def _nsa_attn(qkv, ks_blk0, kw_blk0, vt, gates, prev, bias, e_mat, *, g_groups, r_heads, seq):
    bsz = qkv.shape[0]
    qw = r_heads * HEAD_DIM
    tq = tk = vt.shape[-1]
    cols = r_heads * tq
    kspec = lambda blk0: pl.BlockSpec((1, seq, HEAD_DIM), lambda b, g, i: (b, 0, blk0 + g))
    vspec = lambda h0: pl.BlockSpec((1, 1) + vt.shape[2:], lambda b, g, i: (b, h0 + g, 0, 0, 0))
    qspec = pl.BlockSpec((1, tq, qw), lambda b, g, i: (b, i, g))
    body = functools.partial(_nsa_body, tq=tq, r_heads=r_heads, n_win=(WINDOW + tk - 1) // tk + 1)
    return pl.pallas_call(
        body,
        grid=(bsz, g_groups, seq // tq),
        in_specs=[qspec, pl.BlockSpec((1, 1, tq, LANES), lambda b, g, i: (b, g, i, 0)),
                  kspec(ks_blk0), pl.BlockSpec((seq, LANES), lambda b, g, i: (0, 0)), vspec(0),
                  kspec(kw_blk0), vspec(g_groups),
                  pl.BlockSpec((1, tq, LANES), lambda b, g, i: (b, i, 1)), qspec],
        out_specs=qspec,
        out_shape=jax.ShapeDtypeStruct((bsz, seq, g_groups * qw), CDT),
        scratch_shapes=[pltpu.VMEM((cols, 2 * HEAD_DIM), CDT),
                        pltpu.VMEM((tk, cols), F32), pltpu.VMEM((tk, cols), F32),
                        pltpu.VMEM((2, 1, cols), F32),
                        pltpu.VMEM((2, 1, cols), F32),
                        pltpu.VMEM((2, ACC_ROWS, cols), F32)],
        compiler_params=_params(("arbitrary",) * 3),
        name="nsa_sel_win",
    )(qkv, bias, qkv, e_mat, vt, qkv, vt, gates, prev)


def _mla_body(q_ref, kn_ref, kr_ref, vt_ref, o_ref, *rest, tq, tk, hg):
    s_refs = rest[:hg]
    cm_ref, m_ref, acc_ref = rest[hg:]
    i = pl.program_id(2)
    _flash_init(m_ref, acc_ref)
    n_full = (i * tq) // tk

    def step(j, mask, c0):
        off = pl.multiple_of(j * tk, tk)
        kr = kr_ref[0, pl.ds(off, tk), :]

        def scores(c):
            kmat = jnp.concatenate([kn_ref[0, pl.ds(off, tk), c * MLA_NOPE:(c + 1) * MLA_NOPE], kr], axis=1)
            s = lax.dot_general(kmat, q_ref[0, c0:, c * MLA_QK_PAD:(c + 1) * MLA_QK_PAD],
                                (((1,), (1,)), ((), ())), preferred_element_type=F32)
            if mask is not None:
                s = jnp.where(mask, s, NEG)
            s_refs[c][:, c0:] = s
            cm_ref[c, :, c0:] = jnp.max(s, axis=0, keepdims=True)

        def update(c):
            m_prev = m_ref[c, :, c0:]
            m_new = jnp.maximum(m_prev, cm_ref[c, :, c0:])
            p = jnp.exp2(s_refs[c][:, c0:] - m_new).astype(CDT)
            acc_ref[c, :, c0:] = (jnp.exp2(m_prev - m_new) * acc_ref[c, :, c0:]
                                  + jnp.dot(_with_ones_row(vt_ref[0, c, j]), p, preferred_element_type=F32))
            m_ref[c, :, c0:] = m_new

        scores(0)
        for c in range(1, hg):
            scores(c)
            update(c - 1)
        update(hg - 1)

    def full_tile(j, carry):
        step(j, None, 0)
        return carry

    lax.fori_loop(0, n_full, full_tile, 0)
    for dd in range(tq // tk):
        w = tq - dd * tk
        mask = (lax.broadcasted_iota(jnp.int32, (tk, w), 0) <= lax.broadcasted_iota(jnp.int32, (tk, w), 1))
        step(n_full + dd, mask, dd * tk)
    for c in range(hg):
        acc = acc_ref[c]
        o = acc[:HEAD_DIM] * (1.0 / acc[HEAD_DIM:HEAD_DIM + 1])
        o_ref[0, :, c * MLA_V:(c + 1) * MLA_V] = o.T.astype(o_ref.dtype)


def _mla_attn(q, kv, kr, vt, *, heads, seq):
    bsz = q.shape[0]
    tk = vt.shape[-1]
    tq = _tile(seq, (1024, 512, 256, 128))
    hg = 4 if heads % 4 == 0 else 1
    body = functools.partial(_mla_body, tq=tq, tk=tk, hg=hg)
    return pl.pallas_call(
        body,
        grid=(bsz, heads // hg, seq // tq),
        in_specs=[pl.BlockSpec((1, tq, hg * MLA_QK_PAD), lambda b, h, i: (b, i, h)),
                  pl.BlockSpec((1, seq, hg * MLA_NOPE), lambda b, h, i: (b, 0, h)),
                  pl.BlockSpec((1, seq, LANES), lambda b, h, i: (b, 0, 0)),
                  pl.BlockSpec((1, hg) + vt.shape[2:], lambda b, h, i: (b, h, 0, 0, 0))],
        out_specs=pl.BlockSpec((1, tq, hg * MLA_V), lambda b, h, i: (b, i, h)),
        out_shape=jax.ShapeDtypeStruct((bsz, seq, heads * MLA_V), CDT),
        scratch_shapes=[pltpu.VMEM((tk, tq), F32)] * hg + [
            pltpu.VMEM((hg, 1, tq), F32), pltpu.VMEM((hg, 1, tq), F32), pltpu.VMEM((hg, ACC_ROWS, tq), F32)],
        compiler_params=_params(("arbitrary",) * 3),
        name="mla_attn",
    )(q, kv, kr, vt)


def _rope_tables(positions, rot_dim, period, offset):
    half = rot_dim // 2
    inv = jnp.power(ROPE_THETA, -jnp.arange(0, rot_dim, 2, dtype=F32) / rot_dim)
    ang = positions.astype(F32).reshape(-1)[:, None] * inv
    cos, sin = jnp.cos(ang), jnp.sin(ang)
    t = ang.shape[0]
    ones_l = jnp.ones((t, offset), F32)
    zeros_l = jnp.zeros((t, offset), F32)
    rest = period - offset - rot_dim
    ct = jnp.concatenate([ones_l, cos, cos, jnp.ones((t, rest), F32)], axis=1)
    sa = jnp.concatenate([zeros_l, -sin, jnp.zeros((t, half + rest), F32)], axis=1)
    sb = jnp.concatenate([zeros_l, jnp.zeros((t, half), F32), sin, jnp.zeros((t, rest), F32)], axis=1)
    return ct, sa, sb


def _table_extras(tables, tm):
    return [(tb, pl.BlockSpec((tm, tb.shape[1]), lambda i, j: (i, 0))) for tb in tables]


def _layer(x2, mod, positions, w_in, pos_k, pos_v, k1, k2, v1, v2, q_norm, kv_norm, w_uq, w_ukv,
           w_out, ln1_g, ln1_b, w_ff1, w_ff2, ln2_g, ln2_b, *, bsz, seq, alpha):
    t, d = x2.shape
    n_heads = d // HEAD_DIM
    nsa_h = n_heads // 2
    mla_h = n_heads - nsa_h
    q_rank = q_norm.shape[0]
    kv_rank = kv_norm.shape[0]
    d_in = w_in.shape[1]
    g_groups = (d_in - nsa_h * HEAD_DIM - nsa_h * 3 - q_rank - kv_rank - MLA_ROPE) // (6 * HEAD_DIM)
    r_heads = nsa_h // g_groups
    gw = g_groups * HEAD_DIM
    qw = nsa_h * HEAD_DIM
    sh_a, sc_a, g_a, sh_m, sc_m, g_m = jnp.split(mod, N_ADA, axis=-1)

    o_q, o_kv = 0, qw
    o_gate = o_kv + 6 * gw
    o_cq = o_gate + nsa_h * 3
    o_ckv = o_cq + q_rank
    o_kr = o_ckv + kv_rank
    kvcol = lambda idx: w_in[:, o_kv + idx * gw:o_kv + (idx + 1) * gw]
    w1c = jnp.concatenate([w_in[:, o_q:o_q + qw], kvcol(0), kvcol(2), kvcol(4)], axis=1).astype(CDT)
    wvc = jnp.concatenate([kvcol(3), kvcol(5)], axis=1).astype(CDT)
    w2c = jnp.concatenate([w_in[:, o_cq:o_cq + q_rank], w_in[:, o_ckv:o_ckv + kv_rank], kvcol(1)],
                          axis=1).astype(CDT)
    zpad = lambda n: jnp.zeros((d, n), w_in.dtype)
    w3c = jnp.concatenate([w_in[:, o_kr:o_kr + MLA_ROPE], zpad(LANES - MLA_ROPE),
                           w_in[:, o_gate:o_gate + nsa_h * 3], zpad(LANES - nsa_h * 3)],
                          axis=1).astype(CDT)

    h = _modcast(x2, sc_a, sh_a, seq)

    tm = _tile(t, (1024, 512, 256, 128))
    nsa_tabs = _rope_tables(positions, PARTIAL_ROT, LANES, 0)
    n1 = w1c.shape[1]
    tn1 = _tile(gw, (512, 256, 128))
    n_rope_cols = qw + 3 * gw
    scale = HEAD_DIM ** -0.5 * LOG2E
    p1 = _mm([h], w1c, tm=tm, tn=tn1, out_dtype=CDT, name="in_proj_rot",
             epilogue=_rope_epilogue(PARTIAL_ROT // 2, n_rope_cols // tn1, qw // tn1, scale),
             extras=_table_extras(nsa_tabs, tm))
    tk = _tile(seq, (512, 256, 128))
    vt_nsa = _mm([h], wvc, tm=tm, tn=_tile(2 * gw, (512, 256, 128)), out_dtype=CDT, name="in_proj_vt",
                 head_tiles=tk, seq=seq)
    n2 = w2c.shape[1]
    tn2 = _tile(n2, (512, 256, 128))
    p2 = _mm([h], w2c, tm=tm, tn=tn2, out_dtype=F32, name="in_proj_lat")
    kr_tabs = _rope_tables(positions, MLA_ROPE, LANES, 0)
    p3 = _mm([h], w3c, tm=tm, tn=2 * LANES, out_dtype=F32, name="in_proj_kr_gate",
             epilogue=_krope_gate_epilogue, extras=_table_extras(kr_tabs, tm))

    qkv = p1.reshape(bsz, seq, n1)
    gates = p3.reshape(bsz, seq, 2 * LANES)

    nchunk = seq // CMP_STRIDE
    ncmp = (seq - CMP_LEN) // CMP_STRIDE + 1
    def chunked(a):
        return (a.reshape(bsz, nchunk, CMP_STRIDE, g_groups, HEAD_DIM).transpose(0, 3, 1, 2, 4)
                .reshape(bsz, g_groups, nchunk, CMP_STRIDE * HEAD_DIM))
    kc = _compress(chunked(qkv[:, :, qw:qw + gw]), pos_k, k1, k2)
    v_cmp = p2.reshape(bsz, seq, n2)[:, :, q_rank + kv_rank:]
    vc = _compress(chunked(v_cmp), pos_v, v1, v2)

    nb = seq // SEL_LEN
    c_start = np.arange(nchunk) * CMP_STRIDE
    b_start = np.arange(LANES) * SEL_LEN
    overlap = ((c_start[:, None] < b_start[None, :] + SEL_LEN) &
               (c_start[:, None] + CMP_LEN > b_start[None, :]) &
               (np.arange(nchunk)[:, None] < ncmp) & (np.arange(LANES)[None, :] < nb))
    overlap = jnp.asarray(overlap.astype(np.float32)).astype(CDT)
    o_cmp, bias = _cmp_attn(qkv, kc, vc, overlap, gates, g_groups=g_groups, r_heads=r_heads,
                            seq=seq, ncmp=ncmp)

    e_mat = jnp.asarray((np.arange(seq)[:, None] // SEL_LEN == np.arange(LANES)[None, :])
                        .astype(np.float32)).astype(CDT)
    blk = lambda col: col // HEAD_DIM
    o_nsa = _nsa_attn(qkv, blk(qw + gw), blk(qw + 2 * gw), vt_nsa, gates, o_cmp, bias, e_mat,
                      g_groups=g_groups, r_heads=r_heads, seq=seq)

    qk_dim = MLA_NOPE + MLA_ROPE
    wq = w_uq.reshape(q_rank, mla_h, qk_dim)
    wq = jnp.concatenate([wq, jnp.zeros((q_rank, mla_h, MLA_QK_PAD - qk_dim), wq.dtype)], axis=-1)
    wq = wq.reshape(q_rank, mla_h * MLA_QK_PAD).astype(CDT)
    wkv = w_ukv.reshape(kv_rank, mla_h, MLA_NOPE + MLA_V)
    wk = wkv[:, :, :MLA_NOPE].reshape(kv_rank, mla_h * MLA_NOPE).astype(CDT)
    wv = wkv[:, :, MLA_NOPE:].reshape(kv_rank, mla_h * MLA_V).astype(CDT)
    tnq = _tile(mla_h * MLA_QK_PAD, (512, 256))
    q_mla = _mm([p2], wq, tm=tm, tn=tnq, out_dtype=CDT, name="mla_uq",
                a_specs=[pl.BlockSpec((tm, q_rank), lambda i, j: (i, 0))],
                prologue=_rms_prologue(1e-6),
                epilogue=_mla_q_epilogue(qk_dim ** -0.5 * LOG2E),
                extras=_table_extras(kr_tabs, tm)
                + [(q_norm.reshape(1, q_rank), pl.BlockSpec((1, q_rank), lambda i, j: (0, 0)))])
    tnkv = _tile(wk.shape[1], (512, 256, 128))
    ckv_args = dict(tm=tm, tn=tnkv, out_dtype=CDT, prologue=_rms_prologue(1e-6),
                    a_specs=[pl.BlockSpec((tm, kv_rank), lambda i, j: (i, q_rank // kv_rank))],
                    extras=[(kv_norm.reshape(1, kv_rank), pl.BlockSpec((1, kv_rank), lambda i, j: (0, 0)))])
    k_mla = _mm([p2], wk, name="mla_uk", **ckv_args)
    vt_mla = _mm([p2], wv, name="mla_uv", head_tiles=tk, seq=seq, **ckv_args)
    o_mla = _mla_attn(q_mla.reshape(bsz, seq, -1), k_mla.reshape(bsz, seq, -1),
                      gates[:, :, :LANES].astype(CDT), vt_mla, heads=mla_h, seq=seq)

    tno = _tile(d, (1024, 512, 256, 128))
    a = _mm([o_nsa.reshape(t, qw), o_mla.reshape(t, mla_h * MLA_V)], w_out.astype(CDT), tm=tm, tn=tno,
            out_dtype=F32, name="out_proj")
    x1, h2 = _res_ln(x2, a, g_a, ln1_g, ln1_b, seq, alpha, mod=(sc_m, sh_m))
    d_ff = w_ff1.shape[1]
    f1 = _mm([h2], w_ff1.astype(CDT), tm=tm, tn=_tile(d_ff, (1024, 512, 256, 128)), out_dtype=CDT,
             name="ff1", epilogue=_relu2_epilogue)
    f2 = _mmk(f1, w_ff2.astype(CDT), tm=tm, tn=tno, tk=_tile(d_ff, (4096, 2048, 1024, 512)),
              out_dtype=F32, name="ff2")
    return _res_ln(x1, f2, g_m, ln2_g, ln2_b, seq, alpha)


def kernel(x, c, positions, w_ada, b_ada, w_in, nsa_pos_k, nsa_pos_v, nsa_cmp_k1, nsa_cmp_k2, nsa_cmp_v1, nsa_cmp_v2, mla_q_norm, mla_kv_norm, mla_w_uq, mla_w_ukv, w_out, ln1_g, ln1_b, w_ff1, w_ff2, ln2_g, ln2_b):
    bsz, seq, d = x.shape
    depth = w_ada.shape[0]
    alpha = (2 * depth) ** 0.25
    x2 = x.reshape(bsz * seq, d)
    for layer in range(depth):
        mod = _ada(c, w_ada[layer], b_ada[layer])
        x2 = _layer(x2, mod, positions, w_in[layer], nsa_pos_k[layer], nsa_pos_v[layer],
                    nsa_cmp_k1[layer], nsa_cmp_k2[layer], nsa_cmp_v1[layer], nsa_cmp_v2[layer],
                    mla_q_norm[layer], mla_kv_norm[layer], mla_w_uq[layer], mla_w_ukv[layer],
                    w_out[layer], ln1_g[layer], ln1_b[layer], w_ff1[layer], w_ff2[layer],
                    ln2_g[layer], ln2_b[layer], bsz=bsz, seq=seq, alpha=alpha)
    return x2.reshape(bsz, seq, d)
```

```python
import functools
import math

import numpy as np
import jax
import jax.numpy as jnp
from jax import lax
from jax.experimental import pallas as pl
from jax.experimental.pallas import tpu as pltpu

HEAD_DIM = 128
CMP_LEN = 32
CMP_STRIDE = 16
SEL_LEN = 64
N_SEL = 16
N_LOCAL_FORCED = 2
FORCED_BONUS = 1e4
WINDOW = 512
MLA_NOPE = 128
MLA_ROPE = 64
MLA_V = 128
ROPE_THETA = 500000.0
PARTIAL_ROT = HEAD_DIM // 4
NEG = -1e30
LOG2E = math.log2(math.e)
N_ADA = 6
LANES = 128
MLA_QK_PAD = 2 * LANES
VMEM_LIMIT = 56 * 1024 * 1024
MM_SUB_COLS = 256

F32 = jnp.float32
CDT = jnp.bfloat16


def _params(sem):
    return pltpu.CompilerParams(dimension_semantics=sem, vmem_limit_bytes=VMEM_LIMIT)


def _tile(n, cands):
    for c in cands:
        if n % c == 0:
            return c
    return n


def _ada_body(c_ref, w_ref, b_ref, o_ref):
    c = c_ref[...]
    cond = c * (1.0 / (1.0 + jnp.exp(-c)))
    o_ref[...] = jnp.dot(cond.astype(CDT), w_ref[...].astype(CDT),
                         preferred_element_type=F32) + b_ref[...]


def _ada(c, w, b):
    bsz, d = c.shape
    n = w.shape[1]
    tn = _tile(n, (512, 256, 128))
    return pl.pallas_call(
        _ada_body,
        grid=(n // tn,),
        in_specs=[pl.BlockSpec((bsz, d), lambda j: (0, 0)),
                  pl.BlockSpec((d, tn), lambda j: (0, j)),
                  pl.BlockSpec((1, tn), lambda j: (0, j))],
        out_specs=pl.BlockSpec((bsz, tn), lambda j: (0, j)),
        out_shape=jax.ShapeDtypeStruct((bsz, n), F32),
        compiler_params=_params(("arbitrary",)),
        name="ada",
    )(c, w, b.reshape(1, n))


def _mm_body(*refs, na, nex, prologue, epilogue, tn, head_tiles):
    a_refs = refs[:na]
    b_ref = refs[na]
    ex = refs[na + 1:na + 1 + nex]
    o_ref = refs[na + 1 + nex]
    j = pl.program_id(1)
    if prologue is not None:
        a_sc = refs[na + 2 + nex]

        @pl.when(j == 0)
        def _():
            a_sc[...] = prologue(a_refs[0][...], ex).astype(a_sc.dtype)

        a_refs = (a_sc,)
    sub = MM_SUB_COLS if tn % MM_SUB_COLS == 0 else tn

    def store(s, y):
        if head_tiles is None:
            o_ref[:, s * sub:(s + 1) * sub] = y.astype(o_ref.dtype)
        else:
            for hh in range(sub // HEAD_DIM):
                for tt in range(y.shape[0] // head_tiles):
                    blk = y[tt * head_tiles:(tt + 1) * head_tiles, hh * HEAD_DIM:(hh + 1) * HEAD_DIM]
                    o_ref[0, s * (sub // HEAD_DIM) + hh, tt] = blk.T.astype(o_ref.dtype)

    def run(fn):
        for s in range(tn // sub):
            cols = slice(s * sub, (s + 1) * sub)
            acc = None
            off = 0
            for r in a_refs:
                kr = r.shape[1]
                part = jnp.dot(r[...], b_ref[off:off + kr, cols], preferred_element_type=F32)
                acc = part if acc is None else acc + part
                off += kr
            store(s, fn(acc, j, ex))

    if epilogue is None:
        run(lambda acc, j, ex: acc)
    else:
        for cond, fn in epilogue(j):
            pl.when(cond)(functools.partial(run, fn))


def _mm(a_list, b, *, tm, tn, out_dtype, name, a_specs=None, prologue=None, epilogue=None,
        extras=(), head_tiles=None, seq=None, emit_a=False):
    m = a_list[0].shape[0]
    k, n = b.shape
    if a_specs is None:
        a_specs = [pl.BlockSpec((tm, a.shape[1]), lambda i, j: (i, 0)) for a in a_list]
    ex_arrays = [e[0] for e in extras]
    ex_specs = [e[1] for e in extras]
    scratch = [pltpu.VMEM((tm, k), CDT)] if prologue is not None and not emit_a else []
    body = functools.partial(_mm_body, na=len(a_list), nex=len(extras), prologue=prologue,
                             epilogue=epilogue, tn=tn, head_tiles=head_tiles)
    if head_tiles is None:
        out_specs = pl.BlockSpec((tm, tn), lambda i, j: (i, j))
        out_shape = jax.ShapeDtypeStruct((m, n), out_dtype)
    else:
        per = seq // tm
        out_specs = pl.BlockSpec((1, tn // HEAD_DIM, tm // head_tiles, HEAD_DIM, head_tiles),
                                 lambda i, j: (i // per, j, i % per, 0, 0))
        out_shape = jax.ShapeDtypeStruct((m // seq, n // HEAD_DIM, seq // head_tiles, HEAD_DIM, head_tiles),
                                         out_dtype)
    if emit_a:
        out_specs = (out_specs, pl.BlockSpec((tm, k), lambda i, j: (i, 0)))
        out_shape = (out_shape, jax.ShapeDtypeStruct((m, k), CDT))
    return pl.pallas_call(
        body,
        grid=(m // tm, n // tn),
        in_specs=a_specs + [pl.BlockSpec((k, tn), lambda i, j: (0, j))] + ex_specs,
        out_specs=out_specs,
        out_shape=out_shape,
        scratch_shapes=scratch,
        compiler_params=_params(("arbitrary", "arbitrary")),
        name=name,
    )(*a_list, b, *ex_arrays)


def _rot(acc, cos, sa, sb, shift):
    tn = acc.shape[1]
    reps = tn // cos.shape[1]

    def rep(t):
        return t if reps == 1 else jnp.concatenate([t] * reps, axis=1)

    return (acc * rep(cos) + pltpu.roll(acc, tn - shift, 1) * rep(sa)
            + pltpu.roll(acc, shift, 1) * rep(sb))


def _rope_epilogue(shift, n_rope, n_scaled, scale):
    def roped(acc, j, ex):
        y = _rot(acc, ex[0][...], ex[1][...], ex[2][...], shift)
        return y * jnp.where(j < n_scaled, scale, 1.0).astype(F32)

    def epi(j):
        return [(j < n_rope, roped), (j >= n_rope, lambda acc, j, ex: acc)]

    return epi


def _mla_q_epilogue(scale):
    assert MM_SUB_COLS == MLA_QK_PAD

    def fn(acc, j, ex):
        rot = _rot(acc[:, LANES:], ex[0][...], ex[1][...], ex[2][...], MLA_ROPE // 2)
        return jnp.concatenate([acc[:, :LANES], rot], axis=1) * scale

    return lambda j: [(j >= 0, fn)]


def _krope_gate_epilogue(j):
    def fn(acc, j, ex):
        y = _rot(acc, ex[0][...], ex[1][...], ex[2][...], MLA_ROPE // 2)
        lane = lax.broadcasted_iota(jnp.int32, acc.shape, 1)
        return jnp.where(lane < LANES, y, 1.0 / (1.0 + jnp.exp(-acc)))

    return [(j >= 0, fn)]


def _mod_prologue(a, ex):
    return a * (1.0 + ex[-2][0]) + ex[-1][0]


def _rms_prologue(eps):
    def pro(a, ex):
        g = ex[-1][...]
        y = a * lax.rsqrt(jnp.mean(a * a, axis=-1, keepdims=True) + eps)
        return y * g

    return pro


def _relu2_epilogue(j):
    def fn(acc, j, ex):
        r = jnp.maximum(acc, 0.0)
        return r * r

    return [(j >= 0, fn)]


def _mmk_body(a_ref, b_ref, o_ref, acc_ref):
    kk = pl.program_id(2)

    @pl.when(kk == 0)
    def _():
        acc_ref[...] = jnp.zeros_like(acc_ref)

    acc_ref[...] += jnp.dot(a_ref[...], b_ref[...], preferred_element_type=F32)

    @pl.when(kk == pl.num_programs(2) - 1)
    def _():
        o_ref[...] = acc_ref[...].astype(o_ref.dtype)


def _mmk(a, b, *, tm, tn, tk, out_dtype, name):
    m, k = a.shape
    n = b.shape[1]
    return pl.pallas_call(
        _mmk_body,
        grid=(m // tm, n // tn, k // tk),
        in_specs=[pl.BlockSpec((tm, tk), lambda i, j, kk: (i, kk)),
                  pl.BlockSpec((tk, tn), lambda i, j, kk: (kk, j))],
        out_specs=pl.BlockSpec((tm, tn), lambda i, j, kk: (i, j)),
        out_shape=jax.ShapeDtypeStruct((m, n), out_dtype),
        scratch_shapes=[pltpu.VMEM((tm, tn), F32)],
        compiler_params=_params(("arbitrary", "arbitrary", "arbitrary")),
        name=name,
    )(a, b)


def _ln_body(x_ref, a_ref, gate_ref, g_ref, b_ref, *rest, alpha, with_mod):
    y = alpha * x_ref[...] + (1.0 + gate_ref[0]) * a_ref[...]
    mu = jnp.mean(y, axis=-1, keepdims=True)
    yc = y - mu
    var = jnp.mean(yc * yc, axis=-1, keepdims=True)
    out = yc * lax.rsqrt(var + 1e-5) * g_ref[...] + b_ref[...]
    if with_mod:
        sc_ref, sh_ref, o_ref, h_ref = rest
        o_ref[...] = out
        h_ref[...] = (out * (1.0 + sc_ref[0]) + sh_ref[0]).astype(h_ref.dtype)
    else:
        (o_ref,) = rest
        o_ref[...] = out


def _res_ln(x2, a2, gate, g, b, seq, alpha, mod=None):
    t, d = x2.shape
    ts = _tile(seq, (256, 128, 64, 8))
    per = seq // ts
    row = pl.BlockSpec((ts, d), lambda i: (i, 0))
    vec = pl.BlockSpec((1, 1, d), lambda i: (i // per, 0, 0))
    par = pl.BlockSpec((1, d), lambda i: (0, 0))
    ins = [x2, a2, gate[:, None, :], g.reshape(1, d), b.reshape(1, d)]
    specs = [row, row, vec, par, par]
    if mod is None:
        out_shape = jax.ShapeDtypeStruct((t, d), F32)
        out_specs = row
    else:
        ins += [mod[0][:, None, :], mod[1][:, None, :]]
        specs += [vec, vec]
        out_shape = (jax.ShapeDtypeStruct((t, d), F32), jax.ShapeDtypeStruct((t, d), CDT))
        out_specs = (row, row)
    return pl.pallas_call(
        functools.partial(_ln_body, alpha=alpha, with_mod=mod is not None),
        grid=(t // ts,),
        in_specs=specs,
        out_specs=out_specs,
        out_shape=out_shape,
        compiler_params=_params(("arbitrary",)),
        name="res_ln",
    )(*ins)


def _compress_body(x_ref, p0_ref, p1_ref, w1a_ref, w1b_ref, w2_ref, o_ref):
    x = x_ref[0, 0].astype(F32)
    nc = x.shape[0]
    xa = (x + p0_ref[...]).astype(CDT)
    xb = (x + p1_ref[...]).astype(CDT)
    a = jnp.dot(xa, w1a_ref[...], preferred_element_type=F32)
    bm = jnp.dot(xb, w1b_ref[...], preferred_element_type=F32)
    pre = a + pltpu.roll(bm, nc - 1, 0)
    hid = 0.5 * pre * (1.0 + jnp.tanh(math.sqrt(2.0 / math.pi) * (pre + 0.044715 * pre * pre * pre)))
    o_ref[0, 0] = jnp.dot(hid.astype(CDT), w2_ref[...], preferred_element_type=F32).astype(o_ref.dtype)


def _compress(xc, pos, w1, w2):
    bsz, g, nc, wid = xc.shape
    half = wid
    p0 = pos[:CMP_STRIDE].reshape(1, wid)
    p1 = pos[CMP_STRIDE:].reshape(1, wid)
    w1c = w1.astype(CDT)
    full = lambda shape: pl.BlockSpec(shape, lambda b, gi: (0,) * len(shape))
    return pl.pallas_call(
        _compress_body,
        grid=(bsz, g),
        in_specs=[pl.BlockSpec((1, 1, nc, wid), lambda b, gi: (b, gi, 0, 0)),
                  full((1, wid)), full((1, wid)),
                  full((half, w1.shape[1])), full((half, w1.shape[1])),
                  full(w2.shape)],
        out_specs=pl.BlockSpec((1, 1, nc, HEAD_DIM), lambda b, gi: (b, gi, 0, 0)),
        out_shape=jax.ShapeDtypeStruct((bsz, g, nc, HEAD_DIM), CDT),
        compiler_params=_params(("arbitrary", "arbitrary")),
        name="compress",
    )(xc, p0, p1, w1c[:half], w1c[half:], w2.astype(CDT))


def _cmp_body(q_ref, kc_ref, vc_ref, ov_ref, gate_ref, o_ref, bias_ref, *, tq, r_heads, nb, ncmp):
    gi = pl.program_id(1)
    i = pl.program_id(2)
    kc = kc_ref[0, 0]
    vc = vc_ref[0, 0]
    nc = kc.shape[0]
    t = i * tq + lax.broadcasted_iota(jnp.int32, (tq, nc), 0)
    cidx = lax.broadcasted_iota(jnp.int32, (tq, nc), 1)
    valid = (cidx * CMP_STRIDE + (CMP_LEN - 1) <= t) & (cidx < ncmp)
    validf = valid.astype(F32)
    real = cidx < ncmp
    gates = gate_ref[0]
    glane = lax.broadcasted_iota(jnp.int32, gates.shape, 1)
    psum = jnp.zeros((tq, nc), F32)
    for r in range(r_heads):
        q = q_ref[0, :, r * HEAD_DIM:(r + 1) * HEAD_DIM]
        s = lax.dot_general(q, kc, (((1,), (1,)), ((), ())), preferred_element_type=F32)
        s = jnp.where(valid, s, jnp.where(real, NEG, -jnp.inf))
        m = jnp.max(s, axis=-1, keepdims=True)
        e = jnp.exp2(s - m)
        p = e / jnp.sum(e, axis=-1, keepdims=True) * validf
        psum = psum + p
        o = jnp.dot(p.astype(CDT), vc, preferred_element_type=F32)
        col = ((gi * r_heads + r) * 3 + 0)
        gcol = jnp.sum(jnp.where(glane == col, gates, 0.0), axis=-1, keepdims=True)
        o_ref[0, :, r * HEAD_DIM:(r + 1) * HEAD_DIM] = (gcol * o).astype(o_ref.dtype)

    p_hi = psum.astype(CDT)
    p_lo = (psum - p_hi.astype(F32)).astype(CDT)
    imp = (jnp.dot(p_hi, ov_ref[...], preferred_element_type=F32)
           + jnp.dot(p_lo, ov_ref[...], preferred_element_type=F32))
    tt = i * tq + lax.broadcasted_iota(jnp.int32, (tq, LANES), 0)
    jj = lax.broadcasted_iota(jnp.int32, (tq, LANES), 1)
    cur = tt // SEL_LEN
    forced = (jj == 0) | ((jj <= cur) & (jj > cur - N_LOCAL_FORCED))
    imp = jnp.where(jj > cur, NEG, imp + jnp.where(forced, FORCED_BONUS, 0.0))
    v = imp.T[:nb]
    ridx = lax.broadcasted_iota(jnp.int32, (nb, tq), 0)
    sel = jnp.zeros((nb, tq), F32)
    for _ in range(min(N_SEL, nb)):
        mx = jnp.max(v, axis=0, keepdims=True)
        first = jnp.min(jnp.where(v == mx, ridx, nb), axis=0, keepdims=True)
        hit = ridx == first
        sel = jnp.where(hit, 1.0, sel)
        v = jnp.where(hit, -jnp.inf, v)
    bias_t = jnp.where(sel > 0.0, 0.0, NEG)
    if nb < LANES:
        bias_t = jnp.concatenate([bias_t, jnp.zeros((LANES - nb, tq), F32)], axis=0)
    bias_ref[0, 0] = bias_t.T.astype(bias_ref.dtype)


def _cmp_attn(qkv, kc, vc, overlap, gates, *, g_groups, r_heads, seq, ncmp):
    bsz = qkv.shape[0]
    nc = kc.shape[2]
    nb = seq // SEL_LEN
    tq = _tile(seq, (512, 256, 128))
    qw = r_heads * HEAD_DIM
    body = functools.partial(_cmp_body, tq=tq, r_heads=r_heads, nb=nb, ncmp=ncmp)
    return pl.pallas_call(
        body,
        grid=(bsz, g_groups, seq // tq),
        in_specs=[pl.BlockSpec((1, tq, qw), lambda b, g, i: (b, i, g)),
                  pl.BlockSpec((1, 1, nc, HEAD_DIM), lambda b, g, i: (b, g, 0, 0)),
                  pl.BlockSpec((1, 1, nc, HEAD_DIM), lambda b, g, i: (b, g, 0, 0)),
                  pl.BlockSpec((nc, LANES), lambda b, g, i: (0, 0)),
                  pl.BlockSpec((1, tq, LANES), lambda b, g, i: (b, i, 1))],
        out_specs=(pl.BlockSpec((1, tq, qw), lambda b, g, i: (b, i, g)),
                   pl.BlockSpec((1, 1, tq, LANES), lambda b, g, i: (b, g, i, 0))),
        out_shape=(jax.ShapeDtypeStruct((bsz, seq, g_groups * qw), F32),
                   jax.ShapeDtypeStruct((bsz, g_groups, seq, LANES), CDT)),
        compiler_params=_params(("arbitrary", "arbitrary", "arbitrary")),
        name="cmp_attn",
    )(qkv, kc, vc, overlap, gates)


ACC_ROWS = HEAD_DIM + 16


def _with_ones_row(vt):
    ones_row = jnp.where(lax.broadcasted_iota(jnp.int32, (ACC_ROWS - HEAD_DIM, vt.shape[1]), 0) == 0,
                         1.0, 0.0).astype(CDT)
    return jnp.concatenate([vt, ones_row], axis=0)


def _flash_init(m_ref, acc_ref):
    m_ref[...] = jnp.full(m_ref.shape, -jnp.inf, F32)
    acc_ref[...] = jnp.zeros(acc_ref.shape, F32)


def _nsa_body(q_ref, bias_ref, ks_ref, e_ref, vs_ref, kw_ref, vw_ref, gate_ref, prev_ref, o_ref,
              qc_ref, s0_ref, s1_ref, cm_ref, m_ref, acc_ref, *, tq, r_heads, n_win):
    s_refs = (s0_ref, s1_ref)
    gi = pl.program_id(1)
    i = pl.program_id(2)
    tk = tq
    cols = r_heads * tq
    for r in range(r_heads):
        qc_ref[r * tq:(r + 1) * tq, :] = jnp.concatenate(
            [q_ref[0, :, r * HEAD_DIM:(r + 1) * HEAD_DIM], bias_ref[0, 0]], axis=1)
    _flash_init(m_ref, acc_ref)

    def local_iotas():
        key = lax.broadcasted_iota(jnp.int32, (tk, cols), 0)
        tok = lax.broadcasted_iota(jnp.int32, (tk, cols), 1) & (tq - 1)
        return key, tok

    def put(s, buf):
        s_refs[buf][...] = s
        cm_ref[buf] = jnp.max(s, axis=0, keepdims=True)

    def update(branch, vt, buf):
        m_prev = m_ref[branch]
        m_new = jnp.maximum(m_prev, cm_ref[buf])
        p = jnp.exp2(s_refs[buf][...] - m_new).astype(CDT)
        acc_ref[branch] = (jnp.exp2(m_prev - m_new) * acc_ref[branch]
                           + jnp.dot(_with_ones_row(vt), p, preferred_element_type=F32))
        m_ref[branch] = m_new

    def sel_scores(j, buf, masked):
        off = pl.multiple_of(j * tk, tk)
        kmat = jnp.concatenate([ks_ref[0, pl.ds(off, tk), :], e_ref[pl.ds(off, tk), :]], axis=1)
        s = lax.dot_general(kmat, qc_ref[...], (((1,), (1,)), ((), ())), preferred_element_type=F32)
        if masked:
            key, tok = local_iotas()
            s = jnp.where(key <= tok, s, NEG)
        put(s, buf)

    def sel_update(j, buf):
        update(0, vs_ref[0, 0, j], buf)

    def win_tile(dd):
        back = n_win - 1 - dd
        return back, jnp.maximum(i - back, 0)

    def win_scores(dd, buf):
        back, j = win_tile(dd)
        off = pl.multiple_of(j * tk, tk)
        s = lax.dot_general(kw_ref[0, pl.ds(off, tk), :], qc_ref[:, :HEAD_DIM], (((1,), (1,)), ((), ())),
                            preferred_element_type=F32)
        key, tok = local_iotas()
        if back * tk - (tk - 1) < 0:
            s = jnp.where(key - back * tk <= tok, s, NEG)
        if back > 0 or tq - 1 + back * tk >= WINDOW:
            reach = WINDOW - back * tk
            if back > 0:
                reach = reach - jnp.where(i >= back, 0, 2 * WINDOW + tq)
            s = jnp.where(tok < key + reach, s, NEG)
        put(s, buf)

    def win_update(dd, buf):
        update(1, vw_ref[0, 0, win_tile(dd)[1]], buf)

    def finish(fb):
        wbuf = lambda dd: fb if dd % 2 else 1 - fb
        win_scores(0, wbuf(0))
        sel_update(i, fb)
        for dd in range(1, n_win):
            win_scores(dd, wbuf(dd))
            win_update(dd - 1, wbuf(dd - 1))
        win_update(n_win - 1, wbuf(n_win - 1))

    @pl.when(i == 0)
    def _():
        sel_scores(0, 0, True)

    @pl.when(i > 0)
    def _():
        sel_scores(0, 0, False)

    n_pairs = jnp.maximum(i - 1, 0) // 2

    def body(jp, carry):
        j = 2 * jp
        sel_scores(j + 1, 1, False)
        sel_update(j, 0)
        sel_scores(j + 2, 0, False)
        sel_update(j + 1, 1)
        return carry

    lax.fori_loop(0, n_pairs, body, 0)
    j0 = 2 * n_pairs
    left = i - j0

    @pl.when(left == 1)
    def _():
        sel_scores(i, 1, True)
        sel_update(j0, 0)
        finish(1)

    @pl.when(left == 2)
    def _():
        sel_scores(j0 + 1, 1, False)
        sel_update(j0, 0)
        sel_scores(i, 0, True)
        sel_update(j0 + 1, 1)
        finish(0)

    @pl.when(left == 0)
    def _():
        finish(0)

    gates_t = gate_ref[0].T
    rid = lax.broadcasted_iota(jnp.int32, gates_t.shape, 0)
    for r in range(r_heads):
        cs = slice(r * tq, (r + 1) * tq)
        o = None
        for branch in (0, 1):
            acc = acc_ref[branch]
            col = (gi * r_heads + r) * 3 + 1 + branch
            grow = jnp.sum(jnp.where(rid == col, gates_t, 0.0), axis=0, keepdims=True)
            ob = acc[:HEAD_DIM, cs] * (grow / acc[HEAD_DIM:HEAD_DIM + 1, cs])
            o = ob if o is None else o + ob
        sl = slice(r * HEAD_DIM, (r + 1) * HEAD_DIM)
        o_ref[0, :, sl] = (prev_ref[0, :, sl] + o.T).astype(o_ref.dtype)


def _nsa_attn(qkv, ks_blk0, kw_blk0, vt, gates, prev, bias, e_mat, *, g_groups, r_heads, seq):
    bsz = qkv.shape[0]
    qw = r_heads * HEAD_DIM
    tq = tk = vt.shape[-1]
    cols = r_heads * tq
    kspec = lambda blk0: pl.BlockSpec((1, seq, HEAD_DIM), lambda b, g, i: (b, 0, blk0 + g))
    vspec = lambda h0: pl.BlockSpec((1, 1) + vt.shape[2:], lambda b, g, i: (b, h0 + g, 0, 0, 0))
    qspec = pl.BlockSpec((1, tq, qw), lambda b, g, i: (b, i, g))
    body = functools.partial(_nsa_body, tq=tq, r_heads=r_heads, n_win=(WINDOW + tk - 1) // tk + 1)
    return pl.pallas_call(
        body,
        grid=(bsz, g_groups, seq // tq),
        in_specs=[qspec, pl.BlockSpec((1, 1, tq, LANES), lambda b, g, i: (b, g, i, 0)),
                  kspec(ks_blk0), pl.BlockSpec((seq, LANES), lambda b, g, i: (0, 0)), vspec(0),
                  kspec(kw_blk0), vspec(g_groups),
                  pl.BlockSpec((1, tq, LANES), lambda b, g, i: (b, i, 1)), qspec],
        out_specs=qspec,
        out_shape=jax.ShapeDtypeStruct((bsz, seq, g_groups * qw), CDT),
        scratch_shapes=[pltpu.VMEM((cols, 2 * HEAD_DIM), CDT),
                        pltpu.VMEM((tk, cols), F32), pltpu.VMEM((tk, cols), F32),
                        pltpu.VMEM((2, 1, cols), F32),
                        pltpu.VMEM((2, 1, cols), F32),
                        pltpu.VMEM((2, ACC_ROWS, cols), F32)],
        compiler_params=_params(("arbitrary",) * 3),
        name="nsa_sel_win",
    )(qkv, bias, qkv, e_mat, vt, qkv, vt, gates, prev)


def _mla_body(q_ref, kn_ref, kr_ref, vt_ref, o_ref, *rest, tk, hg):
    tq = 2 * tk
    s_refs = rest[:hg]
    s0_alt = rest[hg]
    cm_ref, m_ref, acc_ref = rest[hg + 1:]
    i = pl.program_id(2)
    _flash_init(m_ref, acc_ref)
    n_full = 2 * i

    def scores(c, j, c0, mask_mode, alt=False):
        off = pl.multiple_of(j * tk, tk)
        kmat = jnp.concatenate([kn_ref[0, pl.ds(off, tk), c * MLA_NOPE:(c + 1) * MLA_NOPE],
                                kr_ref[0, pl.ds(off, tk), :]], axis=1)
        s = lax.dot_general(kmat, q_ref[0, c0:, c * MLA_QK_PAD:(c + 1) * MLA_QK_PAD],
                            (((1,), (1,)), ((), ())), preferred_element_type=F32)
        if mask_mode is not None:
            key = lax.broadcasted_iota(jnp.int32, s.shape, 0)
            tok = lax.broadcasted_iota(jnp.int32, s.shape, 1)
            if mask_mode == "dyn":
                key = key - jnp.where(j >= n_full, 0, tq)
            s = jnp.where(key <= tok, s, NEG)
        ref, ci = (s0_alt, hg) if alt else (s_refs[c], c)
        ref[:, c0:] = s
        cm_ref[ci, :, c0:] = jnp.max(s, axis=0, keepdims=True)

    def update(c, j, c0, alt=False):
        ref, ci = (s0_alt, hg) if alt else (s_refs[c], c)
        m_prev = m_ref[c, :, c0:]
        m_new = jnp.maximum(m_prev, cm_ref[ci, :, c0:])
        p = jnp.exp2(ref[:, c0:] - m_new).astype(CDT)
        acc_ref[c, :, c0:] = (jnp.exp2(m_prev - m_new) * acc_ref[c, :, c0:]
                              + jnp.dot(_with_ones_row(vt_ref[0, c, j]), p, preferred_element_type=F32))
        m_ref[c, :, c0:] = m_new

    def tile(j, c0, mask_mode, head0_alt, prefetch):
        for c in range(1, hg):
            scores(c, j, c0, mask_mode)
            update(c - 1, j, c0, alt=head0_alt and c == 1)
        if prefetch is not None:
            prefetch()
        update(hg - 1, j, c0, alt=head0_alt and hg == 1)

    scores(0, 0, 0, "dyn")

    def pair(jp, carry):
        ja = 2 * jp
        tile(ja, 0, None, False, lambda: scores(0, ja + 1, 0, None, alt=True))
        tile(ja + 1, 0, None, True, lambda: scores(0, ja + 2, 0, "dyn"))
        return carry

    lax.fori_loop(0, i, pair, 0)
    tile(n_full, 0, "tri", False, lambda: scores(0, n_full + 1, tk, "tri", alt=True))
    tile(n_full + 1, tk, "tri", True, None)
    for c in range(hg):
        acc = acc_ref[c]
        o = acc[:HEAD_DIM] * (1.0 / acc[HEAD_DIM:HEAD_DIM + 1])
        o_ref[0, :, c * MLA_V:(c + 1) * MLA_V] = o.T.astype(o_ref.dtype)


def _mla_attn(q, kv, kr, vt, *, heads, seq):
    bsz = q.shape[0]
    tk = vt.shape[-1]
    tq = 2 * tk
    assert seq % tq == 0
    hg = 4 if heads % 4 == 0 else 1
    body = functools.partial(_mla_body, tk=tk, hg=hg)
    return pl.pallas_call(
        body,
        grid=(bsz, heads // hg, seq // tq),
        in_specs=[pl.BlockSpec((1, tq, hg * MLA_QK_PAD), lambda b, h, i: (b, i, h)),
                  pl.BlockSpec((1, seq, hg * MLA_NOPE), lambda b, h, i: (b, 0, h)),
                  pl.BlockSpec((1, seq, LANES), lambda b, h, i: (b, 0, 0)),
                  pl.BlockSpec((1, hg) + vt.shape[2:], lambda b, h, i: (b, h, 0, 0, 0))],
        out_specs=pl.BlockSpec((1, tq, hg * MLA_V), lambda b, h, i: (b, i, h)),
        out_shape=jax.ShapeDtypeStruct((bsz, seq, heads * MLA_V), CDT),
        scratch_shapes=[pltpu.VMEM((tk, tq), F32)] * (hg + 1) + [
            pltpu.VMEM((hg + 1, 1, tq), F32),
            pltpu.VMEM((hg, 1, tq), F32), pltpu.VMEM((hg, ACC_ROWS, tq), F32)],
        compiler_params=_params(("arbitrary",) * 3),
        name="mla_attn",
    )(q, kv, kr, vt)


def _cos_sin(positions, rot_dim):
    inv = jnp.power(ROPE_THETA, -jnp.arange(0, rot_dim, 2, dtype=F32) / rot_dim)
    ang = positions.astype(F32).reshape(-1)[:, None] * inv
    return jnp.cos(ang), jnp.sin(ang)


def _rope_tables(cos, sin, period):
    t, half = cos.shape
    rest = period - 2 * half
    ct = jnp.concatenate([cos, cos, jnp.ones((t, rest), F32)], axis=1)
    sa = jnp.concatenate([-sin, jnp.zeros((t, half + rest), F32)], axis=1)
    sb = jnp.concatenate([jnp.zeros((t, half), F32), sin, jnp.zeros((t, rest), F32)], axis=1)
    return ct, sa, sb


def _table_extras(tables, tm):
    return [(tb, pl.BlockSpec((tm, tb.shape[1]), lambda i, j: (i, 0))) for tb in tables]


def _layer(x2, mod, positions, w_in, pos_k, pos_v, k1, k2, v1, v2, q_norm, kv_norm, w_uq, w_ukv,
           w_out, ln1_g, ln1_b, w_ff1, w_ff2, ln2_g, ln2_b, *, bsz, seq, alpha):
    t, d = x2.shape
    n_heads = d // HEAD_DIM
    nsa_h = n_heads // 2
    mla_h = n_heads - nsa_h
    q_rank = q_norm.shape[0]
    kv_rank = kv_norm.shape[0]
    d_in = w_in.shape[1]
    g_groups = (d_in - nsa_h * HEAD_DIM - nsa_h * 3 - q_rank - kv_rank - MLA_ROPE) // (6 * HEAD_DIM)
    r_heads = nsa_h // g_groups
    gw = g_groups * HEAD_DIM
    qw = nsa_h * HEAD_DIM
    sh_a, sc_a, g_a, sh_m, sc_m, g_m = jnp.split(mod, N_ADA, axis=-1)

    o_q, o_kv = 0, qw
    o_gate = o_kv + 6 * gw
    o_cq = o_gate + nsa_h * 3
    o_ckv = o_cq + q_rank
    o_kr = o_ckv + kv_rank
    kvcol = lambda idx: w_in[:, o_kv + idx * gw:o_kv + (idx + 1) * gw]
    w1c = jnp.concatenate([w_in[:, o_q:o_q + qw], kvcol(0), kvcol(2), kvcol(4)], axis=1).astype(CDT)
    wvc = jnp.concatenate([kvcol(3), kvcol(5)], axis=1).astype(CDT)
    w2c = jnp.concatenate([w_in[:, o_cq:o_cq + q_rank], w_in[:, o_ckv:o_ckv + kv_rank], kvcol(1)],
                          axis=1).astype(CDT)
    zpad = lambda n: jnp.zeros((d, n), w_in.dtype)
    w3c = jnp.concatenate([w_in[:, o_kr:o_kr + MLA_ROPE], zpad(LANES - MLA_ROPE),
                           w_in[:, o_gate:o_gate + nsa_h * 3], zpad(LANES - nsa_h * 3)],
                          axis=1).astype(CDT)

    tm = _tile(t, (1024, 512, 256, 128))
    assert MLA_ROPE % PARTIAL_ROT == 0
    cos_m, sin_m = _cos_sin(positions, MLA_ROPE)
    stride = MLA_ROPE // PARTIAL_ROT
    nsa_tabs = _rope_tables(cos_m[:, ::stride], sin_m[:, ::stride], LANES)
    n1 = w1c.shape[1]
    tn1 = _tile(gw, (512, 256, 128))
    n_rope_cols = qw + 3 * gw
    scale = HEAD_DIM ** -0.5 * LOG2E
    tm1 = _tile(seq, (512, 256, 128))
    per1 = seq // tm1
    seq_vec = lambda v: (v[:, None, :], pl.BlockSpec((1, 1, d), lambda i, j: (i // per1, 0, 0)))
    p1, h = _mm([x2], w1c, tm=tm1, tn=tn1, out_dtype=CDT, name="in_proj_rot",
                prologue=_mod_prologue, emit_a=True,
                epilogue=_rope_epilogue(PARTIAL_ROT // 2, n_rope_cols // tn1, qw // tn1, scale),
                extras=_table_extras(nsa_tabs, tm1) + [seq_vec(sc_a), seq_vec(sh_a)])
    tk = _tile(seq, (512, 256, 128))
    vt_nsa = _mm([h], wvc, tm=tm, tn=_tile(2 * gw, (512, 256, 128)), out_dtype=CDT, name="in_proj_vt",
                 head_tiles=tk, seq=seq)
    n2 = w2c.shape[1]
    tn2 = _tile(n2, (512, 256, 128))
    p2 = _mm([h], w2c, tm=tm, tn=tn2, out_dtype=F32, name="in_proj_lat")
    kr_tabs = _rope_tables(cos_m, sin_m, LANES)
    p3 = _mm([h], w3c, tm=tm, tn=2 * LANES, out_dtype=F32, name="in_proj_kr_gate",
             epilogue=_krope_gate_epilogue, extras=_table_extras(kr_tabs, tm))

    qkv = p1.reshape(bsz, seq, n1)
    gates = p3.reshape(bsz, seq, 2 * LANES)

    nchunk = seq // CMP_STRIDE
    ncmp = (seq - CMP_LEN) // CMP_STRIDE + 1
    def chunked(a):
        return (a.reshape(bsz, nchunk, CMP_STRIDE, g_groups, HEAD_DIM).transpose(0, 3, 1, 2, 4)
                .reshape(bsz, g_groups, nchunk, CMP_STRIDE * HEAD_DIM))
    kc = _compress(chunked(qkv[:, :, qw:qw + gw]), pos_k, k1, k2)
    v_cmp = p2.reshape(bsz, seq, n2)[:, :, q_rank + kv_rank:]
    vc = _compress(chunked(v_cmp), pos_v, v1, v2)

    nb = seq // SEL_LEN
    c_start = np.arange(nchunk) * CMP_STRIDE
    b_start = np.arange(LANES) * SEL_LEN
    overlap = ((c_start[:, None] < b_start[None, :] + SEL_LEN) &
               (c_start[:, None] + CMP_LEN > b_start[None, :]) &
               (np.arange(nchunk)[:, None] < ncmp) & (np.arange(LANES)[None, :] < nb))
    overlap = jnp.asarray(overlap.astype(np.float32)).astype(CDT)
    o_cmp, bias = _cmp_attn(qkv, kc, vc, overlap, gates, g_groups=g_groups, r_heads=r_heads,
                            seq=seq, ncmp=ncmp)

    e_mat = jnp.asarray((np.arange(seq)[:, None] // SEL_LEN == np.arange(LANES)[None, :])
                        .astype(np.float32)).astype(CDT)
    blk = lambda col: col // HEAD_DIM
    o_nsa = _nsa_attn(qkv, blk(qw + gw), blk(qw + 2 * gw), vt_nsa, gates, o_cmp, bias, e_mat,
                      g_groups=g_groups, r_heads=r_heads, seq=seq)

    qk_dim = MLA_NOPE + MLA_ROPE
    wq = w_uq.reshape(q_rank, mla_h, qk_dim)
    wq = jnp.concatenate([wq, jnp.zeros((q_rank, mla_h, MLA_QK_PAD - qk_dim), wq.dtype)], axis=-1)
    wq = wq.reshape(q_rank, mla_h * MLA_QK_PAD).astype(CDT)
    wkv = w_ukv.reshape(kv_rank, mla_h, MLA_NOPE + MLA_V)
    wk = wkv[:, :, :MLA_NOPE].reshape(kv_rank, mla_h * MLA_NOPE).astype(CDT)
    wv = wkv[:, :, MLA_NOPE:].reshape(kv_rank, mla_h * MLA_V).astype(CDT)
    tnq = _tile(mla_h * MLA_QK_PAD, (512, 256))
    q_mla = _mm([p2], wq, tm=tm, tn=tnq, out_dtype=CDT, name="mla_uq",
                a_specs=[pl.BlockSpec((tm, q_rank), lambda i, j: (i, 0))],
                prologue=_rms_prologue(1e-6),
                epilogue=_mla_q_epilogue(qk_dim ** -0.5 * LOG2E),
                extras=_table_extras(kr_tabs, tm)
                + [(q_norm.reshape(1, q_rank), pl.BlockSpec((1, q_rank), lambda i, j: (0, 0)))])
    tnkv = _tile(wk.shape[1], (512, 256, 128))
    ckv_args = dict(tm=tm, tn=tnkv, out_dtype=CDT, prologue=_rms_prologue(1e-6),
                    a_specs=[pl.BlockSpec((tm, kv_rank), lambda i, j: (i, q_rank // kv_rank))],
                    extras=[(kv_norm.reshape(1, kv_rank), pl.BlockSpec((1, kv_rank), lambda i, j: (0, 0)))])
    k_mla = _mm([p2], wk, name="mla_uk", **ckv_args)
    vt_mla = _mm([p2], wv, name="mla_uv", head_tiles=tk, seq=seq, **ckv_args)
    o_mla = _mla_attn(q_mla.reshape(bsz, seq, -1), k_mla.reshape(bsz, seq, -1),
                      gates[:, :, :LANES].astype(CDT), vt_mla, heads=mla_h, seq=seq)

    tno = _tile(d, (1024, 512, 256, 128))
    a = _mm([o_nsa.reshape(t, qw), o_mla.reshape(t, mla_h * MLA_V)], w_out.astype(CDT), tm=tm, tn=tno,
            out_dtype=F32, name="out_proj")
    x1, h2 = _res_ln(x2, a, g_a, ln1_g, ln1_b, seq, alpha, mod=(sc_m, sh_m))
    d_ff = w_ff1.shape[1]
    f1 = _mm([h2], w_ff1.astype(CDT), tm=tm, tn=_tile(d_ff, (1024, 512, 256, 128)), out_dtype=CDT,
             name="ff1", epilogue=_relu2_epilogue)
    f2 = _mmk(f1, w_ff2.astype(CDT), tm=tm, tn=tno, tk=_tile(d_ff, (4096, 2048, 1024, 512)),
              out_dtype=F32, name="ff2")
    return _res_ln(x1, f2, g_m, ln2_g, ln2_b, seq, alpha)


def kernel(x, c, positions, w_ada, b_ada, w_in, nsa_pos_k, nsa_pos_v, nsa_cmp_k1, nsa_cmp_k2, nsa_cmp_v1, nsa_cmp_v2, mla_q_norm, mla_kv_norm, mla_w_uq, mla_w_ukv, w_out, ln1_g, ln1_b, w_ff1, w_ff2, ln2_g, ln2_b):
    bsz, seq, d = x.shape
    depth = w_ada.shape[0]
    alpha = (2 * depth) ** 0.25
    x2 = x.reshape(bsz * seq, d)
    for layer in range(depth):
        mod = _ada(c, w_ada[layer], b_ada[layer])
        x2 = _layer(x2, mod, positions, w_in[layer], nsa_pos_k[layer], nsa_pos_v[layer],
                    nsa_cmp_k1[layer], nsa_cmp_k2[layer], nsa_cmp_v1[layer], nsa_cmp_v2[layer],
                    mla_q_norm[layer], mla_kv_norm[layer], mla_w_uq[layer], mla_w_ukv[layer],
                    w_out[layer], ln1_g[layer], ln1_b[layer], w_ff1[layer], w_ff2[layer],
                    ln2_g[layer], ln2_b[layer], bsz=bsz, seq=seq, alpha=alpha)
    return x2.reshape(bsz, seq, d)
```

```python
import functools
import math

import numpy as np
import jax
import jax.numpy as jnp
from jax import lax
from jax.experimental import pallas as pl
from jax.experimental.pallas import tpu as pltpu

HEAD_DIM = 128
CMP_LEN = 32
CMP_STRIDE = 16
SEL_LEN = 64
N_SEL = 16
N_LOCAL_FORCED = 2
FORCED_BONUS = 1e4
WINDOW = 512
MLA_NOPE = 128
MLA_ROPE = 64
MLA_V = 128
ROPE_THETA = 500000.0
PARTIAL_ROT = HEAD_DIM // 4
NEG = -1e30
LOG2E = math.log2(math.e)
N_ADA = 6
LANES = 128
MLA_QK_PAD = 2 * LANES
VMEM_LIMIT = 56 * 1024 * 1024
MM_SUB_COLS = 256

F32 = jnp.float32
CDT = jnp.bfloat16


def _params(sem):
    return pltpu.CompilerParams(dimension_semantics=sem, vmem_limit_bytes=VMEM_LIMIT)


def _tile(n, cands):
    for c in cands:
        if n % c == 0:
            return c
    return n


def _ada_body(c_ref, w_ref, b_ref, o_ref):
    c = c_ref[...]
    cond = c * (1.0 / (1.0 + jnp.exp(-c)))
    o_ref[...] = jnp.dot(cond.astype(CDT), w_ref[...].astype(CDT),
                         preferred_element_type=F32) + b_ref[...]


def _ada(c, w, b):
    bsz, d = c.shape
    n = w.shape[1]
    tn = _tile(n, (512, 256, 128))
    return pl.pallas_call(
        _ada_body,
        grid=(n // tn,),
        in_specs=[pl.BlockSpec((bsz, d), lambda j: (0, 0)),
                  pl.BlockSpec((d, tn), lambda j: (0, j)),
                  pl.BlockSpec((1, tn), lambda j: (0, j))],
        out_specs=pl.BlockSpec((bsz, tn), lambda j: (0, j)),
        out_shape=jax.ShapeDtypeStruct((bsz, n), F32),
        compiler_params=_params(("arbitrary",)),
        name="ada",
    )(c, w, b.reshape(1, n))


def _modcast_body(x_ref, sc_ref, sh_ref, o_ref):
    o_ref[...] = (x_ref[...] * (1.0 + sc_ref[0]) + sh_ref[0]).astype(o_ref.dtype)


def _modcast(x2, sc, sh, seq):
    t, d = x2.shape
    ts = _tile(seq, (256, 128, 64, 8))
    per = seq // ts
    vec = pl.BlockSpec((1, 1, d), lambda i: (i // per, 0, 0))
    return pl.pallas_call(
        _modcast_body,
        grid=(t // ts,),
        in_specs=[pl.BlockSpec((ts, d), lambda i: (i, 0)), vec, vec],
        out_specs=pl.BlockSpec((ts, d), lambda i: (i, 0)),
        out_shape=jax.ShapeDtypeStruct((t, d), CDT),
        compiler_params=_params(("arbitrary",)),
        name="modcast",
    )(x2, sc[:, None, :], sh[:, None, :])


def _mm_body(*refs, na, nex, prologue, epilogue, tn, head_tiles):
    a_refs = refs[:na]
    b_ref = refs[na]
    ex = refs[na + 1:na + 1 + nex]
    o_ref = refs[na + 1 + nex]
    j = pl.program_id(1)
    if prologue is not None:
        a_sc = refs[na + 2 + nex]

        @pl.when(j == 0)
        def _():
            a_sc[...] = prologue(a_refs[0][...], ex).astype(a_sc.dtype)

        a_refs = (a_sc,)
    sub = MM_SUB_COLS if tn % MM_SUB_COLS == 0 else tn

    def store(s, y):
        if head_tiles is None:
            o_ref[:, s * sub:(s + 1) * sub] = y.astype(o_ref.dtype)
        else:
            for hh in range(sub // HEAD_DIM):
                for tt in range(y.shape[0] // head_tiles):
                    blk = y[tt * head_tiles:(tt + 1) * head_tiles, hh * HEAD_DIM:(hh + 1) * HEAD_DIM]
                    o_ref[0, s * (sub // HEAD_DIM) + hh, tt] = blk.T.astype(o_ref.dtype)

    def run(fn):
        for s in range(tn // sub):
            cols = slice(s * sub, (s + 1) * sub)
            acc = None
            off = 0
            for r in a_refs:
                kr = r.shape[1]
                part = jnp.dot(r[...], b_ref[off:off + kr, cols], preferred_element_type=F32)
                acc = part if acc is None else acc + part
                off += kr
            store(s, fn(acc, j, ex))

    if epilogue is None:
        run(lambda acc, j, ex: acc)
    else:
        for cond, fn in epilogue(j):
            pl.when(cond)(functools.partial(run, fn))


def _mm(a_list, b, *, tm, tn, out_dtype, name, a_specs=None, prologue=None, epilogue=None,
        extras=(), head_tiles=None, seq=None):
    m = a_list[0].shape[0]
    k, n = b.shape
    if a_specs is None:
        a_specs = [pl.BlockSpec((tm, a.shape[1]), lambda i, j: (i, 0)) for a in a_list]
    ex_arrays = [e[0] for e in extras]
    ex_specs = [e[1] for e in extras]
    scratch = [pltpu.VMEM((tm, k), CDT)] if prologue is not None else []
    body = functools.partial(_mm_body, na=len(a_list), nex=len(extras), prologue=prologue,
                             epilogue=epilogue, tn=tn, head_tiles=head_tiles)
    if head_tiles is None:
        out_specs = pl.BlockSpec((tm, tn), lambda i, j: (i, j))
        out_shape = jax.ShapeDtypeStruct((m, n), out_dtype)
    else:
        per = seq // tm
        out_specs = pl.BlockSpec((1, tn // HEAD_DIM, tm // head_tiles, HEAD_DIM, head_tiles),
                                 lambda i, j: (i // per, j, i % per, 0, 0))
        out_shape = jax.ShapeDtypeStruct((m // seq, n // HEAD_DIM, seq // head_tiles, HEAD_DIM, head_tiles),
                                         out_dtype)
    return pl.pallas_call(
        body,
        grid=(m // tm, n // tn),
        in_specs=a_specs + [pl.BlockSpec((k, tn), lambda i, j: (0, j))] + ex_specs,
        out_specs=out_specs,
        out_shape=out_shape,
        scratch_shapes=scratch,
        compiler_params=_params(("arbitrary", "arbitrary")),
        name=name,
    )(*a_list, b, *ex_arrays)


def _rot(acc, cos, sa, sb, shift):
    tn = acc.shape[1]
    reps = tn // cos.shape[1]

    def rep(t):
        return t if reps == 1 else jnp.concatenate([t] * reps, axis=1)

    return (acc * rep(cos) + pltpu.roll(acc, tn - shift, 1) * rep(sa)
            + pltpu.roll(acc, shift, 1) * rep(sb))


def _rope_epilogue(shift, lo, hi, n_scaled, scale):
    def roped(acc, j, ex):
        y = _rot(acc, ex[0][...], ex[1][...], ex[2][...], shift)
        return y * jnp.where(j < n_scaled, scale, 1.0).astype(F32)

    def epi(j):
        inside = (j >= lo) & (j < hi)
        return [(inside, roped), (jnp.logical_not(inside), lambda acc, j, ex: acc)]

    return epi


def _mla_q_epilogue(scale):
    assert MM_SUB_COLS == MLA_QK_PAD

    def fn(acc, j, ex):
        rot = _rot(acc[:, LANES:], ex[0][...], ex[1][...], ex[2][...], MLA_ROPE // 2)
        return jnp.concatenate([acc[:, :LANES], rot], axis=1) * scale

    return lambda j: [(j >= 0, fn)]


def _krope_gate_epilogue(j):
    def fn(acc, j, ex):
        y = _rot(acc, ex[0][...], ex[1][...], ex[2][...], MLA_ROPE // 2)
        lane = lax.broadcasted_iota(jnp.int32, acc.shape, 1)
        return jnp.where(lane < LANES, y, 1.0 / (1.0 + jnp.exp(-acc)))

    return [(j >= 0, fn)]


def _rms_prologue(eps):
    def pro(a, ex):
        g = ex[-1][...]
        y = a * lax.rsqrt(jnp.mean(a * a, axis=-1, keepdims=True) + eps)
        return y * g

    return pro


def _relu2_epilogue(j):
    def fn(acc, j, ex):
        r = jnp.maximum(acc, 0.0)
        return r * r

    return [(j >= 0, fn)]


def _mmk_body(a_ref, b_ref, o_ref, acc_ref):
    kk = pl.program_id(2)

    @pl.when(kk == 0)
    def _():
        acc_ref[...] = jnp.zeros_like(acc_ref)

    acc_ref[...] += jnp.dot(a_ref[...], b_ref[...], preferred_element_type=F32)

    @pl.when(kk == pl.num_programs(2) - 1)
    def _():
        o_ref[...] = acc_ref[...].astype(o_ref.dtype)


def _mmk(a, b, *, tm, tn, tk, out_dtype, name):
    m, k = a.shape
    n = b.shape[1]
    return pl.pallas_call(
        _mmk_body,
        grid=(m // tm, n // tn, k // tk),
        in_specs=[pl.BlockSpec((tm, tk), lambda i, j, kk: (i, kk)),
                  pl.BlockSpec((tk, tn), lambda i, j, kk: (kk, j))],
        out_specs=pl.BlockSpec((tm, tn), lambda i, j, kk: (i, j)),
        out_shape=jax.ShapeDtypeStruct((m, n), out_dtype),
        scratch_shapes=[pltpu.VMEM((tm, tn), F32)],
        compiler_params=_params(("arbitrary", "arbitrary", "arbitrary")),
        name=name,
    )(a, b)


def _ln_body(x_ref, a_ref, gate_ref, g_ref, b_ref, *rest, alpha, with_mod):
    y = alpha * x_ref[...] + (1.0 + gate_ref[0]) * a_ref[...]
    mu = jnp.mean(y, axis=-1, keepdims=True)
    yc = y - mu
    var = jnp.mean(yc * yc, axis=-1, keepdims=True)
    out = yc * lax.rsqrt(var + 1e-5) * g_ref[...] + b_ref[...]
    if with_mod:
        sc_ref, sh_ref, o_ref, h_ref = rest
        o_ref[...] = out
        h_ref[...] = (out * (1.0 + sc_ref[0]) + sh_ref[0]).astype(h_ref.dtype)
    else:
        (o_ref,) = rest
        o_ref[...] = out


def _res_ln(x2, a2, gate, g, b, seq, alpha, mod=None):
    t, d = x2.shape
    ts = _tile(seq, (256, 128, 64, 8))
    per = seq // ts
    row = pl.BlockSpec((ts, d), lambda i: (i, 0))
    vec = pl.BlockSpec((1, 1, d), lambda i: (i // per, 0, 0))
    par = pl.BlockSpec((1, d), lambda i: (0, 0))
    ins = [x2, a2, gate[:, None, :], g.reshape(1, d), b.reshape(1, d)]
    specs = [row, row, vec, par, par]
    if mod is None:
        out_shape = jax.ShapeDtypeStruct((t, d), F32)
        out_specs = row
    else:
        ins += [mod[0][:, None, :], mod[1][:, None, :]]
        specs += [vec, vec]
        out_shape = (jax.ShapeDtypeStruct((t, d), F32), jax.ShapeDtypeStruct((t, d), CDT))
        out_specs = (row, row)
    return pl.pallas_call(
        functools.partial(_ln_body, alpha=alpha, with_mod=mod is not None),
        grid=(t // ts,),
        in_specs=specs,
        out_specs=out_specs,
        out_shape=out_shape,
        compiler_params=_params(("arbitrary",)),
        name="res_ln",
    )(*ins)


def _compress_body(x_ref, pos_ref, w1_ref, w2_ref, o_ref):
    nc = o_ref.shape[2]
    acc_a = None
    acc_b = None
    for l in range(CMP_STRIDE):
        x = x_ref[pl.ds(l, nc, stride=CMP_STRIDE), :]
        xa = (x + pos_ref[l:l + 1, :]).astype(CDT)
        xb = (x + pos_ref[CMP_STRIDE + l:CMP_STRIDE + l + 1, :]).astype(CDT)
        wa = w1_ref[l * HEAD_DIM:(l + 1) * HEAD_DIM, :]
        wb = w1_ref[(CMP_STRIDE + l) * HEAD_DIM:(CMP_STRIDE + l + 1) * HEAD_DIM, :]
        pa = jnp.dot(xa, wa, preferred_element_type=F32)
        pb = jnp.dot(xb, wb, preferred_element_type=F32)
        acc_a = pa if acc_a is None else acc_a + pa
        acc_b = pb if acc_b is None else acc_b + pb
    pre = acc_a + pltpu.roll(acc_b, nc - 1, 0)
    hid = 0.5 * pre * (1.0 + jnp.tanh(math.sqrt(2.0 / math.pi) * (pre + 0.044715 * pre * pre * pre)))
    o_ref[0, 0] = jnp.dot(hid.astype(CDT), w2_ref[...], preferred_element_type=F32).astype(o_ref.dtype)


def _compress(x2, col_blk0, pos, w1, w2, *, bsz, seq, g_groups):
    nc = seq // CMP_STRIDE
    full = lambda shape: pl.BlockSpec(shape, lambda b, gi: (0,) * len(shape))
    return pl.pallas_call(
        _compress_body,
        grid=(bsz, g_groups),
        in_specs=[pl.BlockSpec((seq, HEAD_DIM), lambda b, gi: (b, col_blk0 + gi)),
                  full(pos.shape), full(w1.shape), full(w2.shape)],
        out_specs=pl.BlockSpec((1, 1, nc, HEAD_DIM), lambda b, gi: (b, gi, 0, 0)),
        out_shape=jax.ShapeDtypeStruct((bsz, g_groups, nc, HEAD_DIM), CDT),
        compiler_params=_params(("arbitrary", "arbitrary")),
        name="compress",
    )(x2, pos, w1.astype(CDT), w2.astype(CDT))


def _cmp_body(q_ref, kc_ref, vct_ref, ovt_ref, gate_ref, o_ref, bias_ref, qs_ref, *, tq, r_heads, nb):
    gi = pl.program_id(1)
    i = pl.program_id(2)
    kc = kc_ref[0, 0]
    nc = kc.shape[0]
    cols = r_heads * tq
    for r in range(r_heads):
        qs_ref[r * tq:(r + 1) * tq, :] = q_ref[0, :, r * HEAD_DIM:(r + 1) * HEAD_DIM]
    s = lax.dot_general(kc, qs_ref[...], (((1,), (1,)), ((), ())), preferred_element_type=F32)
    cidx = lax.broadcasted_iota(jnp.int32, (nc, cols), 0)
    tok = i * tq + (lax.broadcasted_iota(jnp.int32, (nc, cols), 1) & (tq - 1))
    valid = cidx * CMP_STRIDE + (CMP_LEN - 1) <= tok
    s = jnp.where(valid, s, NEG)
    e = jnp.exp2(s - jnp.max(s, axis=0, keepdims=True))
    p = jnp.where(valid, e, 0.0) * (1.0 / jnp.sum(e, axis=0, keepdims=True))
    o_t = jnp.dot(vct_ref[0, 0], p.astype(CDT), preferred_element_type=F32)
    gates_t = gate_ref[0].T
    rid = lax.broadcasted_iota(jnp.int32, gates_t.shape, 0)
    psum = None
    for r in range(r_heads):
        cs = slice(r * tq, (r + 1) * tq)
        col = (gi * r_heads + r) * 3
        grow = jnp.sum(jnp.where(rid == col, gates_t, 0.0), axis=0, keepdims=True)
        o_ref[0, :, r * HEAD_DIM:(r + 1) * HEAD_DIM] = (o_t[:, cs] * grow).T.astype(o_ref.dtype)
        psum = p[:, cs] if psum is None else psum + p[:, cs]

    p_hi = psum.astype(CDT)
    p_lo = (psum - p_hi.astype(F32)).astype(CDT)
    imp = (jnp.dot(ovt_ref[...], p_hi, preferred_element_type=F32)
           + jnp.dot(ovt_ref[...], p_lo, preferred_element_type=F32))
    jj = lax.broadcasted_iota(jnp.int32, (LANES, tq), 0)
    cur = (i * tq + lax.broadcasted_iota(jnp.int32, (LANES, tq), 1)) // SEL_LEN
    forced = (jj == 0) | ((jj <= cur) & (jj > cur - N_LOCAL_FORCED))
    imp = jnp.where(jj > cur, NEG, imp + jnp.where(forced, FORCED_BONUS, 0.0))
    v = imp[:nb]
    ridx = lax.broadcasted_iota(jnp.int32, (nb, tq), 0)
    sel = jnp.zeros((nb, tq), F32)
    for _ in range(min(N_SEL, nb)):
        mx = jnp.max(v, axis=0, keepdims=True)
        first = jnp.min(jnp.where(v == mx, ridx, nb), axis=0, keepdims=True)
        hit = ridx == first
        sel = jnp.where(hit, 1.0, sel)
        v = jnp.where(hit, -jnp.inf, v)
    bias_t = jnp.where(sel > 0.0, 0.0, NEG)
    if nb < LANES:
        bias_t = jnp.concatenate([bias_t, jnp.zeros((LANES - nb, tq), F32)], axis=0)
    bias_ref[0, 0] = bias_t.T.astype(bias_ref.dtype)


def _cmp_attn(qkv, kc, vct, overlap_t, gates, *, g_groups, r_heads, seq):
    bsz = qkv.shape[0]
    nc = kc.shape[2]
    nb = seq // SEL_LEN
    tq = _tile(seq, (512, 256, 128))
    qw = r_heads * HEAD_DIM
    body = functools.partial(_cmp_body, tq=tq, r_heads=r_heads, nb=nb)
    return pl.pallas_call(
        body,
        grid=(bsz, g_groups, seq // tq),
        in_specs=[pl.BlockSpec((1, tq, qw), lambda b, g, i: (b, i, g)),
                  pl.BlockSpec((1, 1, nc, HEAD_DIM), lambda b, g, i: (b, g, 0, 0)),
                  pl.BlockSpec((1, 1, HEAD_DIM, nc), lambda b, g, i: (b, g, 0, 0)),
                  pl.BlockSpec((LANES, nc), lambda b, g, i: (0, 0)),
                  pl.BlockSpec((1, tq, LANES), lambda b, g, i: (b, i, 1))],
        out_specs=(pl.BlockSpec((1, tq, qw), lambda b, g, i: (b, i, g)),
                   pl.BlockSpec((1, 1, tq, LANES), lambda b, g, i: (b, g, i, 0))),
        out_shape=(jax.ShapeDtypeStruct((bsz, seq, g_groups * qw), F32),
                   jax.ShapeDtypeStruct((bsz, g_groups, seq, LANES), CDT)),
        scratch_shapes=[pltpu.VMEM((r_heads * tq, HEAD_DIM), CDT)],
        compiler_params=_params(("arbitrary", "arbitrary", "arbitrary")),
        name="cmp_attn",
    )(qkv, kc, vct, overlap_t, gates)


ACC_ROWS = HEAD_DIM + 16


def _with_ones_row(vt):
    ones_row = jnp.where(lax.broadcasted_iota(jnp.int32, (ACC_ROWS - HEAD_DIM, vt.shape[1]), 0) == 0,
                         1.0, 0.0).astype(CDT)
    return jnp.concatenate([vt, ones_row], axis=0)


def _flash_init(m_ref, acc_ref):
    m_ref[...] = jnp.full(m_ref.shape, -jnp.inf, F32)
    acc_ref[...] = jnp.zeros(acc_ref.shape, F32)


def _nsa_body(q_ref, bias_ref, ks_ref, e_ref, vs_ref, kw_ref, vw_ref, gate_ref, prev_ref, o_ref,
              qc_ref, s0_ref, s1_ref, cm_ref, m_ref, acc_ref, *, tq, r_heads, n_win):
    s_refs = (s0_ref, s1_ref)
    gi = pl.program_id(1)
    i = pl.program_id(2)
    tk = tq
    cols = r_heads * tq
    for r in range(r_heads):
        qc_ref[r * tq:(r + 1) * tq, :] = jnp.concatenate(
            [q_ref[0, :, r * HEAD_DIM:(r + 1) * HEAD_DIM], bias_ref[0, 0]], axis=1)
    _flash_init(m_ref, acc_ref)

    def local_iotas():
        key = lax.broadcasted_iota(jnp.int32, (tk, cols), 0)
        tok = lax.broadcasted_iota(jnp.int32, (tk, cols), 1) & (tq - 1)
        return key, tok

    def put(s, buf):
        s_refs[buf][...] = s
        cm_ref[buf] = jnp.max(s, axis=0, keepdims=True)

    def update(branch, vt, buf):
        m_prev = m_ref[branch]
        m_new = jnp.maximum(m_prev, cm_ref[buf])
        p = jnp.exp2(s_refs[buf][...] - m_new).astype(CDT)
        acc_ref[branch] = (jnp.exp2(m_prev - m_new) * acc_ref[branch]
                           + jnp.dot(_with_ones_row(vt), p, preferred_element_type=F32))
        m_ref[branch] = m_new

    def sel_scores(j, buf, masked):
        off = pl.multiple_of(j * tk, tk)
        kmat = jnp.concatenate([ks_ref[0, pl.ds(off, tk), :], e_ref[pl.ds(off, tk), :]], axis=1)
        s = lax.dot_general(kmat, qc_ref[...], (((1,), (1,)), ((), ())), preferred_element_type=F32)
        if masked:
            key, tok = local_iotas()
            s = jnp.where(key <= tok, s, NEG)
        put(s, buf)

    def sel_update(j, buf):
        update(0, vs_ref[0, 0, j], buf)

    def win_tile(dd):
        back = n_win - 1 - dd
        return back, jnp.maximum(i - back, 0)

    def win_scores(dd, buf):
        back, j = win_tile(dd)
        off = pl.multiple_of(j * tk, tk)
        s = lax.dot_general(kw_ref[0, pl.ds(off, tk), :], qc_ref[:, :HEAD_DIM], (((1,), (1,)), ((), ())),
                            preferred_element_type=F32)
        key, tok = local_iotas()
        if back * tk - (tk - 1) < 0:
            s = jnp.where(key - back * tk <= tok, s, NEG)
        if back > 0 or tq - 1 + back * tk >= WINDOW:
            reach = WINDOW - back * tk
            if back > 0:
                reach = reach - jnp.where(i >= back, 0, 2 * WINDOW + tq)
            s = jnp.where(tok < key + reach, s, NEG)
        put(s, buf)

    def win_update(dd, buf):
        update(1, vw_ref[0, 0, win_tile(dd)[1]], buf)

    def finish(fb):
        wbuf = lambda dd: fb if dd % 2 else 1 - fb
        win_scores(0, wbuf(0))
        sel_update(i, fb)
        for dd in range(1, n_win):
            win_scores(dd, wbuf(dd))
            win_update(dd - 1, wbuf(dd - 1))
        win_update(n_win - 1, wbuf(n_win - 1))

    @pl.when(i == 0)
    def _():
        sel_scores(0, 0, True)

    @pl.when(i > 0)
    def _():
        sel_scores(0, 0, False)

    n_pairs = jnp.maximum(i - 1, 0) // 2

    def body(jp, carry):
        j = 2 * jp
        sel_scores(j + 1, 1, False)
        sel_update(j, 0)
        sel_scores(j + 2, 0, False)
        sel_update(j + 1, 1)
        return carry

    lax.fori_loop(0, n_pairs, body, 0)
    j0 = 2 * n_pairs
    left = i - j0

    @pl.when(left == 1)
    def _():
        sel_scores(i, 1, True)
        sel_update(j0, 0)
        finish(1)

    @pl.when(left == 2)
    def _():
        sel_scores(j0 + 1, 1, False)
        sel_update(j0, 0)
        sel_scores(i, 0, True)
        sel_update(j0 + 1, 1)
        finish(0)

    @pl.when(left == 0)
    def _():
        finish(0)

    gates_t = gate_ref[0].T
    rid = lax.broadcasted_iota(jnp.int32, gates_t.shape, 0)
    for r in range(r_heads):
        cs = slice(r * tq, (r + 1) * tq)
        o = None
        for branch in (0, 1):
            acc = acc_ref[branch]
            col = (gi * r_heads + r) * 3 + 1 + branch
            grow = jnp.sum(jnp.where(rid == col, gates_t, 0.0), axis=0, keepdims=True)
            ob = acc[:HEAD_DIM, cs] * (grow / acc[HEAD_DIM:HEAD_DIM + 1, cs])
            o = ob if o is None else o + ob
        sl = slice(r * HEAD_DIM, (r + 1) * HEAD_DIM)
        o_ref[0, :, sl] = (prev_ref[0, :, sl] + o.T).astype(o_ref.dtype)


def _nsa_attn(qkv, ks_blk0, kw_blk0, vt, gates, prev, bias, e_mat, *, g_groups, r_heads, seq):
    bsz = qkv.shape[0]
    qw = r_heads * HEAD_DIM
    tq = tk = vt.shape[-1]
    cols = r_heads * tq
    kspec = lambda blk0: pl.BlockSpec((1, seq, HEAD_DIM), lambda b, g, i: (b, 0, blk0 + g))
    vspec = lambda h0: pl.BlockSpec((1, 1) + vt.shape[2:], lambda b, g, i: (b, h0 + g, 0, 0, 0))
    qspec = pl.BlockSpec((1, tq, qw), lambda b, g, i: (b, i, g))
    body = functools.partial(_nsa_body, tq=tq, r_heads=r_heads, n_win=(WINDOW + tk - 1) // tk + 1)
    return pl.pallas_call(
        body,
        grid=(bsz, g_groups, seq // tq),
        in_specs=[qspec, pl.BlockSpec((1, 1, tq, LANES), lambda b, g, i: (b, g, i, 0)),
                  kspec(ks_blk0), pl.BlockSpec((seq, LANES), lambda b, g, i: (0, 0)), vspec(0),
                  kspec(kw_blk0), vspec(g_groups),
                  pl.BlockSpec((1, tq, LANES), lambda b, g, i: (b, i, 1)), qspec],
        out_specs=qspec,
        out_shape=jax.ShapeDtypeStruct((bsz, seq, g_groups * qw), CDT),
        scratch_shapes=[pltpu.VMEM((cols, 2 * HEAD_DIM), CDT),
                        pltpu.VMEM((tk, cols), F32), pltpu.VMEM((tk, cols), F32),
                        pltpu.VMEM((2, 1, cols), F32),
                        pltpu.VMEM((2, 1, cols), F32),
                        pltpu.VMEM((2, ACC_ROWS, cols), F32)],
        compiler_params=_params(("arbitrary",) * 3),
        name="nsa_sel_win",
    )(qkv, bias, qkv, e_mat, vt, qkv, vt, gates, prev)


def _mla_body(q_ref, kn_ref, kr_ref, vt_ref, o_ref, *rest, tq, tk, hg):
    s_refs = rest[:hg]
    cm_ref, m_ref, acc_ref = rest[hg:]
    i = pl.program_id(2)
    _flash_init(m_ref, acc_ref)
    n_full = (i * tq) // tk

    def step(j, mask, c0):
        off = pl.multiple_of(j * tk, tk)
        kr = kr_ref[0, pl.ds(off, tk), :]

        def scores(c):
            kmat = jnp.concatenate([kn_ref[0, pl.ds(off, tk), c * MLA_NOPE:(c + 1) * MLA_NOPE], kr], axis=1)
            s = lax.dot_general(kmat, q_ref[0, c0:, c * MLA_QK_PAD:(c + 1) * MLA_QK_PAD],
                                (((1,), (1,)), ((), ())), preferred_element_type=F32)
            if mask is not None:
                s = jnp.where(mask, s, NEG)
            s_refs[c][:, c0:] = s
            cm_ref[c, :, c0:] = jnp.max(s, axis=0, keepdims=True)

        def update(c):
            m_prev = m_ref[c, :, c0:]
            m_new = jnp.maximum(m_prev, cm_ref[c, :, c0:])
            p = jnp.exp2(s_refs[c][:, c0:] - m_new).astype(CDT)
            acc_ref[c, :, c0:] = (jnp.exp2(m_prev - m_new) * acc_ref[c, :, c0:]
                                  + jnp.dot(_with_ones_row(vt_ref[0, c, j]), p, preferred_element_type=F32))
            m_ref[c, :, c0:] = m_new

        scores(0)
        for c in range(1, hg):
            scores(c)
            update(c - 1)
        update(hg - 1)

    def full_tile(j, carry):
        step(j, None, 0)
        return carry

    lax.fori_loop(0, n_full, full_tile, 0)
    for dd in range(tq // tk):
        w = tq - dd * tk
        mask = (lax.broadcasted_iota(jnp.int32, (tk, w), 0) <= lax.broadcasted_iota(jnp.int32, (tk, w), 1))
        step(n_full + dd, mask, dd * tk)
    for c in range(hg):
        acc = acc_ref[c]
        o = acc[:HEAD_DIM] * (1.0 / acc[HEAD_DIM:HEAD_DIM + 1])
        o_ref[0, :, c * MLA_V:(c + 1) * MLA_V] = o.T.astype(o_ref.dtype)


def _mla_attn(q, kv, kr, vt, *, heads, seq):
    bsz = q.shape[0]
    tk = vt.shape[-1]
    tq = _tile(seq, (1024, 512, 256, 128))
    hg = 4 if heads % 4 == 0 else 1
    body = functools.partial(_mla_body, tq=tq, tk=tk, hg=hg)
    return pl.pallas_call(
        body,
        grid=(bsz, heads // hg, seq // tq),
        in_specs=[pl.BlockSpec((1, tq, hg * MLA_QK_PAD), lambda b, h, i: (b, i, h)),
                  pl.BlockSpec((1, seq, hg * MLA_NOPE), lambda b, h, i: (b, 0, h)),
                  pl.BlockSpec((1, seq, LANES), lambda b, h, i: (b, 0, 0)),
                  pl.BlockSpec((1, hg) + vt.shape[2:], lambda b, h, i: (b, h, 0, 0, 0))],
        out_specs=pl.BlockSpec((1, tq, hg * MLA_V), lambda b, h, i: (b, i, h)),
        out_shape=jax.ShapeDtypeStruct((bsz, seq, heads * MLA_V), CDT),
        scratch_shapes=[pltpu.VMEM((tk, tq), F32)] * hg + [
            pltpu.VMEM((hg, 1, tq), F32), pltpu.VMEM((hg, 1, tq), F32), pltpu.VMEM((hg, ACC_ROWS, tq), F32)],
        compiler_params=_params(("arbitrary",) * 3),
        name="mla_attn",
    )(q, kv, kr, vt)


def _rope_tables(positions, rot_dim, period, offset):
    half = rot_dim // 2
    inv = jnp.power(ROPE_THETA, -jnp.arange(0, rot_dim, 2, dtype=F32) / rot_dim)
    ang = positions.astype(F32).reshape(-1)[:, None] * inv
    cos, sin = jnp.cos(ang), jnp.sin(ang)
    t = ang.shape[0]
    ones_l = jnp.ones((t, offset), F32)
    zeros_l = jnp.zeros((t, offset), F32)
    rest = period - offset - rot_dim
    ct = jnp.concatenate([ones_l, cos, cos, jnp.ones((t, rest), F32)], axis=1)
    sa = jnp.concatenate([zeros_l, -sin, jnp.zeros((t, half + rest), F32)], axis=1)
    sb = jnp.concatenate([zeros_l, jnp.zeros((t, half), F32), sin, jnp.zeros((t, rest), F32)], axis=1)
    return ct, sa, sb


def _table_extras(tables, tm):
    return [(tb, pl.BlockSpec((tm, tb.shape[1]), lambda i, j: (i, 0))) for tb in tables]


def _layer(x2, mod, positions, w_in, pos_k, pos_v, k1, k2, v1, v2, q_norm, kv_norm, w_uq, w_ukv,
           w_out, ln1_g, ln1_b, w_ff1, w_ff2, ln2_g, ln2_b, *, bsz, seq, alpha):
    t, d = x2.shape
    n_heads = d // HEAD_DIM
    nsa_h = n_heads // 2
    mla_h = n_heads - nsa_h
    q_rank = q_norm.shape[0]
    kv_rank = kv_norm.shape[0]
    d_in = w_in.shape[1]
    g_groups = (d_in - nsa_h * HEAD_DIM - nsa_h * 3 - q_rank - kv_rank - MLA_ROPE) // (6 * HEAD_DIM)
    r_heads = nsa_h // g_groups
    gw = g_groups * HEAD_DIM
    qw = nsa_h * HEAD_DIM
    sh_a, sc_a, g_a, sh_m, sc_m, g_m = jnp.split(mod, N_ADA, axis=-1)

    o_q, o_kv = 0, qw
    o_gate = o_kv + 6 * gw
    o_cq = o_gate + nsa_h * 3
    o_ckv = o_cq + q_rank
    o_kr = o_ckv + kv_rank
    kvcol = lambda idx: w_in[:, o_kv + idx * gw:o_kv + (idx + 1) * gw]
    w1c = jnp.concatenate([w_in[:, o_q:o_q + qw], kvcol(2), kvcol(4)], axis=1).astype(CDT)
    wvc = jnp.concatenate([kvcol(3), kvcol(5)], axis=1).astype(CDT)
    w2c = jnp.concatenate([w_in[:, o_cq:o_cq + q_rank], w_in[:, o_ckv:o_ckv + kv_rank], kvcol(0),
                           kvcol(1)], axis=1).astype(CDT)
    zpad = lambda n: jnp.zeros((d, n), w_in.dtype)
    w3c = jnp.concatenate([w_in[:, o_kr:o_kr + MLA_ROPE], zpad(LANES - MLA_ROPE),
                           w_in[:, o_gate:o_gate + nsa_h * 3], zpad(LANES - nsa_h * 3)],
                          axis=1).astype(CDT)

    h = _modcast(x2, sc_a, sh_a, seq)

    tm = _tile(t, (1024, 512, 256, 128))
    nsa_tabs = _rope_tables(positions, PARTIAL_ROT, LANES, 0)
    n1 = w1c.shape[1]
    tn1 = _tile(gw, (512, 256, 128))
    scale = HEAD_DIM ** -0.5 * LOG2E
    p1 = _mm([h], w1c, tm=tm, tn=tn1, out_dtype=CDT, name="in_proj_rot",
             epilogue=_rope_epilogue(PARTIAL_ROT // 2, 0, n1 // tn1, qw // tn1, scale),
             extras=_table_extras(nsa_tabs, tm))
    tk = _tile(seq, (512, 256, 128))
    vt_nsa = _mm([h], wvc, tm=tm, tn=_tile(2 * gw, (512, 256, 128)), out_dtype=CDT, name="in_proj_vt",
                 head_tiles=tk, seq=seq)
    n2 = w2c.shape[1]
    lat = q_rank + kv_rank
    tn2 = _tile(math.gcd(lat, gw), (512, 256, 128))
    p2 = _mm([h], w2c, tm=tm, tn=tn2, out_dtype=F32, name="in_proj_lat",
             epilogue=_rope_epilogue(PARTIAL_ROT // 2, lat // tn2, (lat + gw) // tn2, 0, 1.0),
             extras=_table_extras(nsa_tabs, tm))
    kr_tabs = _rope_tables(positions, MLA_ROPE, LANES, 0)
    p3 = _mm([h], w3c, tm=tm, tn=2 * LANES, out_dtype=F32, name="in_proj_kr_gate",
             epilogue=_krope_gate_epilogue, extras=_table_extras(kr_tabs, tm))

    qkv = p1.reshape(bsz, seq, n1)
    gates = p3.reshape(bsz, seq, 2 * LANES)

    nchunk = seq // CMP_STRIDE
    ncmp = (seq - CMP_LEN) // CMP_STRIDE + 1
    blk = lambda col: col // HEAD_DIM
    kc = _compress(p2, blk(lat), pos_k, k1, k2, bsz=bsz, seq=seq, g_groups=g_groups)
    vc = _compress(p2, blk(lat + gw), pos_v, v1, v2, bsz=bsz, seq=seq, g_groups=g_groups)

    nb = seq // SEL_LEN
    c_start = np.arange(nchunk) * CMP_STRIDE
    b_start = np.arange(LANES) * SEL_LEN
    overlap = ((c_start[:, None] < b_start[None, :] + SEL_LEN) &
               (c_start[:, None] + CMP_LEN > b_start[None, :]) &
               (np.arange(nchunk)[:, None] < ncmp) & (np.arange(LANES)[None, :] < nb))
    overlap_t = jnp.asarray(overlap.T.astype(np.float32)).astype(CDT)
    o_cmp, bias = _cmp_attn(qkv, kc, vc.transpose(0, 1, 3, 2), overlap_t, gates, g_groups=g_groups,
                            r_heads=r_heads, seq=seq)

    e_mat = jnp.asarray((np.arange(seq)[:, None] // SEL_LEN == np.arange(LANES)[None, :])
                        .astype(np.float32)).astype(CDT)
    o_nsa = _nsa_attn(qkv, blk(qw), blk(qw + gw), vt_nsa, gates, o_cmp, bias, e_mat,
                      g_groups=g_groups, r_heads=r_heads, seq=seq)

    qk_dim = MLA_NOPE + MLA_ROPE
    wq = w_uq.reshape(q_rank, mla_h, qk_dim)
    wq = jnp.concatenate([wq, jnp.zeros((q_rank, mla_h, MLA_QK_PAD - qk_dim), wq.dtype)], axis=-1)
    wq = wq.reshape(q_rank, mla_h * MLA_QK_PAD).astype(CDT)
    wkv = w_ukv.reshape(kv_rank, mla_h, MLA_NOPE + MLA_V)
    wk = wkv[:, :, :MLA_NOPE].reshape(kv_rank, mla_h * MLA_NOPE).astype(CDT)
    wv = wkv[:, :, MLA_NOPE:].reshape(kv_rank, mla_h * MLA_V).astype(CDT)
    tnq = _tile(mla_h * MLA_QK_PAD, (512, 256))
    q_mla = _mm([p2], wq, tm=tm, tn=tnq, out_dtype=CDT, name="mla_uq",
                a_specs=[pl.BlockSpec((tm, q_rank), lambda i, j: (i, 0))],
                prologue=_rms_prologue(1e-6),
                epilogue=_mla_q_epilogue(qk_dim ** -0.5 * LOG2E),
                extras=_table_extras(kr_tabs, tm)
                + [(q_norm.reshape(1, q_rank), pl.BlockSpec((1, q_rank), lambda i, j: (0, 0)))])
    tnkv = _tile(wk.shape[1], (512, 256, 128))
    ckv_args = dict(tm=tm, tn=tnkv, out_dtype=CDT, prologue=_rms_prologue(1e-6),
                    a_specs=[pl.BlockSpec((tm, kv_rank), lambda i, j: (i, q_rank // kv_rank))],
                    extras=[(kv_norm.reshape(1, kv_rank), pl.BlockSpec((1, kv_rank), lambda i, j: (0, 0)))])
    k_mla = _mm([p2], wk, name="mla_uk", **ckv_args)
    vt_mla = _mm([p2], wv, name="mla_uv", head_tiles=tk, seq=seq, **ckv_args)
    o_mla = _mla_attn(q_mla.reshape(bsz, seq, -1), k_mla.reshape(bsz, seq, -1),
                      gates[:, :, :LANES].astype(CDT), vt_mla, heads=mla_h, seq=seq)

    tno = _tile(d, (1024, 512, 256, 128))
    a = _mm([o_nsa.reshape(t, qw), o_mla.reshape(t, mla_h * MLA_V)], w_out.astype(CDT), tm=tm, tn=tno,
            out_dtype=F32, name="out_proj")
    x1, h2 = _res_ln(x2, a, g_a, ln1_g, ln1_b, seq, alpha, mod=(sc_m, sh_m))
    d_ff = w_ff1.shape[1]
    f1 = _mm([h2], w_ff1.astype(CDT), tm=tm, tn=_tile(d_ff, (1024, 512, 256, 128)), out_dtype=CDT,
             name="ff1", epilogue=_relu2_epilogue)
    f2 = _mmk(f1, w_ff2.astype(CDT), tm=tm, tn=tno, tk=_tile(d_ff, (4096, 2048, 1024, 512)),
              out_dtype=F32, name="ff2")
    return _res_ln(x1, f2, g_m, ln2_g, ln2_b, seq, alpha)


def kernel(x, c, positions, w_ada, b_ada, w_in, nsa_pos_k, nsa_pos_v, nsa_cmp_k1, nsa_cmp_k2, nsa_cmp_v1, nsa_cmp_v2, mla_q_norm, mla_kv_norm, mla_w_uq, mla_w_ukv, w_out, ln1_g, ln1_b, w_ff1, w_ff2, ln2_g, ln2_b):
    bsz, seq, d = x.shape
    depth = w_ada.shape[0]
    alpha = (2 * depth) ** 0.25
    x2 = x.reshape(bsz * seq, d)
    for layer in range(depth):
        mod = _ada(c, w_ada[layer], b_ada[layer])
        x2 = _layer(x2, mod, positions, w_in[layer], nsa_pos_k[layer], nsa_pos_v[layer],
                    nsa_cmp_k1[layer], nsa_cmp_k2[layer], nsa_cmp_v1[layer], nsa_cmp_v2[layer],
                    mla_q_norm[layer], mla_kv_norm[layer], mla_w_uq[layer], mla_w_ukv[layer],
                    w_out[layer], ln1_g[layer], ln1_b[layer], w_ff1[layer], w_ff2[layer],
                    ln2_g[layer], ln2_b[layer], bsz=bsz, seq=seq, alpha=alpha)
    return x2.reshape(bsz, seq, d)
```

```python
import functools
import math

import numpy as np
import jax
import jax.numpy as jnp
from jax import lax
from jax.experimental import pallas as pl
from jax.experimental.pallas import tpu as pltpu

HEAD_DIM = 128
CMP_LEN = 32
CMP_STRIDE = 16
SEL_LEN = 64
N_SEL = 16
N_LOCAL_FORCED = 2
FORCED_BONUS = 1e4
WINDOW = 512
MLA_NOPE = 128
MLA_ROPE = 64
MLA_V = 128
ROPE_THETA = 500000.0
PARTIAL_ROT = HEAD_DIM // 4
NEG = -1e30
LOG2E = math.log2(math.e)
N_ADA = 6
LANES = 128
MLA_QK_PAD = 2 * LANES
VMEM_LIMIT = 56 * 1024 * 1024
MM_SUB_COLS = 256

F32 = jnp.float32
CDT = jnp.bfloat16


def _params(sem):
    return pltpu.CompilerParams(dimension_semantics=sem, vmem_limit_bytes=VMEM_LIMIT)


def _tile(n, cands):
    for c in cands:
        if n % c == 0:
            return c
    return n


def _ada_body(c_ref, w_ref, b_ref, o_ref):
    c = c_ref[...]
    cond = c * (1.0 / (1.0 + jnp.exp(-c)))
    o_ref[...] = jnp.dot(cond.astype(CDT), w_ref[...].astype(CDT),
                         preferred_element_type=F32) + b_ref[...]


def _ada(c, w, b):
    bsz, d = c.shape
    n = w.shape[1]
    tn = _tile(n, (512, 256, 128))
    return pl.pallas_call(
        _ada_body,
        grid=(n // tn,),
        in_specs=[pl.BlockSpec((bsz, d), lambda j: (0, 0)),
                  pl.BlockSpec((d, tn), lambda j: (0, j)),
                  pl.BlockSpec((1, tn), lambda j: (0, j))],
        out_specs=pl.BlockSpec((bsz, tn), lambda j: (0, j)),
        out_shape=jax.ShapeDtypeStruct((bsz, n), F32),
        compiler_params=_params(("arbitrary",)),
        name="ada",
    )(c, w, b.reshape(1, n))


def _modcast_body(x_ref, sc_ref, sh_ref, o_ref):
    o_ref[...] = (x_ref[...] * (1.0 + sc_ref[0]) + sh_ref[0]).astype(o_ref.dtype)


def _modcast(x2, sc, sh, seq):
    t, d = x2.shape
    ts = _tile(seq, (256, 128, 64, 8))
    per = seq // ts
    vec = pl.BlockSpec((1, 1, d), lambda i: (i // per, 0, 0))
    return pl.pallas_call(
        _modcast_body,
        grid=(t // ts,),
        in_specs=[pl.BlockSpec((ts, d), lambda i: (i, 0)), vec, vec],
        out_specs=pl.BlockSpec((ts, d), lambda i: (i, 0)),
        out_shape=jax.ShapeDtypeStruct((t, d), CDT),
        compiler_params=_params(("arbitrary",)),
        name="modcast",
    )(x2, sc[:, None, :], sh[:, None, :])


def _mm_body(*refs, na, nex, prologue, epilogue, tn, head_tiles):
    a_refs = refs[:na]
    b_ref = refs[na]
    ex = refs[na + 1:na + 1 + nex]
    o_ref = refs[na + 1 + nex]
    j = pl.program_id(1)
    if prologue is not None:
        a_sc = refs[na + 2 + nex]

        @pl.when(j == 0)
        def _():
            a_sc[...] = prologue(a_refs[0][...], ex).astype(a_sc.dtype)

        a_refs = (a_sc,)
    sub = MM_SUB_COLS if tn % MM_SUB_COLS == 0 else tn

    def store(s, y):
        if head_tiles is None:
            o_ref[:, s * sub:(s + 1) * sub] = y.astype(o_ref.dtype)
        else:
            for hh in range(sub // HEAD_DIM):
                for tt in range(y.shape[0] // head_tiles):
                    blk = y[tt * head_tiles:(tt + 1) * head_tiles, hh * HEAD_DIM:(hh + 1) * HEAD_DIM]
                    o_ref[0, s * (sub // HEAD_DIM) + hh, tt] = blk.T.astype(o_ref.dtype)

    def run(fn):
        for s in range(tn // sub):
            cols = slice(s * sub, (s + 1) * sub)
            acc = None
            off = 0
            for r in a_refs:
                kr = r.shape[1]
                part = jnp.dot(r[...], b_ref[off:off + kr, cols], preferred_element_type=F32)
                acc = part if acc is None else acc + part
                off += kr
            store(s, fn(acc, j, ex))

    if epilogue is None:
        run(lambda acc, j, ex: acc)
    else:
        for cond, fn in epilogue(j):
            pl.when(cond)(functools.partial(run, fn))


def _mm(a_list, b, *, tm, tn, out_dtype, name, a_specs=None, prologue=None, epilogue=None,
        extras=(), head_tiles=None, seq=None):
    m = a_list[0].shape[0]
    k, n = b.shape
    if a_specs is None:
        a_specs = [pl.BlockSpec((tm, a.shape[1]), lambda i, j: (i, 0)) for a in a_list]
    ex_arrays = [e[0] for e in extras]
    ex_specs = [e[1] for e in extras]
    scratch = [pltpu.VMEM((tm, k), CDT)] if prologue is not None else []
    body = functools.partial(_mm_body, na=len(a_list), nex=len(extras), prologue=prologue,
                             epilogue=epilogue, tn=tn, head_tiles=head_tiles)
    if head_tiles is None:
        out_specs = pl.BlockSpec((tm, tn), lambda i, j: (i, j))
        out_shape = jax.ShapeDtypeStruct((m, n), out_dtype)
    else:
        per = seq // tm
        out_specs = pl.BlockSpec((1, tn // HEAD_DIM, tm // head_tiles, HEAD_DIM, head_tiles),
                                 lambda i, j: (i // per, j, i % per, 0, 0))
        out_shape = jax.ShapeDtypeStruct((m // seq, n // HEAD_DIM, seq // head_tiles, HEAD_DIM, head_tiles),
                                         out_dtype)
    return pl.pallas_call(
        body,
        grid=(m // tm, n // tn),
        in_specs=a_specs + [pl.BlockSpec((k, tn), lambda i, j: (0, j))] + ex_specs,
        out_specs=out_specs,
        out_shape=out_shape,
        scratch_shapes=scratch,
        compiler_params=_params(("arbitrary", "arbitrary")),
        name=name,
    )(*a_list, b, *ex_arrays)


def _rot(acc, cos, sa, sb, shift):
    tn = acc.shape[1]
    reps = tn // cos.shape[1]

    def rep(t):
        return t if reps == 1 else jnp.concatenate([t] * reps, axis=1)

    return (acc * rep(cos) + pltpu.roll(acc, tn - shift, 1) * rep(sa)
            + pltpu.roll(acc, shift, 1) * rep(sb))


def _rope_epilogue(shift, lo, hi, n_scaled, scale):
    def roped(acc, j, ex):
        y = _rot(acc, ex[0][...], ex[1][...], ex[2][...], shift)
        return y * jnp.where(j < n_scaled, scale, 1.0).astype(F32)

    def epi(j):
        inside = (j >= lo) & (j < hi)
        return [(inside, roped), (jnp.logical_not(inside), lambda acc, j, ex: acc)]

    return epi


def _mla_q_epilogue(scale):
    assert MM_SUB_COLS == MLA_QK_PAD

    def fn(acc, j, ex):
        rot = _rot(acc[:, LANES:], ex[0][...], ex[1][...], ex[2][...], MLA_ROPE // 2)
        return jnp.concatenate([acc[:, :LANES], rot], axis=1) * scale

    return lambda j: [(j >= 0, fn)]


def _krope_gate_epilogue(j):
    def fn(acc, j, ex):
        y = _rot(acc, ex[0][...], ex[1][...], ex[2][...], MLA_ROPE // 2)
        lane = lax.broadcasted_iota(jnp.int32, acc.shape, 1)
        return jnp.where(lane < LANES, y, 1.0 / (1.0 + jnp.exp(-acc)))

    return [(j >= 0, fn)]


def _rms_prologue(eps):
    def pro(a, ex):
        g = ex[-1][...]
        y = a * lax.rsqrt(jnp.mean(a * a, axis=-1, keepdims=True) + eps)
        return y * g

    return pro


def _relu2_epilogue(j):
    def fn(acc, j, ex):
        r = jnp.maximum(acc, 0.0)
        return r * r

    return [(j >= 0, fn)]


def _mmk_body(a_ref, b_ref, o_ref, acc_ref):
    kk = pl.program_id(2)

    @pl.when(kk == 0)
    def _():
        acc_ref[...] = jnp.zeros_like(acc_ref)

    acc_ref[...] += jnp.dot(a_ref[...], b_ref[...], preferred_element_type=F32)

    @pl.when(kk == pl.num_programs(2) - 1)
    def _():
        o_ref[...] = acc_ref[...].astype(o_ref.dtype)


def _mmk(a, b, *, tm, tn, tk, out_dtype, name):
    m, k = a.shape
    n = b.shape[1]
    return pl.pallas_call(
        _mmk_body,
        grid=(m // tm, n // tn, k // tk),
        in_specs=[pl.BlockSpec((tm, tk), lambda i, j, kk: (i, kk)),
                  pl.BlockSpec((tk, tn), lambda i, j, kk: (kk, j))],
        out_specs=pl.BlockSpec((tm, tn), lambda i, j, kk: (i, j)),
        out_shape=jax.ShapeDtypeStruct((m, n), out_dtype),
        scratch_shapes=[pltpu.VMEM((tm, tn), F32)],
        compiler_params=_params(("arbitrary", "arbitrary", "arbitrary")),
        name=name,
    )(a, b)


def _ln_body(x_ref, a_ref, gate_ref, g_ref, b_ref, *rest, alpha, with_mod):
    y = alpha * x_ref[...] + (1.0 + gate_ref[0]) * a_ref[...]
    mu = jnp.mean(y, axis=-1, keepdims=True)
    yc = y - mu
    var = jnp.mean(yc * yc, axis=-1, keepdims=True)
    out = yc * lax.rsqrt(var + 1e-5) * g_ref[...] + b_ref[...]
    if with_mod:
        sc_ref, sh_ref, o_ref, h_ref = rest
        o_ref[...] = out
        h_ref[...] = (out * (1.0 + sc_ref[0]) + sh_ref[0]).astype(h_ref.dtype)
    else:
        (o_ref,) = rest
        o_ref[...] = out


def _res_ln(x2, a2, gate, g, b, seq, alpha, mod=None):
    t, d = x2.shape
    ts = _tile(seq, (256, 128, 64, 8))
    per = seq // ts
    row = pl.BlockSpec((ts, d), lambda i: (i, 0))
    vec = pl.BlockSpec((1, 1, d), lambda i: (i // per, 0, 0))
    par = pl.BlockSpec((1, d), lambda i: (0, 0))
    ins = [x2, a2, gate[:, None, :], g.reshape(1, d), b.reshape(1, d)]
    specs = [row, row, vec, par, par]
    if mod is None:
        out_shape = jax.ShapeDtypeStruct((t, d), F32)
        out_specs = row
    else:
        ins += [mod[0][:, None, :], mod[1][:, None, :]]
        specs += [vec, vec]
        out_shape = (jax.ShapeDtypeStruct((t, d), F32), jax.ShapeDtypeStruct((t, d), CDT))
        out_specs = (row, row)
    return pl.pallas_call(
        functools.partial(_ln_body, alpha=alpha, with_mod=mod is not None),
        grid=(t // ts,),
        in_specs=specs,
        out_specs=out_specs,
        out_shape=out_shape,
        compiler_params=_params(("arbitrary",)),
        name="res_ln",
    )(*ins)


def _compress_body(x_ref, pos_ref, w1_ref, w2_ref, o_ref):
    nc = o_ref.shape[2]
    acc_a = None
    acc_b = None
    for l in range(CMP_STRIDE):
        x = x_ref[pl.ds(l, nc, stride=CMP_STRIDE), :]
        xa = (x + pos_ref[l:l + 1, :]).astype(CDT)
        xb = (x + pos_ref[CMP_STRIDE + l:CMP_STRIDE + l + 1, :]).astype(CDT)
        wa = w1_ref[l * HEAD_DIM:(l + 1) * HEAD_DIM, :]
        wb = w1_ref[(CMP_STRIDE + l) * HEAD_DIM:(CMP_STRIDE + l + 1) * HEAD_DIM, :]
        pa = jnp.dot(xa, wa, preferred_element_type=F32)
        pb = jnp.dot(xb, wb, preferred_element_type=F32)
        acc_a = pa if acc_a is None else acc_a + pa
        acc_b = pb if acc_b is None else acc_b + pb
    pre = acc_a + pltpu.roll(acc_b, nc - 1, 0)
    hid = 0.5 * pre * (1.0 + jnp.tanh(math.sqrt(2.0 / math.pi) * (pre + 0.044715 * pre * pre * pre)))
    o_ref[0, 0] = jnp.dot(hid.astype(CDT), w2_ref[...], preferred_element_type=F32).astype(o_ref.dtype)


def _compress(x2, col_blk0, pos, w1, w2, *, bsz, seq, g_groups):
    nc = seq // CMP_STRIDE
    full = lambda shape: pl.BlockSpec(shape, lambda b, gi: (0,) * len(shape))
    return pl.pallas_call(
        _compress_body,
        grid=(bsz, g_groups),
        in_specs=[pl.BlockSpec((seq, HEAD_DIM), lambda b, gi: (b, col_blk0 + gi)),
                  full(pos.shape), full(w1.shape), full(w2.shape)],
        out_specs=pl.BlockSpec((1, 1, nc, HEAD_DIM), lambda b, gi: (b, gi, 0, 0)),
        out_shape=jax.ShapeDtypeStruct((bsz, g_groups, nc, HEAD_DIM), CDT),
        compiler_params=_params(("arbitrary", "arbitrary")),
        name="compress",
    )(x2, pos, w1.astype(CDT), w2.astype(CDT))


def _cmp_branch(qs, kc, vct, ovt, gates_t, gi, i, *, tq, r_heads, nb):
    nc = kc.shape[0]
    cols = r_heads * tq
    s = lax.dot_general(kc, qs, (((1,), (1,)), ((), ())), preferred_element_type=F32)
    cidx = lax.broadcasted_iota(jnp.int32, (nc, cols), 0)
    tok = i * tq + (lax.broadcasted_iota(jnp.int32, (nc, cols), 1) & (tq - 1))
    valid = cidx * CMP_STRIDE + (CMP_LEN - 1) <= tok
    s = jnp.where(valid, s, NEG)
    e = jnp.exp2(s - jnp.max(s, axis=0, keepdims=True))
    p = jnp.where(valid, e, 0.0) * (1.0 / jnp.sum(e, axis=0, keepdims=True))
    o_t = jnp.dot(vct, p.astype(CDT), preferred_element_type=F32)
    rid = lax.broadcasted_iota(jnp.int32, gates_t.shape, 0)
    psum = None
    gated = []
    for r in range(r_heads):
        cs = slice(r * tq, (r + 1) * tq)
        col = (gi * r_heads + r) * 3
        grow = jnp.sum(jnp.where(rid == col, gates_t, 0.0), axis=0, keepdims=True)
        gated.append(o_t[:, cs] * grow)
        psum = p[:, cs] if psum is None else psum + p[:, cs]

    p_hi = psum.astype(CDT)
    p_lo = (psum - p_hi.astype(F32)).astype(CDT)
    imp = (jnp.dot(ovt, p_hi, preferred_element_type=F32)
           + jnp.dot(ovt, p_lo, preferred_element_type=F32))
    jj = lax.broadcasted_iota(jnp.int32, (LANES, tq), 0)
    cur = (i * tq + lax.broadcasted_iota(jnp.int32, (LANES, tq), 1)) // SEL_LEN
    forced = (jj == 0) | ((jj <= cur) & (jj > cur - N_LOCAL_FORCED))
    imp = jnp.where(jj > cur, NEG, imp + jnp.where(forced, FORCED_BONUS, 0.0))
    v = imp[:nb]
    ridx = lax.broadcasted_iota(jnp.int32, (nb, tq), 0)
    sel = jnp.zeros((nb, tq), F32)
    for _ in range(min(N_SEL, nb)):
        mx = jnp.max(v, axis=0, keepdims=True)
        first = jnp.min(jnp.where(v == mx, ridx, nb), axis=0, keepdims=True)
        hit = ridx == first
        sel = jnp.where(hit, 1.0, sel)
        v = jnp.where(hit, -jnp.inf, v)
    bias_t = jnp.where(sel > 0.0, 0.0, NEG)
    if nb < LANES:
        bias_t = jnp.concatenate([bias_t, jnp.zeros((LANES - nb, tq), F32)], axis=0)
    return jnp.concatenate(gated, axis=1), bias_t.T.astype(CDT)


ACC_ROWS = HEAD_DIM + 16


def _with_ones_row(vt):
    ones_row = jnp.where(lax.broadcasted_iota(jnp.int32, (ACC_ROWS - HEAD_DIM, vt.shape[1]), 0) == 0,
                         1.0, 0.0).astype(CDT)
    return jnp.concatenate([vt, ones_row], axis=0)


def _flash_init(m_ref, acc_ref):
    m_ref[...] = jnp.full(m_ref.shape, -jnp.inf, F32)
    acc_ref[...] = jnp.zeros(acc_ref.shape, F32)


def _nsa_body(q_ref, kc_ref, vct_ref, ovt_ref, ks_ref, e_ref, vs_ref, kw_ref, vw_ref, gate_ref, o_ref,
              qc_ref, oc_ref, s0_ref, s1_ref, cm_ref, m_ref, acc_ref, *, tq, r_heads, n_win, nb):
    s_refs = (s0_ref, s1_ref)
    gi = pl.program_id(1)
    i = pl.program_id(2)
    tk = tq
    cols = r_heads * tq
    for r in range(r_heads):
        qc_ref[r * tq:(r + 1) * tq, :HEAD_DIM] = q_ref[0, :, r * HEAD_DIM:(r + 1) * HEAD_DIM]
    oc_ref[...], bias = _cmp_branch(qc_ref[:, :HEAD_DIM], kc_ref[0, 0], vct_ref[0, 0], ovt_ref[...],
                                    gate_ref[0].T, gi, i, tq=tq, r_heads=r_heads, nb=nb)
    for r in range(r_heads):
        qc_ref[r * tq:(r + 1) * tq, HEAD_DIM:] = bias
    _flash_init(m_ref, acc_ref)

    def local_iotas():
        key = lax.broadcasted_iota(jnp.int32, (tk, cols), 0)
        tok = lax.broadcasted_iota(jnp.int32, (tk, cols), 1) & (tq - 1)
        return key, tok

    def put(s, buf):
        s_refs[buf][...] = s
        cm_ref[buf] = jnp.max(s, axis=0, keepdims=True)

    def update(branch, vt, buf):
        m_prev = m_ref[branch]
        m_new = jnp.maximum(m_prev, cm_ref[buf])
        p = jnp.exp2(s_refs[buf][...] - m_new).astype(CDT)
        acc_ref[branch] = (jnp.exp2(m_prev - m_new) * acc_ref[branch]
                           + jnp.dot(_with_ones_row(vt), p, preferred_element_type=F32))
        m_ref[branch] = m_new

    def sel_scores(j, buf, masked):
        off = pl.multiple_of(j * tk, tk)
        kmat = jnp.concatenate([ks_ref[0, pl.ds(off, tk), :], e_ref[pl.ds(off, tk), :]], axis=1)
        s = lax.dot_general(kmat, qc_ref[...], (((1,), (1,)), ((), ())), preferred_element_type=F32)
        if masked:
            key, tok = local_iotas()
            s = jnp.where(key <= tok, s, NEG)
        put(s, buf)

    def sel_update(j, buf):
        update(0, vs_ref[0, 0, j], buf)

    def win_tile(dd):
        back = n_win - 1 - dd
        return back, jnp.maximum(i - back, 0)

    def win_scores(dd, buf):
        back, j = win_tile(dd)
        off = pl.multiple_of(j * tk, tk)
        s = lax.dot_general(kw_ref[0, pl.ds(off, tk), :], qc_ref[:, :HEAD_DIM], (((1,), (1,)), ((), ())),
                            preferred_element_type=F32)
        key, tok = local_iotas()
        if back * tk - (tk - 1) < 0:
            s = jnp.where(key - back * tk <= tok, s, NEG)
        if back > 0 or tq - 1 + back * tk >= WINDOW:
            reach = WINDOW - back * tk
            if back > 0:
                reach = reach - jnp.where(i >= back, 0, 2 * WINDOW + tq)
            s = jnp.where(tok < key + reach, s, NEG)
        put(s, buf)

    def win_update(dd, buf):
        update(1, vw_ref[0, 0, win_tile(dd)[1]], buf)

    def finish(fb):
        wbuf = lambda dd: fb if dd % 2 else 1 - fb
        win_scores(0, wbuf(0))
        sel_update(i, fb)
        for dd in range(1, n_win):
            win_scores(dd, wbuf(dd))
            win_update(dd - 1, wbuf(dd - 1))
        win_update(n_win - 1, wbuf(n_win - 1))

    @pl.when(i == 0)
    def _():
        sel_scores(0, 0, True)

    @pl.when(i > 0)
    def _():
        sel_scores(0, 0, False)

    n_pairs = jnp.maximum(i - 1, 0) // 2

    def body(jp, carry):
        j = 2 * jp
        sel_scores(j + 1, 1, False)
        sel_update(j, 0)
        sel_scores(j + 2, 0, False)
        sel_update(j + 1, 1)
        return carry

    lax.fori_loop(0, n_pairs, body, 0)
    j0 = 2 * n_pairs
    left = i - j0

    @pl.when(left == 1)
    def _():
        sel_scores(i, 1, True)
        sel_update(j0, 0)
        finish(1)

    @pl.when(left == 2)
    def _():
        sel_scores(j0 + 1, 1, False)
        sel_update(j0, 0)
        sel_scores(i, 0, True)
        sel_update(j0 + 1, 1)
        finish(0)

    @pl.when(left == 0)
    def _():
        finish(0)

    gates_t = gate_ref[0].T
    rid = lax.broadcasted_iota(jnp.int32, gates_t.shape, 0)
    for r in range(r_heads):
        cs = slice(r * tq, (r + 1) * tq)
        o = oc_ref[:, cs]
        for branch in (0, 1):
            acc = acc_ref[branch]
            col = (gi * r_heads + r) * 3 + 1 + branch
            grow = jnp.sum(jnp.where(rid == col, gates_t, 0.0), axis=0, keepdims=True)
            o = o + acc[:HEAD_DIM, cs] * (grow / acc[HEAD_DIM:HEAD_DIM + 1, cs])
        o_ref[0, :, r * HEAD_DIM:(r + 1) * HEAD_DIM] = o.T.astype(o_ref.dtype)


def _nsa_attn(qkv, ks_blk0, kw_blk0, vt, gates, kc, vct, overlap_t, e_mat, *, g_groups, r_heads, seq):
    bsz = qkv.shape[0]
    qw = r_heads * HEAD_DIM
    tq = tk = vt.shape[-1]
    cols = r_heads * tq
    nc = kc.shape[2]
    kspec = lambda blk0: pl.BlockSpec((1, seq, HEAD_DIM), lambda b, g, i: (b, 0, blk0 + g))
    vspec = lambda h0: pl.BlockSpec((1, 1) + vt.shape[2:], lambda b, g, i: (b, h0 + g, 0, 0, 0))
    qspec = pl.BlockSpec((1, tq, qw), lambda b, g, i: (b, i, g))
    body = functools.partial(_nsa_body, tq=tq, r_heads=r_heads, n_win=(WINDOW + tk - 1) // tk + 1,
                             nb=seq // SEL_LEN)
    return pl.pallas_call(
        body,
        grid=(bsz, g_groups, seq // tq),
        in_specs=[qspec,
                  pl.BlockSpec((1, 1, nc, HEAD_DIM), lambda b, g, i: (b, g, 0, 0)),
                  pl.BlockSpec((1, 1, HEAD_DIM, nc), lambda b, g, i: (b, g, 0, 0)),
                  pl.BlockSpec((LANES, nc), lambda b, g, i: (0, 0)),
                  kspec(ks_blk0), pl.BlockSpec((seq, LANES), lambda b, g, i: (0, 0)), vspec(0),
                  kspec(kw_blk0), vspec(g_groups),
                  pl.BlockSpec((1, tq, LANES), lambda b, g, i: (b, i, 1))],
        out_specs=qspec,
        out_shape=jax.ShapeDtypeStruct((bsz, seq, g_groups * qw), CDT),
        scratch_shapes=[pltpu.VMEM((cols, 2 * HEAD_DIM), CDT),
                        pltpu.VMEM((HEAD_DIM, cols), F32),
                        pltpu.VMEM((tk, cols), F32), pltpu.VMEM((tk, cols), F32),
                        pltpu.VMEM((2, 1, cols), F32),
                        pltpu.VMEM((2, 1, cols), F32),
                        pltpu.VMEM((2, ACC_ROWS, cols), F32)],
        compiler_params=_params(("arbitrary",) * 3),
        name="nsa_attn",
    )(qkv, kc, vct, overlap_t, qkv, e_mat, vt, qkv, vt, gates)


def _mla_body(q_ref, kn_ref, kr_ref, vt_ref, o_ref, *rest, tq, tk, hg):
    s_refs = rest[:hg]
    cm_ref, m_ref, acc_ref = rest[hg:]
    i = pl.program_id(2)
    _flash_init(m_ref, acc_ref)
    n_full = (i * tq) // tk

    def step(j, mask, c0):
        off = pl.multiple_of(j * tk, tk)
        kr = kr_ref[0, pl.ds(off, tk), :]

        def scores(c):
            kmat = jnp.concatenate([kn_ref[0, pl.ds(off, tk), c * MLA_NOPE:(c + 1) * MLA_NOPE], kr], axis=1)
            s = lax.dot_general(kmat, q_ref[0, c0:, c * MLA_QK_PAD:(c + 1) * MLA_QK_PAD],
                                (((1,), (1,)), ((), ())), preferred_element_type=F32)
            if mask is not None:
                s = jnp.where(mask, s, NEG)
            s_refs[c][:, c0:] = s
            cm_ref[c, :, c0:] = jnp.max(s, axis=0, keepdims=True)

        def update(c):
            m_prev = m_ref[c, :, c0:]
            m_new = jnp.maximum(m_prev, cm_ref[c, :, c0:])
            p = jnp.exp2(s_refs[c][:, c0:] - m_new).astype(CDT)
            acc_ref[c, :, c0:] = (jnp.exp2(m_prev - m_new) * acc_ref[c, :, c0:]
                                  + jnp.dot(_with_ones_row(vt_ref[0, c, j]), p, preferred_element_type=F32))
            m_ref[c, :, c0:] = m_new

        scores(0)
        for c in range(1, hg):
            scores(c)
            update(c - 1)
        update(hg - 1)

    def full_tile(j, carry):
        step(j, None, 0)
        return carry

    lax.fori_loop(0, n_full, full_tile, 0)
    for dd in range(tq // tk):
        w = tq - dd * tk
        mask = (lax.broadcasted_iota(jnp.int32, (tk, w), 0) <= lax.broadcasted_iota(jnp.int32, (tk, w), 1))
        step(n_full + dd, mask, dd * tk)
    for c in range(hg):
        acc = acc_ref[c]
        o = acc[:HEAD_DIM] * (1.0 / acc[HEAD_DIM:HEAD_DIM + 1])
        o_ref[0, :, c * MLA_V:(c + 1) * MLA_V] = o.T.astype(o_ref.dtype)


def _mla_attn(q, kv, kr, vt, *, heads, seq):
    bsz = q.shape[0]
    tk = vt.shape[-1]
    tq = _tile(seq, (1024, 512, 256, 128))
    hg = 4 if heads % 4 == 0 else 1
    body = functools.partial(_mla_body, tq=tq, tk=tk, hg=hg)
    return pl.pallas_call(
        body,
        grid=(bsz, heads // hg, seq // tq),
        in_specs=[pl.BlockSpec((1, tq, hg * MLA_QK_PAD), lambda b, h, i: (b, i, h)),
                  pl.BlockSpec((1, seq, hg * MLA_NOPE), lambda b, h, i: (b, 0, h)),
                  pl.BlockSpec((1, seq, LANES), lambda b, h, i: (b, 0, 0)),
                  pl.BlockSpec((1, hg) + vt.shape[2:], lambda b, h, i: (b, h, 0, 0, 0))],
        out_specs=pl.BlockSpec((1, tq, hg * MLA_V), lambda b, h, i: (b, i, h)),
        out_shape=jax.ShapeDtypeStruct((bsz, seq, heads * MLA_V), CDT),
        scratch_shapes=[pltpu.VMEM((tk, tq), F32)] * hg + [
            pltpu.VMEM((hg, 1, tq), F32), pltpu.VMEM((hg, 1, tq), F32), pltpu.VMEM((hg, ACC_ROWS, tq), F32)],
        compiler_params=_params(("arbitrary",) * 3),
        name="mla_attn",
    )(q, kv, kr, vt)


def _rope_tables(positions, rot_dim, period, offset):
    half = rot_dim // 2
    inv = jnp.power(ROPE_THETA, -jnp.arange(0, rot_dim, 2, dtype=F32) / rot_dim)
    ang = positions.astype(F32).reshape(-1)[:, None] * inv
    cos, sin = jnp.cos(ang), jnp.sin(ang)
    t = ang.shape[0]
    ones_l = jnp.ones((t, offset), F32)
    zeros_l = jnp.zeros((t, offset), F32)
    rest = period - offset - rot_dim
    ct = jnp.concatenate([ones_l, cos, cos, jnp.ones((t, rest), F32)], axis=1)
    sa = jnp.concatenate([zeros_l, -sin, jnp.zeros((t, half + rest), F32)], axis=1)
    sb = jnp.concatenate([zeros_l, jnp.zeros((t, half), F32), sin, jnp.zeros((t, rest), F32)], axis=1)
    return ct, sa, sb


def _table_extras(tables, tm):
    return [(tb, pl.BlockSpec((tm, tb.shape[1]), lambda i, j: (i, 0))) for tb in tables]


def _layer(x2, mod, positions, w_in, pos_k, pos_v, k1, k2, v1, v2, q_norm, kv_norm, w_uq, w_ukv,
           w_out, ln1_g, ln1_b, w_ff1, w_ff2, ln2_g, ln2_b, *, bsz, seq, alpha):
    t, d = x2.shape
    n_heads = d // HEAD_DIM
    nsa_h = n_heads // 2
    mla_h = n_heads - nsa_h
    q_rank = q_norm.shape[0]
    kv_rank = kv_norm.shape[0]
    d_in = w_in.shape[1]
    g_groups = (d_in - nsa_h * HEAD_DIM - nsa_h * 3 - q_rank - kv_rank - MLA_ROPE) // (6 * HEAD_DIM)
    r_heads = nsa_h // g_groups
    gw = g_groups * HEAD_DIM
    qw = nsa_h * HEAD_DIM
    sh_a, sc_a, g_a, sh_m, sc_m, g_m = jnp.split(mod, N_ADA, axis=-1)

    o_q, o_kv = 0, qw
    o_gate = o_kv + 6 * gw
    o_cq = o_gate + nsa_h * 3
    o_ckv = o_cq + q_rank
    o_kr = o_ckv + kv_rank
    kvcol = lambda idx: w_in[:, o_kv + idx * gw:o_kv + (idx + 1) * gw]
    w1c = jnp.concatenate([w_in[:, o_q:o_q + qw], kvcol(2), kvcol(4)], axis=1).astype(CDT)
    wvc = jnp.concatenate([kvcol(3), kvcol(5)], axis=1).astype(CDT)
    w2c = jnp.concatenate([w_in[:, o_cq:o_cq + q_rank], w_in[:, o_ckv:o_ckv + kv_rank], kvcol(0),
                           kvcol(1)], axis=1).astype(CDT)
    zpad = lambda n: jnp.zeros((d, n), w_in.dtype)
    w3c = jnp.concatenate([w_in[:, o_kr:o_kr + MLA_ROPE], zpad(LANES - MLA_ROPE),
                           w_in[:, o_gate:o_gate + nsa_h * 3], zpad(LANES - nsa_h * 3)],
                          axis=1).astype(CDT)

    h = _modcast(x2, sc_a, sh_a, seq)

    tm = _tile(t, (1024, 512, 256, 128))
    nsa_tabs = _rope_tables(positions, PARTIAL_ROT, LANES, 0)
    n1 = w1c.shape[1]
    tn1 = _tile(gw, (512, 256, 128))
    scale = HEAD_DIM ** -0.5 * LOG2E
    p1 = _mm([h], w1c, tm=tm, tn=tn1, out_dtype=CDT, name="in_proj_rot",
             epilogue=_rope_epilogue(PARTIAL_ROT // 2, 0, n1 // tn1, qw // tn1, scale),
             extras=_table_extras(nsa_tabs, tm))
    tk = _tile(seq, (512, 256, 128))
    vt_nsa = _mm([h], wvc, tm=tm, tn=_tile(2 * gw, (512, 256, 128)), out_dtype=CDT, name="in_proj_vt",
                 head_tiles=tk, seq=seq)
    n2 = w2c.shape[1]
    lat = q_rank + kv_rank
    tn2 = _tile(math.gcd(lat, gw), (512, 256, 128))
    p2 = _mm([h], w2c, tm=tm, tn=tn2, out_dtype=F32, name="in_proj_lat",
             epilogue=_rope_epilogue(PARTIAL_ROT // 2, lat // tn2, (lat + gw) // tn2, 0, 1.0),
             extras=_table_extras(nsa_tabs, tm))
    kr_tabs = _rope_tables(positions, MLA_ROPE, LANES, 0)
    p3 = _mm([h], w3c, tm=tm, tn=2 * LANES, out_dtype=F32, name="in_proj_kr_gate",
             epilogue=_krope_gate_epilogue, extras=_table_extras(kr_tabs, tm))

    qkv = p1.reshape(bsz, seq, n1)
    gates = p3.reshape(bsz, seq, 2 * LANES)

    nchunk = seq // CMP_STRIDE
    ncmp = (seq - CMP_LEN) // CMP_STRIDE + 1
    blk = lambda col: col // HEAD_DIM
    kc = _compress(p2, blk(lat), pos_k, k1, k2, bsz=bsz, seq=seq, g_groups=g_groups)
    vc = _compress(p2, blk(lat + gw), pos_v, v1, v2, bsz=bsz, seq=seq, g_groups=g_groups)

    nb = seq // SEL_LEN
    c_start = np.arange(nchunk) * CMP_STRIDE
    b_start = np.arange(LANES) * SEL_LEN
    overlap = ((c_start[:, None] < b_start[None, :] + SEL_LEN) &
               (c_start[:, None] + CMP_LEN > b_start[None, :]) &
               (np.arange(nchunk)[:, None] < ncmp) & (np.arange(LANES)[None, :] < nb))
    overlap_t = jnp.asarray(overlap.T.astype(np.float32)).astype(CDT)
    e_mat = jnp.asarray((np.arange(seq)[:, None] // SEL_LEN == np.arange(LANES)[None, :])
                        .astype(np.float32)).astype(CDT)
    o_nsa = _nsa_attn(qkv, blk(qw), blk(qw + gw), vt_nsa, gates, kc, vc.transpose(0, 1, 3, 2),
                      overlap_t, e_mat, g_groups=g_groups, r_heads=r_heads, seq=seq)

    qk_dim = MLA_NOPE + MLA_ROPE
    wq = w_uq.reshape(q_rank, mla_h, qk_dim)
    wq = jnp.concatenate([wq, jnp.zeros((q_rank, mla_h, MLA_QK_PAD - qk_dim), wq.dtype)], axis=-1)
    wq = wq.reshape(q_rank, mla_h * MLA_QK_PAD).astype(CDT)
    wkv = w_ukv.reshape(kv_rank, mla_h, MLA_NOPE + MLA_V)
    wk = wkv[:, :, :MLA_NOPE].reshape(kv_rank, mla_h * MLA_NOPE).astype(CDT)
    wv = wkv[:, :, MLA_NOPE:].reshape(kv_rank, mla_h * MLA_V).astype(CDT)
    tnq = _tile(mla_h * MLA_QK_PAD, (512, 256))
    q_mla = _mm([p2], wq, tm=tm, tn=tnq, out_dtype=CDT, name="mla_uq",
                a_specs=[pl.BlockSpec((tm, q_rank), lambda i, j: (i, 0))],
                prologue=_rms_prologue(1e-6),
                epilogue=_mla_q_epilogue(qk_dim ** -0.5 * LOG2E),
                extras=_table_extras(kr_tabs, tm)
                + [(q_norm.reshape(1, q_rank), pl.BlockSpec((1, q_rank), lambda i, j: (0, 0)))])
    tnkv = _tile(wk.shape[1], (512, 256, 128))
    ckv_args = dict(tm=tm, tn=tnkv, out_dtype=CDT, prologue=_rms_prologue(1e-6),
                    a_specs=[pl.BlockSpec((tm, kv_rank), lambda i, j: (i, q_rank // kv_rank))],
                    extras=[(kv_norm.reshape(1, kv_rank), pl.BlockSpec((1, kv_rank), lambda i, j: (0, 0)))])
    k_mla = _mm([p2], wk, name="mla_uk", **ckv_args)
    vt_mla = _mm([p2], wv, name="mla_uv", head_tiles=tk, seq=seq, **ckv_args)
    o_mla = _mla_attn(q_mla.reshape(bsz, seq, -1), k_mla.reshape(bsz, seq, -1),
                      gates[:, :, :LANES].astype(CDT), vt_mla, heads=mla_h, seq=seq)

    tno = _tile(d, (1024, 512, 256, 128))
    a = _mm([o_nsa.reshape(t, qw), o_mla.reshape(t, mla_h * MLA_V)], w_out.astype(CDT), tm=tm, tn=tno,
            out_dtype=F32, name="out_proj")
    x1, h2 = _res_ln(x2, a, g_a, ln1_g, ln1_b, seq, alpha, mod=(sc_m, sh_m))
    d_ff = w_ff1.shape[1]
    f1 = _mm([h2], w_ff1.astype(CDT), tm=tm, tn=_tile(d_ff, (1024, 512, 256, 128)), out_dtype=CDT,
             name="ff1", epilogue=_relu2_epilogue)
    f2 = _mmk(f1, w_ff2.astype(CDT), tm=tm, tn=tno, tk=_tile(d_ff, (4096, 2048, 1024, 512)),
              out_dtype=F32, name="ff2")
    return _res_ln(x1, f2, g_m, ln2_g, ln2_b, seq, alpha)


def kernel(x, c, positions, w_ada, b_ada, w_in, nsa_pos_k, nsa_pos_v, nsa_cmp_k1, nsa_cmp_k2, nsa_cmp_v1, nsa_cmp_v2, mla_q_norm, mla_kv_norm, mla_w_uq, mla_w_ukv, w_out, ln1_g, ln1_b, w_ff1, w_ff2, ln2_g, ln2_b):
    bsz, seq, d = x.shape
    depth = w_ada.shape[0]
    alpha = (2 * depth) ** 0.25
    x2 = x.reshape(bsz * seq, d)
    for layer in range(depth):
        mod = _ada(c, w_ada[layer], b_ada[layer])
        x2 = _layer(x2, mod, positions, w_in[layer], nsa_pos_k[layer], nsa_pos_v[layer],
                    nsa_cmp_k1[layer], nsa_cmp_k2[layer], nsa_cmp_v1[layer], nsa_cmp_v2[layer],
                    mla_q_norm[layer], mla_kv_norm[layer], mla_w_uq[layer], mla_w_ukv[layer],
                    w_out[layer], ln1_g[layer], ln1_b[layer], w_ff1[layer], w_ff2[layer],
                    ln2_g[layer], ln2_b[layer], bsz=bsz, seq=seq, alpha=alpha)
    return x2.reshape(bsz, seq, d)
```

```python
import functools
import math

import numpy as np
import jax
import jax.numpy as jnp
from jax import lax
from jax.experimental import pallas as pl
from jax.experimental.pallas import tpu as pltpu

HEAD_DIM = 128
CMP_LEN = 32
CMP_STRIDE = 16
SEL_LEN = 64
N_SEL = 16
N_LOCAL_FORCED = 2
FORCED_BONUS = 1e4
WINDOW = 512
MLA_NOPE = 128
MLA_ROPE = 64
MLA_V = 128
ROPE_THETA = 500000.0
PARTIAL_ROT = HEAD_DIM // 4
NEG = -1e30
LOG2E = math.log2(math.e)
N_ADA = 6
LANES = 128
MLA_QK_PAD = 2 * LANES
VMEM_LIMIT = 56 * 1024 * 1024
MM_SUB_COLS = 256

F32 = jnp.float32
CDT = jnp.bfloat16


def _params(sem):
    return pltpu.CompilerParams(dimension_semantics=sem, vmem_limit_bytes=VMEM_LIMIT)


def _tile(n, cands):
    for c in cands:
        if n % c == 0:
            return c
    return n


def _ada_body(c_ref, w_ref, b_ref, o_ref):
    c = c_ref[...]
    cond = c * (1.0 / (1.0 + jnp.exp(-c)))
    o_ref[...] = jnp.dot(cond.astype(CDT), w_ref[...].astype(CDT),
                         preferred_element_type=F32) + b_ref[...]


def _ada(c, w, b):
    bsz, d = c.shape
    n = w.shape[1]
    tn = _tile(n, (512, 256, 128))
    return pl.pallas_call(
        _ada_body,
        grid=(n // tn,),
        in_specs=[pl.BlockSpec((bsz, d), lambda j: (0, 0)),
                  pl.BlockSpec((d, tn), lambda j: (0, j)),
                  pl.BlockSpec((1, tn), lambda j: (0, j))],
        out_specs=pl.BlockSpec((bsz, tn), lambda j: (0, j)),
        out_shape=jax.ShapeDtypeStruct((bsz, n), F32),
        compiler_params=_params(("arbitrary",)),
        name="ada",
    )(c, w, b.reshape(1, n))


def _modcast_body(x_ref, sc_ref, sh_ref, o_ref):
    o_ref[...] = (x_ref[...] * (1.0 + sc_ref[0]) + sh_ref[0]).astype(o_ref.dtype)


def _modcast(x2, sc, sh, seq):
    t, d = x2.shape
    ts = _tile(seq, (256, 128, 64, 8))
    per = seq // ts
    vec = pl.BlockSpec((1, 1, d), lambda i: (i // per, 0, 0))
    return pl.pallas_call(
        _modcast_body,
        grid=(t // ts,),
        in_specs=[pl.BlockSpec((ts, d), lambda i: (i, 0)), vec, vec],
        out_specs=pl.BlockSpec((ts, d), lambda i: (i, 0)),
        out_shape=jax.ShapeDtypeStruct((t, d), CDT),
        compiler_params=_params(("arbitrary",)),
        name="modcast",
    )(x2, sc[:, None, :], sh[:, None, :])


def _mm_body(*refs, na, nex, prologue, epilogue, tn, head_tiles):
    a_refs = refs[:na]
    b_ref = refs[na]
    ex = refs[na + 1:na + 1 + nex]
    o_ref = refs[na + 1 + nex]
    j = pl.program_id(1)
    if prologue is not None:
        a_sc = refs[na + 2 + nex]

        @pl.when(j == 0)
        def _():
            a_sc[...] = prologue(a_refs[0][...], ex).astype(a_sc.dtype)

        a_refs = (a_sc,)
    sub = MM_SUB_COLS if tn % MM_SUB_COLS == 0 else tn

    def store(s, y):
        if head_tiles is None:
            o_ref[:, s * sub:(s + 1) * sub] = y.astype(o_ref.dtype)
        else:
            for hh in range(sub // HEAD_DIM):
                for tt in range(y.shape[0] // head_tiles):
                    blk = y[tt * head_tiles:(tt + 1) * head_tiles, hh * HEAD_DIM:(hh + 1) * HEAD_DIM]
                    o_ref[0, s * (sub // HEAD_DIM) + hh, tt] = blk.T.astype(o_ref.dtype)

    def run(fn):
        for s in range(tn // sub):
            cols = slice(s * sub, (s + 1) * sub)
            acc = None
            off = 0
            for r in a_refs:
                kr = r.shape[1]
                part = jnp.dot(r[...], b_ref[off:off + kr, cols], preferred_element_type=F32)
                acc = part if acc is None else acc + part
                off += kr
            store(s, fn(acc, j, ex))

    if epilogue is None:
        run(lambda acc, j, ex: acc)
    else:
        for cond, fn in epilogue(j):
            pl.when(cond)(functools.partial(run, fn))


def _mm(a_list, b, *, tm, tn, out_dtype, name, a_specs=None, prologue=None, epilogue=None,
        extras=(), head_tiles=None, seq=None):
    m = a_list[0].shape[0]
    k, n = b.shape
    if a_specs is None:
        a_specs = [pl.BlockSpec((tm, a.shape[1]), lambda i, j: (i, 0)) for a in a_list]
    ex_arrays = [e[0] for e in extras]
    ex_specs = [e[1] for e in extras]
    scratch = [pltpu.VMEM((tm, k), CDT)] if prologue is not None else []
    body = functools.partial(_mm_body, na=len(a_list), nex=len(extras), prologue=prologue,
                             epilogue=epilogue, tn=tn, head_tiles=head_tiles)
    if head_tiles is None:
        out_specs = pl.BlockSpec((tm, tn), lambda i, j: (i, j))
        out_shape = jax.ShapeDtypeStruct((m, n), out_dtype)
    else:
        per = seq // tm
        out_specs = pl.BlockSpec((1, tn // HEAD_DIM, tm // head_tiles, HEAD_DIM, head_tiles),
                                 lambda i, j: (i // per, j, i % per, 0, 0))
        out_shape = jax.ShapeDtypeStruct((m // seq, n // HEAD_DIM, seq // head_tiles, HEAD_DIM, head_tiles),
                                         out_dtype)
    return pl.pallas_call(
        body,
        grid=(m // tm, n // tn),
        in_specs=a_specs + [pl.BlockSpec((k, tn), lambda i, j: (0, j))] + ex_specs,
        out_specs=out_specs,
        out_shape=out_shape,
        scratch_shapes=scratch,
        compiler_params=_params(("arbitrary", "arbitrary")),
        name=name,
    )(*a_list, b, *ex_arrays)


def _rot(acc, cos, sa, sb, shift):
    tn = acc.shape[1]
    reps = tn // cos.shape[1]

    def rep(t):
        return t if reps == 1 else jnp.concatenate([t] * reps, axis=1)

    return (acc * rep(cos) + pltpu.roll(acc, tn - shift, 1) * rep(sa)
            + pltpu.roll(acc, shift, 1) * rep(sb))


def _rope_epilogue(shift, lo, hi, n_scaled, scale):
    def roped(acc, j, ex):
        y = _rot(acc, ex[0][...], ex[1][...], ex[2][...], shift)
        return y * jnp.where(j < n_scaled, scale, 1.0).astype(F32)

    def epi(j):
        inside = (j >= lo) & (j < hi)
        return [(inside, roped), (jnp.logical_not(inside), lambda acc, j, ex: acc)]

    return epi


def _mla_q_epilogue(scale):
    assert MM_SUB_COLS == MLA_QK_PAD

    def fn(acc, j, ex):
        rot = _rot(acc[:, LANES:], ex[0][...], ex[1][...], ex[2][...], MLA_ROPE // 2)
        return jnp.concatenate([acc[:, :LANES], rot], axis=1) * scale

    return lambda j: [(j >= 0, fn)]


def _krope_gate_epilogue(j):
    def fn(acc, j, ex):
        y = _rot(acc, ex[0][...], ex[1][...], ex[2][...], MLA_ROPE // 2)
        lane = lax.broadcasted_iota(jnp.int32, acc.shape, 1)
        return jnp.where(lane < LANES, y, 1.0 / (1.0 + jnp.exp(-acc)))

    return [(j >= 0, fn)]


def _rms_prologue(eps):
    def pro(a, ex):
        g = ex[-1][...]
        y = a * lax.rsqrt(jnp.mean(a * a, axis=-1, keepdims=True) + eps)
        return y * g

    return pro


def _relu2_epilogue(j):
    def fn(acc, j, ex):
        r = jnp.maximum(acc, 0.0)
        return r * r

    return [(j >= 0, fn)]


def _mmk_body(a_ref, b_ref, o_ref, acc_ref):
    kk = pl.program_id(2)

    @pl.when(kk == 0)
    def _():
        acc_ref[...] = jnp.zeros_like(acc_ref)

    acc_ref[...] += jnp.dot(a_ref[...], b_ref[...], preferred_element_type=F32)

    @pl.when(kk == pl.num_programs(2) - 1)
    def _():
        o_ref[...] = acc_ref[...].astype(o_ref.dtype)


def _mmk(a, b, *, tm, tn, tk, out_dtype, name):
    m, k = a.shape
    n = b.shape[1]
    return pl.pallas_call(
        _mmk_body,
        grid=(m // tm, n // tn, k // tk),
        in_specs=[pl.BlockSpec((tm, tk), lambda i, j, kk: (i, kk)),
                  pl.BlockSpec((tk, tn), lambda i, j, kk: (kk, j))],
        out_specs=pl.BlockSpec((tm, tn), lambda i, j, kk: (i, j)),
        out_shape=jax.ShapeDtypeStruct((m, n), out_dtype),
        scratch_shapes=[pltpu.VMEM((tm, tn), F32)],
        compiler_params=_params(("arbitrary", "arbitrary", "arbitrary")),
        name=name,
    )(a, b)


def _ln_body(x_ref, a_ref, gate_ref, g_ref, b_ref, *rest, alpha, with_mod):
    y = alpha * x_ref[...] + (1.0 + gate_ref[0]) * a_ref[...]
    mu = jnp.mean(y, axis=-1, keepdims=True)
    yc = y - mu
    var = jnp.mean(yc * yc, axis=-1, keepdims=True)
    out = yc * lax.rsqrt(var + 1e-5) * g_ref[...] + b_ref[...]
    if with_mod:
        sc_ref, sh_ref, o_ref, h_ref = rest
        o_ref[...] = out
        h_ref[...] = (out * (1.0 + sc_ref[0]) + sh_ref[0]).astype(h_ref.dtype)
    else:
        (o_ref,) = rest
        o_ref[...] = out


def _res_ln(x2, a2, gate, g, b, seq, alpha, mod=None):
    t, d = x2.shape
    ts = _tile(seq, (256, 128, 64, 8))
    per = seq // ts
    row = pl.BlockSpec((ts, d), lambda i: (i, 0))
    vec = pl.BlockSpec((1, 1, d), lambda i: (i // per, 0, 0))
    par = pl.BlockSpec((1, d), lambda i: (0, 0))
    ins = [x2, a2, gate[:, None, :], g.reshape(1, d), b.reshape(1, d)]
    specs = [row, row, vec, par, par]
    if mod is None:
        out_shape = jax.ShapeDtypeStruct((t, d), F32)
        out_specs = row
    else:
        ins += [mod[0][:, None, :], mod[1][:, None, :]]
        specs += [vec, vec]
        out_shape = (jax.ShapeDtypeStruct((t, d), F32), jax.ShapeDtypeStruct((t, d), CDT))
        out_specs = (row, row)
    return pl.pallas_call(
        functools.partial(_ln_body, alpha=alpha, with_mod=mod is not None),
        grid=(t // ts,),
        in_specs=specs,
        out_specs=out_specs,
        out_shape=out_shape,
        compiler_params=_params(("arbitrary",)),
        name="res_ln",
    )(*ins)


def _compress_body(x_ref, pos_ref, w1_ref, w2_ref, o_ref):
    nc = o_ref.shape[2]
    acc_a = None
    acc_b = None
    for l in range(CMP_STRIDE):
        x = x_ref[pl.ds(l, nc, stride=CMP_STRIDE), :]
        xa = (x + pos_ref[l:l + 1, :]).astype(CDT)
        xb = (x + pos_ref[CMP_STRIDE + l:CMP_STRIDE + l + 1, :]).astype(CDT)
        wa = w1_ref[l * HEAD_DIM:(l + 1) * HEAD_DIM, :]
        wb = w1_ref[(CMP_STRIDE + l) * HEAD_DIM:(CMP_STRIDE + l + 1) * HEAD_DIM, :]
        pa = jnp.dot(xa, wa, preferred_element_type=F32)
        pb = jnp.dot(xb, wb, preferred_element_type=F32)
        acc_a = pa if acc_a is None else acc_a + pa
        acc_b = pb if acc_b is None else acc_b + pb
    pre = acc_a + pltpu.roll(acc_b, nc - 1, 0)
    hid = 0.5 * pre * (1.0 + jnp.tanh(math.sqrt(2.0 / math.pi) * (pre + 0.044715 * pre * pre * pre)))
    o_ref[0, 0] = jnp.dot(hid.astype(CDT), w2_ref[...], preferred_element_type=F32).astype(o_ref.dtype)


def _compress(x2, col_blk0, pos, w1, w2, *, bsz, seq, g_groups):
    nc = seq // CMP_STRIDE
    full = lambda shape: pl.BlockSpec(shape, lambda b, gi: (0,) * len(shape))
    return pl.pallas_call(
        _compress_body,
        grid=(bsz, g_groups),
        in_specs=[pl.BlockSpec((seq, HEAD_DIM), lambda b, gi: (b, col_blk0 + gi)),
                  full(pos.shape), full(w1.shape), full(w2.shape)],
        out_specs=pl.BlockSpec((1, 1, nc, HEAD_DIM), lambda b, gi: (b, gi, 0, 0)),
        out_shape=jax.ShapeDtypeStruct((bsz, g_groups, nc, HEAD_DIM), CDT),
        compiler_params=_params(("arbitrary", "arbitrary")),
        name="compress",
    )(x2, pos, w1.astype(CDT), w2.astype(CDT))


def _cmp_branch(qs, kc, vct, ovt, gates_t, gi, i, *, tq, r_heads, nb):
    nc = kc.shape[0]
    cols = r_heads * tq
    s = lax.dot_general(kc, qs, (((1,), (1,)), ((), ())), preferred_element_type=F32)
    cidx = lax.broadcasted_iota(jnp.int32, (nc, cols), 0)
    tok = i * tq + (lax.broadcasted_iota(jnp.int32, (nc, cols), 1) & (tq - 1))
    valid = cidx * CMP_STRIDE + (CMP_LEN - 1) <= tok
    s = jnp.where(valid, s, NEG)
    e = jnp.exp2(s - jnp.max(s, axis=0, keepdims=True))
    p = jnp.where(valid, e, 0.0) * (1.0 / jnp.sum(e, axis=0, keepdims=True))
    o_t = jnp.dot(vct, p.astype(CDT), preferred_element_type=F32)
    rid = lax.broadcasted_iota(jnp.int32, gates_t.shape, 0)
    psum = None
    gated = []
    for r in range(r_heads):
        cs = slice(r * tq, (r + 1) * tq)
        col = (gi * r_heads + r) * 3
        grow = jnp.sum(jnp.where(rid == col, gates_t, 0.0), axis=0, keepdims=True)
        gated.append(o_t[:, cs] * grow)
        psum = p[:, cs] if psum is None else psum + p[:, cs]

    p_hi = psum.astype(CDT)
    p_lo = (psum - p_hi.astype(F32)).astype(CDT)
    imp = (jnp.dot(ovt, p_hi, preferred_element_type=F32)
           + jnp.dot(ovt, p_lo, preferred_element_type=F32))
    jj = lax.broadcasted_iota(jnp.int32, (LANES, tq), 0)
    cur = (i * tq + lax.broadcasted_iota(jnp.int32, (LANES, tq), 1)) // SEL_LEN
    forced = (jj == 0) | ((jj <= cur) & (jj > cur - N_LOCAL_FORCED))
    imp = jnp.where(jj > cur, NEG, imp + jnp.where(forced, FORCED_BONUS, 0.0))
    v = imp[:nb]
    ridx = lax.broadcasted_iota(jnp.int32, (nb, tq), 0)
    sel = jnp.zeros((nb, tq), F32)
    for _ in range(min(N_SEL, nb)):
        mx = jnp.max(v, axis=0, keepdims=True)
        first = jnp.min(jnp.where(v == mx, ridx, nb), axis=0, keepdims=True)
        hit = ridx == first
        sel = jnp.where(hit, 1.0, sel)
        v = jnp.where(hit, -jnp.inf, v)
    bias_t = jnp.where(sel > 0.0, 0.0, NEG)
    if nb < LANES:
        bias_t = jnp.concatenate([bias_t, jnp.zeros((LANES - nb, tq), F32)], axis=0)
    return jnp.concatenate(gated, axis=1), bias_t.T.astype(CDT)


ACC_ROWS = HEAD_DIM + 16


def _with_ones_row(vt):
    ones_row = jnp.where(lax.broadcasted_iota(jnp.int32, (ACC_ROWS - HEAD_DIM, vt.shape[1]), 0) == 0,
                         1.0, 0.0).astype(CDT)
    return jnp.concatenate([vt, ones_row], axis=0)


def _flash_init(m_ref, acc_ref):
    m_ref[...] = jnp.full(m_ref.shape, -jnp.inf, F32)
    acc_ref[...] = jnp.zeros(acc_ref.shape, F32)


def _nsa_body(q_ref, kc_ref, vct_ref, ovt_ref, ks_ref, e_ref, vs_ref, kw_ref, vw_ref, gate_ref, o_ref,
              qc_ref, oc_ref, s0_ref, s1_ref, cm_ref, m_ref, acc_ref, *, tq, r_heads, n_win, nb):
    s_refs = (s0_ref, s1_ref)
    gi = pl.program_id(1)
    i = pl.program_id(2)
    tk = tq
    cols = r_heads * tq
    for r in range(r_heads):
        qc_ref[r * tq:(r + 1) * tq, :HEAD_DIM] = q_ref[0, :, r * HEAD_DIM:(r + 1) * HEAD_DIM]
    _flash_init(m_ref, acc_ref)

    def local_iotas():
        key = lax.broadcasted_iota(jnp.int32, (tk, cols), 0)
        tok = lax.broadcasted_iota(jnp.int32, (tk, cols), 1) & (tq - 1)
        return key, tok

    def put(s, buf):
        s_refs[buf][...] = s
        cm_ref[buf] = jnp.max(s, axis=0, keepdims=True)

    def update(branch, vt, buf):
        m_prev = m_ref[branch]
        m_new = jnp.maximum(m_prev, cm_ref[buf])
        p = jnp.exp2(s_refs[buf][...] - m_new).astype(CDT)
        acc_ref[branch] = (jnp.exp2(m_prev - m_new) * acc_ref[branch]
                           + jnp.dot(_with_ones_row(vt), p, preferred_element_type=F32))
        m_ref[branch] = m_new

    def sel_scores(j, buf, masked):
        off = pl.multiple_of(j * tk, tk)
        kmat = jnp.concatenate([ks_ref[0, pl.ds(off, tk), :], e_ref[pl.ds(off, tk), :]], axis=1)
        s = lax.dot_general(kmat, qc_ref[...], (((1,), (1,)), ((), ())), preferred_element_type=F32)
        if masked:
            key, tok = local_iotas()
            if masked == "if_diagonal":
                tok = tok + jnp.where(j == i, 0, tk)
            s = jnp.where(key <= tok, s, NEG)
        put(s, buf)

    def sel_update(j, buf):
        update(0, vs_ref[0, 0, j], buf)

    def win_tile(dd):
        back = n_win - 1 - dd
        return back, jnp.maximum(i - back, 0)

    def win_scores(dd, buf):
        back, j = win_tile(dd)
        off = pl.multiple_of(j * tk, tk)
        s = lax.dot_general(kw_ref[0, pl.ds(off, tk), :], qc_ref[:, :HEAD_DIM], (((1,), (1,)), ((), ())),
                            preferred_element_type=F32)
        key, tok = local_iotas()
        if back * tk - (tk - 1) < 0:
            s = jnp.where(key - back * tk <= tok, s, NEG)
        if back > 0 or tq - 1 + back * tk >= WINDOW:
            reach = WINDOW - back * tk
            if back > 0:
                reach = reach - jnp.where(i >= back, 0, 2 * WINDOW + tq)
            s = jnp.where(tok < key + reach, s, NEG)
        put(s, buf)

    def win_update(dd, buf):
        update(1, vw_ref[0, 0, win_tile(dd)[1]], buf)

    win_scores(0, 0)
    if n_win > 1:
        win_scores(1, 1)
    oc_ref[...], bias = _cmp_branch(qc_ref[:, :HEAD_DIM], kc_ref[0, 0], vct_ref[0, 0], ovt_ref[...],
                                    gate_ref[0].T, gi, i, tq=tq, r_heads=r_heads, nb=nb)
    for r in range(r_heads):
        qc_ref[r * tq:(r + 1) * tq, HEAD_DIM:] = bias
    for dd in range(n_win):
        if dd == n_win - 1 and n_win % 2 == 0:
            sel_scores(0, 0, "if_diagonal")
        win_update(dd, dd % 2)
        if dd + 2 < n_win:
            win_scores(dd + 2, dd % 2)
    if n_win % 2 == 1:
        sel_scores(0, 0, "if_diagonal")

    n_pairs = jnp.maximum(i - 1, 0) // 2

    def body(jp, carry):
        j = 2 * jp
        sel_scores(j + 1, 1, False)
        sel_update(j, 0)
        sel_scores(j + 2, 0, False)
        sel_update(j + 1, 1)
        return carry

    lax.fori_loop(0, n_pairs, body, 0)
    j0 = 2 * n_pairs
    left = i - j0

    @pl.when(left == 1)
    def _():
        sel_scores(i, 1, True)
        sel_update(j0, 0)
        sel_update(i, 1)

    @pl.when(left == 2)
    def _():
        sel_scores(j0 + 1, 1, False)
        sel_update(j0, 0)
        sel_scores(i, 0, True)
        sel_update(j0 + 1, 1)
        sel_update(i, 0)

    @pl.when(left == 0)
    def _():
        sel_update(0, 0)

    gates_t = gate_ref[0].T
    rid = lax.broadcasted_iota(jnp.int32, gates_t.shape, 0)
    for r in range(r_heads):
        cs = slice(r * tq, (r + 1) * tq)
        o = oc_ref[:, cs]
        for branch in (0, 1):
            acc = acc_ref[branch]
            col = (gi * r_heads + r) * 3 + 1 + branch
            grow = jnp.sum(jnp.where(rid == col, gates_t, 0.0), axis=0, keepdims=True)
            o = o + acc[:HEAD_DIM, cs] * (grow / acc[HEAD_DIM:HEAD_DIM + 1, cs])
        o_ref[0, :, r * HEAD_DIM:(r + 1) * HEAD_DIM] = o.T.astype(o_ref.dtype)


def _nsa_attn(qkv, ks_blk0, kw_blk0, vt, gates, kc, vct, overlap_t, e_mat, *, g_groups, r_heads, seq):
    bsz = qkv.shape[0]
    qw = r_heads * HEAD_DIM
    tq = tk = vt.shape[-1]
    cols = r_heads * tq
    nc = kc.shape[2]
    kspec = lambda blk0: pl.BlockSpec((1, seq, HEAD_DIM), lambda b, g, i: (b, 0, blk0 + g))
    vspec = lambda h0: pl.BlockSpec((1, 1) + vt.shape[2:], lambda b, g, i: (b, h0 + g, 0, 0, 0))
    qspec = pl.BlockSpec((1, tq, qw), lambda b, g, i: (b, i, g))
    body = functools.partial(_nsa_body, tq=tq, r_heads=r_heads, n_win=(WINDOW + tk - 1) // tk + 1,
                             nb=seq // SEL_LEN)
    return pl.pallas_call(
        body,
        grid=(bsz, g_groups, seq // tq),
        in_specs=[qspec,
                  pl.BlockSpec((1, 1, nc, HEAD_DIM), lambda b, g, i: (b, g, 0, 0)),
                  pl.BlockSpec((1, 1, HEAD_DIM, nc), lambda b, g, i: (b, g, 0, 0)),
                  pl.BlockSpec((LANES, nc), lambda b, g, i: (0, 0)),
                  kspec(ks_blk0), pl.BlockSpec((seq, LANES), lambda b, g, i: (0, 0)), vspec(0),
                  kspec(kw_blk0), vspec(g_groups),
                  pl.BlockSpec((1, tq, LANES), lambda b, g, i: (b, i, 1))],
        out_specs=qspec,
        out_shape=jax.ShapeDtypeStruct((bsz, seq, g_groups * qw), CDT),
        scratch_shapes=[pltpu.VMEM((cols, 2 * HEAD_DIM), CDT),
                        pltpu.VMEM((HEAD_DIM, cols), F32),
                        pltpu.VMEM((tk, cols), F32), pltpu.VMEM((tk, cols), F32),
                        pltpu.VMEM((2, 1, cols), F32),
                        pltpu.VMEM((2, 1, cols), F32),
                        pltpu.VMEM((2, ACC_ROWS, cols), F32)],
        compiler_params=_params(("arbitrary",) * 3),
        name="nsa_attn",
    )(qkv, kc, vct, overlap_t, qkv, e_mat, vt, qkv, vt, gates)


def _mla_body(q_ref, kn_ref, kr_ref, vt_ref, o_ref, *rest, tq, tk, hg):
    s_refs = rest[:hg]
    cm_ref, m_ref, acc_ref = rest[hg:]
    i = pl.program_id(2)
    _flash_init(m_ref, acc_ref)
    n_full = (i * tq) // tk

    def step(j, mask, c0):
        off = pl.multiple_of(j * tk, tk)
        kr = kr_ref[0, pl.ds(off, tk), :]

        def scores(c):
            kmat = jnp.concatenate([kn_ref[0, pl.ds(off, tk), c * MLA_NOPE:(c + 1) * MLA_NOPE], kr], axis=1)
            s = lax.dot_general(kmat, q_ref[0, c0:, c * MLA_QK_PAD:(c + 1) * MLA_QK_PAD],
                                (((1,), (1,)), ((), ())), preferred_element_type=F32)
            if mask is not None:
                s = jnp.where(mask, s, NEG)
            s_refs[c][:, c0:] = s
            cm_ref[c, :, c0:] = jnp.max(s, axis=0, keepdims=True)

        def update(c):
            m_prev = m_ref[c, :, c0:]
            m_new = jnp.maximum(m_prev, cm_ref[c, :, c0:])
            p = jnp.exp2(s_refs[c][:, c0:] - m_new).astype(CDT)
            acc_ref[c, :, c0:] = (jnp.exp2(m_prev - m_new) * acc_ref[c, :, c0:]
                                  + jnp.dot(_with_ones_row(vt_ref[0, c, j]), p, preferred_element_type=F32))
            m_ref[c, :, c0:] = m_new

        scores(0)
        for c in range(1, hg):
            scores(c)
            update(c - 1)
        update(hg - 1)

    def full_tile(j, carry):
        step(j, None, 0)
        return carry

    lax.fori_loop(0, n_full, full_tile, 0)
    for dd in range(tq // tk):
        w = tq - dd * tk
        mask = (lax.broadcasted_iota(jnp.int32, (tk, w), 0) <= lax.broadcasted_iota(jnp.int32, (tk, w), 1))
        step(n_full + dd, mask, dd * tk)
    for c in range(hg):
        acc = acc_ref[c]
        o = acc[:HEAD_DIM] * (1.0 / acc[HEAD_DIM:HEAD_DIM + 1])
        o_ref[0, :, c * MLA_V:(c + 1) * MLA_V] = o.T.astype(o_ref.dtype)


def _mla_attn(q, kv, kr, vt, *, heads, seq):
    bsz = q.shape[0]
    tk = vt.shape[-1]
    tq = _tile(seq, (1024, 512, 256, 128))
    hg = 4 if heads % 4 == 0 else 1
    body = functools.partial(_mla_body, tq=tq, tk=tk, hg=hg)
    return pl.pallas_call(
        body,
        grid=(bsz, heads // hg, seq // tq),
        in_specs=[pl.BlockSpec((1, tq, hg * MLA_QK_PAD), lambda b, h, i: (b, i, h)),
                  pl.BlockSpec((1, seq, hg * MLA_NOPE), lambda b, h, i: (b, 0, h)),
                  pl.BlockSpec((1, seq, LANES), lambda b, h, i: (b, 0, 0)),
                  pl.BlockSpec((1, hg) + vt.shape[2:], lambda b, h, i: (b, h, 0, 0, 0))],
        out_specs=pl.BlockSpec((1, tq, hg * MLA_V), lambda b, h, i: (b, i, h)),
        out_shape=jax.ShapeDtypeStruct((bsz, seq, heads * MLA_V), CDT),
        scratch_shapes=[pltpu.VMEM((tk, tq), F32)] * hg + [
            pltpu.VMEM((hg, 1, tq), F32), pltpu.VMEM((hg, 1, tq), F32), pltpu.VMEM((hg, ACC_ROWS, tq), F32)],
        compiler_params=_params(("arbitrary",) * 3),
        name="mla_attn",
    )(q, kv, kr, vt)


def _rope_tables(positions, rot_dim, period, offset):
    half = rot_dim // 2
    inv = jnp.power(ROPE_THETA, -jnp.arange(0, rot_dim, 2, dtype=F32) / rot_dim)
    ang = positions.astype(F32).reshape(-1)[:, None] * inv
    cos, sin = jnp.cos(ang), jnp.sin(ang)
    t = ang.shape[0]
    ones_l = jnp.ones((t, offset), F32)
    zeros_l = jnp.zeros((t, offset), F32)
    rest = period - offset - rot_dim
    ct = jnp.concatenate([ones_l, cos, cos, jnp.ones((t, rest), F32)], axis=1)
    sa = jnp.concatenate([zeros_l, -sin, jnp.zeros((t, half + rest), F32)], axis=1)
    sb = jnp.concatenate([zeros_l, jnp.zeros((t, half), F32), sin, jnp.zeros((t, rest), F32)], axis=1)
    return ct, sa, sb


def _table_extras(tables, tm):
    return [(tb, pl.BlockSpec((tm, tb.shape[1]), lambda i, j: (i, 0))) for tb in tables]


def _layer(x2, mod, positions, w_in, pos_k, pos_v, k1, k2, v1, v2, q_norm, kv_norm, w_uq, w_ukv,
           w_out, ln1_g, ln1_b, w_ff1, w_ff2, ln2_g, ln2_b, *, bsz, seq, alpha):
    t, d = x2.shape
    n_heads = d // HEAD_DIM
    nsa_h = n_heads // 2
    mla_h = n_heads - nsa_h
    q_rank = q_norm.shape[0]
    kv_rank = kv_norm.shape[0]
    d_in = w_in.shape[1]
    g_groups = (d_in - nsa_h * HEAD_DIM - nsa_h * 3 - q_rank - kv_rank - MLA_ROPE) // (6 * HEAD_DIM)
    r_heads = nsa_h // g_groups
    gw = g_groups * HEAD_DIM
    qw = nsa_h * HEAD_DIM
    sh_a, sc_a, g_a, sh_m, sc_m, g_m = jnp.split(mod, N_ADA, axis=-1)

    o_q, o_kv = 0, qw
    o_gate = o_kv + 6 * gw
    o_cq = o_gate + nsa_h * 3
    o_ckv = o_cq + q_rank
    o_kr = o_ckv + kv_rank
    kvcol = lambda idx: w_in[:, o_kv + idx * gw:o_kv + (idx + 1) * gw]
    w1c = jnp.concatenate([w_in[:, o_q:o_q + qw], kvcol(2), kvcol(4)], axis=1).astype(CDT)
    wvc = jnp.concatenate([kvcol(3), kvcol(5)], axis=1).astype(CDT)
    w2c = jnp.concatenate([w_in[:, o_cq:o_cq + q_rank], w_in[:, o_ckv:o_ckv + kv_rank], kvcol(0),
                           kvcol(1)], axis=1).astype(CDT)
    zpad = lambda n: jnp.zeros((d, n), w_in.dtype)
    w3c = jnp.concatenate([w_in[:, o_kr:o_kr + MLA_ROPE], zpad(LANES - MLA_ROPE),
                           w_in[:, o_gate:o_gate + nsa_h * 3], zpad(LANES - nsa_h * 3)],
                          axis=1).astype(CDT)

    h = _modcast(x2, sc_a, sh_a, seq)

    tm = _tile(t, (1024, 512, 256, 128))
    nsa_tabs = _rope_tables(positions, PARTIAL_ROT, LANES, 0)
    n1 = w1c.shape[1]
    tn1 = _tile(gw, (512, 256, 128))
    scale = HEAD_DIM ** -0.5 * LOG2E
    p1 = _mm([h], w1c, tm=tm, tn=tn1, out_dtype=CDT, name="in_proj_rot",
             epilogue=_rope_epilogue(PARTIAL_ROT // 2, 0, n1 // tn1, qw // tn1, scale),
             extras=_table_extras(nsa_tabs, tm))
    tk = _tile(seq, (512, 256, 128))
    vt_nsa = _mm([h], wvc, tm=tm, tn=_tile(2 * gw, (512, 256, 128)), out_dtype=CDT, name="in_proj_vt",
                 head_tiles=tk, seq=seq)
    n2 = w2c.shape[1]
    lat = q_rank + kv_rank
    tn2 = _tile(math.gcd(lat, gw), (512, 256, 128))
    p2 = _mm([h], w2c, tm=tm, tn=tn2, out_dtype=F32, name="in_proj_lat",
             epilogue=_rope_epilogue(PARTIAL_ROT // 2, lat // tn2, (lat + gw) // tn2, 0, 1.0),
             extras=_table_extras(nsa_tabs, tm))
    kr_tabs = _rope_tables(positions, MLA_ROPE, LANES, 0)
    p3 = _mm([h], w3c, tm=tm, tn=2 * LANES, out_dtype=F32, name="in_proj_kr_gate",
             epilogue=_krope_gate_epilogue, extras=_table_extras(kr_tabs, tm))

    qkv = p1.reshape(bsz, seq, n1)
    gates = p3.reshape(bsz, seq, 2 * LANES)

    nchunk = seq // CMP_STRIDE
    ncmp = (seq - CMP_LEN) // CMP_STRIDE + 1
    blk = lambda col: col // HEAD_DIM
    kc = _compress(p2, blk(lat), pos_k, k1, k2, bsz=bsz, seq=seq, g_groups=g_groups)
    vc = _compress(p2, blk(lat + gw), pos_v, v1, v2, bsz=bsz, seq=seq, g_groups=g_groups)

    nb = seq // SEL_LEN
    c_start = np.arange(nchunk) * CMP_STRIDE
    b_start = np.arange(LANES) * SEL_LEN
    overlap = ((c_start[:, None] < b_start[None, :] + SEL_LEN) &
               (c_start[:, None] + CMP_LEN > b_start[None, :]) &
               (np.arange(nchunk)[:, None] < ncmp) & (np.arange(LANES)[None, :] < nb))
    overlap_t = jnp.asarray(overlap.T.astype(np.float32)).astype(CDT)
    e_mat = jnp.asarray((np.arange(seq)[:, None] // SEL_LEN == np.arange(LANES)[None, :])
                        .astype(np.float32)).astype(CDT)
    o_nsa = _nsa_attn(qkv, blk(qw), blk(qw + gw), vt_nsa, gates, kc, vc.transpose(0, 1, 3, 2),
                      overlap_t, e_mat, g_groups=g_groups, r_heads=r_heads, seq=seq)

    qk_dim = MLA_NOPE + MLA_ROPE
    wq = w_uq.reshape(q_rank, mla_h, qk_dim)
    wq = jnp.concatenate([wq, jnp.zeros((q_rank, mla_h, MLA_QK_PAD - qk_dim), wq.dtype)], axis=-1)
    wq = wq.reshape(q_rank, mla_h * MLA_QK_PAD).astype(CDT)
    wkv = w_ukv.reshape(kv_rank, mla_h, MLA_NOPE + MLA_V)
    wk = wkv[:, :, :MLA_NOPE].reshape(kv_rank, mla_h * MLA_NOPE).astype(CDT)
    wv = wkv[:, :, MLA_NOPE:].reshape(kv_rank, mla_h * MLA_V).astype(CDT)
    tnq = _tile(mla_h * MLA_QK_PAD, (512, 256))
    q_mla = _mm([p2], wq, tm=tm, tn=tnq, out_dtype=CDT, name="mla_uq",
                a_specs=[pl.BlockSpec((tm, q_rank), lambda i, j: (i, 0))],
                prologue=_rms_prologue(1e-6),
                epilogue=_mla_q_epilogue(qk_dim ** -0.5 * LOG2E),
                extras=_table_extras(kr_tabs, tm)
                + [(q_norm.reshape(1, q_rank), pl.BlockSpec((1, q_rank), lambda i, j: (0, 0)))])
    tnkv = _tile(wk.shape[1], (512, 256, 128))
    ckv_args = dict(tm=tm, tn=tnkv, out_dtype=CDT, prologue=_rms_prologue(1e-6),
                    a_specs=[pl.BlockSpec((tm, kv_rank), lambda i, j: (i, q_rank // kv_rank))],
                    extras=[(kv_norm.reshape(1, kv_rank), pl.BlockSpec((1, kv_rank), lambda i, j: (0, 0)))])
    k_mla = _mm([p2], wk, name="mla_uk", **ckv_args)
    vt_mla = _mm([p2], wv, name="mla_uv", head_tiles=tk, seq=seq, **ckv_args)
    o_mla = _mla_attn(q_mla.reshape(bsz, seq, -1), k_mla.reshape(bsz, seq, -1),
                      gates[:, :, :LANES].astype(CDT), vt_mla, heads=mla_h, seq=seq)

    tno = _tile(d, (1024, 512, 256, 128))
    a = _mm([o_nsa.reshape(t, qw), o_mla.reshape(t, mla_h * MLA_V)], w_out.astype(CDT), tm=tm, tn=tno,
            out_dtype=F32, name="out_proj")
    x1, h2 = _res_ln(x2, a, g_a, ln1_g, ln1_b, seq, alpha, mod=(sc_m, sh_m))
    d_ff = w_ff1.shape[1]
    f1 = _mm([h2], w_ff1.astype(CDT), tm=tm, tn=_tile(d_ff, (1024, 512, 256, 128)), out_dtype=CDT,
             name="ff1", epilogue=_relu2_epilogue)
    f2 = _mmk(f1, w_ff2.astype(CDT), tm=tm, tn=tno, tk=_tile(d_ff, (4096, 2048, 1024, 512)),
              out_dtype=F32, name="ff2")
    return _res_ln(x1, f2, g_m, ln2_g, ln2_b, seq, alpha)


def kernel(x, c, positions, w_ada, b_ada, w_in, nsa_pos_k, nsa_pos_v, nsa_cmp_k1, nsa_cmp_k2, nsa_cmp_v1, nsa_cmp_v2, mla_q_norm, mla_kv_norm, mla_w_uq, mla_w_ukv, w_out, ln1_g, ln1_b, w_ff1, w_ff2, ln2_g, ln2_b):
    bsz, seq, d = x.shape
    depth = w_ada.shape[0]
    alpha = (2 * depth) ** 0.25
    x2 = x.reshape(bsz * seq, d)
    for layer in range(depth):
        mod = _ada(c, w_ada[layer], b_ada[layer])
        x2 = _layer(x2, mod, positions, w_in[layer], nsa_pos_k[layer], nsa_pos_v[layer],
                    nsa_cmp_k1[layer], nsa_cmp_k2[layer], nsa_cmp_v1[layer], nsa_cmp_v2[layer],
                    mla_q_norm[layer], mla_kv_norm[layer], mla_w_uq[layer], mla_w_ukv[layer],
                    w_out[layer], ln1_g[layer], ln1_b[layer], w_ff1[layer], w_ff2[layer],
                    ln2_g[layer], ln2_b[layer], bsz=bsz, seq=seq, alpha=alpha)
    return x2.reshape(bsz, seq, d)
```

```python
import functools
import math

import numpy as np
import jax
import jax.numpy as jnp
from jax import lax
from jax.experimental import pallas as pl
from jax.experimental.pallas import tpu as pltpu

HEAD_DIM = 128
CMP_LEN = 32
CMP_STRIDE = 16
SEL_LEN = 64
N_SEL = 16
N_LOCAL_FORCED = 2
FORCED_BONUS = 1e4
WINDOW = 512
MLA_NOPE = 128
MLA_ROPE = 64
MLA_V = 128
ROPE_THETA = 500000.0
PARTIAL_ROT = HEAD_DIM // 4
NEG = -1e30
LOG2E = math.log2(math.e)
N_ADA = 6
LANES = 128
MLA_QK_PAD = 2 * LANES
VMEM_LIMIT = 56 * 1024 * 1024
MM_SUB_COLS = 256

F32 = jnp.float32
CDT = jnp.bfloat16


def _params(sem):
    return pltpu.CompilerParams(dimension_semantics=sem, vmem_limit_bytes=VMEM_LIMIT)


def _tile(n, cands):
    for c in cands:
        if n % c == 0:
            return c
    return n


def _ada_body(c_ref, w_ref, b_ref, o_ref):
    c = c_ref[...]
    cond = c * (1.0 / (1.0 + jnp.exp(-c)))
    o_ref[...] = jnp.dot(cond.astype(CDT), w_ref[...].astype(CDT),
                         preferred_element_type=F32) + b_ref[...]


def _ada(c, w, b):
    bsz, d = c.shape
    n = w.shape[1]
    tn = _tile(n, (1024, 512, 256, 128))
    return pl.pallas_call(
        _ada_body,
        grid=(n // tn,),
        in_specs=[pl.BlockSpec((bsz, d), lambda j: (0, 0)),
                  pl.BlockSpec((d, tn), lambda j: (0, j)),
                  pl.BlockSpec((1, tn), lambda j: (0, j))],
        out_specs=pl.BlockSpec((bsz, tn), lambda j: (0, j)),
        out_shape=jax.ShapeDtypeStruct((bsz, n), F32),
        compiler_params=_params(("arbitrary",)),
        name="ada",
    )(c, w, b.reshape(1, n))


def _modcast_body(x_ref, sc_ref, sh_ref, o_ref):
    o_ref[...] = (x_ref[...] * (1.0 + sc_ref[0]) + sh_ref[0]).astype(o_ref.dtype)


def _modcast(x2, sc, sh, seq):
    t, d = x2.shape
    ts = _tile(seq, (256, 128, 64, 8))
    per = seq // ts
    vec = pl.BlockSpec((1, 1, d), lambda i: (i // per, 0, 0))
    return pl.pallas_call(
        _modcast_body,
        grid=(t // ts,),
        in_specs=[pl.BlockSpec((ts, d), lambda i: (i, 0)), vec, vec],
        out_specs=pl.BlockSpec((ts, d), lambda i: (i, 0)),
        out_shape=jax.ShapeDtypeStruct((t, d), CDT),
        compiler_params=_params(("arbitrary",)),
        name="modcast",
    )(x2, sc[:, None, :], sh[:, None, :])


def _mm_body(*refs, na, nex, prologue, epilogue, tn, head_tiles):
    a_refs = refs[:na]
    b_ref = refs[na]
    ex = refs[na + 1:na + 1 + nex]
    o_ref = refs[na + 1 + nex]
    j = pl.program_id(1)
    if prologue is not None:
        a_sc = refs[na + 2 + nex]

        @pl.when(j == 0)
        def _():
            a_sc[...] = prologue(a_refs[0][...], ex).astype(a_sc.dtype)

        a_refs = (a_sc,)
    sub = MM_SUB_COLS if tn % MM_SUB_COLS == 0 else tn

    def store(s, y):
        if head_tiles is None:
            o_ref[:, s * sub:(s + 1) * sub] = y.astype(o_ref.dtype)
        else:
            for hh in range(sub // HEAD_DIM):
                for tt in range(y.shape[0] // head_tiles):
                    blk = y[tt * head_tiles:(tt + 1) * head_tiles, hh * HEAD_DIM:(hh + 1) * HEAD_DIM]
                    o_ref[0, s * (sub // HEAD_DIM) + hh, tt] = blk.T.astype(o_ref.dtype)

    def run(fn):
        for s in range(tn // sub):
            cols = slice(s * sub, (s + 1) * sub)
            acc = None
            off = 0
            for r in a_refs:
                kr = r.shape[1]
                part = jnp.dot(r[...], b_ref[off:off + kr, cols], preferred_element_type=F32)
                acc = part if acc is None else acc + part
                off += kr
            store(s, fn(acc, j, ex))

    if epilogue is None:
        run(lambda acc, j, ex: acc)
    else:
        for cond, fn in epilogue(j):
            pl.when(cond)(functools.partial(run, fn))


def _mm(a_list, b, *, tm, tn, out_dtype, name, a_specs=None, prologue=None, epilogue=None,
        extras=(), head_tiles=None, seq=None):
    m = a_list[0].shape[0]
    k, n = b.shape
    if a_specs is None:
        a_specs = [pl.BlockSpec((tm, a.shape[1]), lambda i, j: (i, 0)) for a in a_list]
    ex_arrays = [e[0] for e in extras]
    ex_specs = [e[1] for e in extras]
    scratch = [pltpu.VMEM((tm, k), CDT)] if prologue is not None else []
    body = functools.partial(_mm_body, na=len(a_list), nex=len(extras), prologue=prologue,
                             epilogue=epilogue, tn=tn, head_tiles=head_tiles)
    if head_tiles is None:
        out_specs = pl.BlockSpec((tm, tn), lambda i, j: (i, j))
        out_shape = jax.ShapeDtypeStruct((m, n), out_dtype)
    else:
        per = seq // tm
        out_specs = pl.BlockSpec((1, tn // HEAD_DIM, tm // head_tiles, HEAD_DIM, head_tiles),
                                 lambda i, j: (i // per, j, i % per, 0, 0))
        out_shape = jax.ShapeDtypeStruct((m // seq, n // HEAD_DIM, seq // head_tiles, HEAD_DIM, head_tiles),
                                         out_dtype)
    return pl.pallas_call(
        body,
        grid=(m // tm, n // tn),
        in_specs=a_specs + [pl.BlockSpec((k, tn), lambda i, j: (0, j))] + ex_specs,
        out_specs=out_specs,
        out_shape=out_shape,
        scratch_shapes=scratch,
        compiler_params=_params(("arbitrary", "arbitrary")),
        name=name,
    )(*a_list, b, *ex_arrays)


def _rot(acc, cos, sa, sb, shift):
    tn = acc.shape[1]
    reps = tn // cos.shape[1]

    def rep(t):
        return t if reps == 1 else jnp.concatenate([t] * reps, axis=1)

    return (acc * rep(cos) + pltpu.roll(acc, tn - shift, 1) * rep(sa)
            + pltpu.roll(acc, shift, 1) * rep(sb))


def _rope_epilogue(shift, lo, hi, n_scaled, scale):
    def roped(acc, j, ex):
        y = _rot(acc, ex[0][...], ex[1][...], ex[2][...], shift)
        return y * jnp.where(j < n_scaled, scale, 1.0).astype(F32)

    def epi(j):
        inside = (j >= lo) & (j < hi)
        return [(inside, roped), (jnp.logical_not(inside), lambda acc, j, ex: acc)]

    return epi


def _mla_q_epilogue(scale):
    assert MM_SUB_COLS == MLA_QK_PAD

    def fn(acc, j, ex):
        rot = _rot(acc[:, LANES:], ex[0][...], ex[1][...], ex[2][...], MLA_ROPE // 2)
        return jnp.concatenate([acc[:, :LANES], rot], axis=1) * scale

    return lambda j: [(j >= 0, fn)]


def _krope_gate_epilogue(j):
    def fn(acc, j, ex):
        y = _rot(acc, ex[0][...], ex[1][...], ex[2][...], MLA_ROPE // 2)
        lane = lax.broadcasted_iota(jnp.int32, acc.shape, 1)
        return jnp.where(lane < LANES, y, 1.0 / (1.0 + jnp.exp(-acc)))

    return [(j >= 0, fn)]


def _rms_prologue(eps):
    def pro(a, ex):
        g = ex[-1][...]
        y = a * lax.rsqrt(jnp.mean(a * a, axis=-1, keepdims=True) + eps)
        return y * g

    return pro


def _relu2_epilogue(j):
    def fn(acc, j, ex):
        r = jnp.maximum(acc, 0.0)
        return r * r

    return [(j >= 0, fn)]


def _mmk_body(a_ref, b_ref, o_ref):
    kk = pl.program_id(2)

    @pl.when(kk == 0)
    def _():
        o_ref[...] = jnp.dot(a_ref[...], b_ref[...], preferred_element_type=F32)

    @pl.when(kk > 0)
    def _():
        o_ref[...] += jnp.dot(a_ref[...], b_ref[...], preferred_element_type=F32)


def _mmk(a, b, *, tm, tn, tk, name):
    m, k = a.shape
    n = b.shape[1]
    return pl.pallas_call(
        _mmk_body,
        grid=(m // tm, n // tn, k // tk),
        in_specs=[pl.BlockSpec((tm, tk), lambda i, j, kk: (i, kk)),
                  pl.BlockSpec((tk, tn), lambda i, j, kk: (kk, j))],
        out_specs=pl.BlockSpec((tm, tn), lambda i, j, kk: (i, j)),
        out_shape=jax.ShapeDtypeStruct((m, n), F32),
        compiler_params=_params(("arbitrary", "arbitrary", "arbitrary")),
        name=name,
    )(a, b)


def _ln_body(x_ref, a_ref, gate_ref, g_ref, b_ref, *rest, alpha, with_mod):
    y = alpha * x_ref[...] + (1.0 + gate_ref[0]) * a_ref[...]
    mu = jnp.mean(y, axis=-1, keepdims=True)
    yc = y - mu
    var = jnp.mean(yc * yc, axis=-1, keepdims=True)
    out = yc * lax.rsqrt(var + 1e-5) * g_ref[...] + b_ref[...]
    if with_mod:
        sc_ref, sh_ref, o_ref, h_ref = rest
        o_ref[...] = out
        h_ref[...] = (out * (1.0 + sc_ref[0]) + sh_ref[0]).astype(h_ref.dtype)
    else:
        (o_ref,) = rest
        o_ref[...] = out


def _res_ln(x2, a2, gate, g, b, seq, alpha, mod=None):
    t, d = x2.shape
    ts = _tile(seq, (256, 128, 64, 8))
    per = seq // ts
    row = pl.BlockSpec((ts, d), lambda i: (i, 0))
    vec = pl.BlockSpec((1, 1, d), lambda i: (i // per, 0, 0))
    par = pl.BlockSpec((1, d), lambda i: (0, 0))
    ins = [x2, a2, gate[:, None, :], g.reshape(1, d), b.reshape(1, d)]
    specs = [row, row, vec, par, par]
    if mod is None:
        out_shape = jax.ShapeDtypeStruct((t, d), F32)
        out_specs = row
    else:
        ins += [mod[0][:, None, :], mod[1][:, None, :]]
        specs += [vec, vec]
        out_shape = (jax.ShapeDtypeStruct((t, d), F32), jax.ShapeDtypeStruct((t, d), CDT))
        out_specs = (row, row)
    return pl.pallas_call(
        functools.partial(_ln_body, alpha=alpha, with_mod=mod is not None),
        grid=(t // ts,),
        in_specs=specs,
        out_specs=out_specs,
        out_shape=out_shape,
        compiler_params=_params(("arbitrary",)),
        name="res_ln",
    )(*ins)


def _compress_body(x_ref, pos_ref, w1_ref, w2_ref, o_ref):
    nc = o_ref.shape[2]
    acc_a = None
    acc_b = None
    for l in range(CMP_STRIDE):
        x = x_ref[pl.ds(l, nc, stride=CMP_STRIDE), :]
        xa = (x + pos_ref[l:l + 1, :]).astype(CDT)
        xb = (x + pos_ref[CMP_STRIDE + l:CMP_STRIDE + l + 1, :]).astype(CDT)
        wa = w1_ref[l * HEAD_DIM:(l + 1) * HEAD_DIM, :]
        wb = w1_ref[(CMP_STRIDE + l) * HEAD_DIM:(CMP_STRIDE + l + 1) * HEAD_DIM, :]
        pa = jnp.dot(xa, wa, preferred_element_type=F32)
        pb = jnp.dot(xb, wb, preferred_element_type=F32)
        acc_a = pa if acc_a is None else acc_a + pa
        acc_b = pb if acc_b is None else acc_b + pb
    pre = acc_a + pltpu.roll(acc_b, nc - 1, 0)
    hid = 0.5 * pre * (1.0 + jnp.tanh(math.sqrt(2.0 / math.pi) * (pre + 0.044715 * pre * pre * pre)))
    o_ref[0, 0] = jnp.dot(hid.astype(CDT), w2_ref[...], preferred_element_type=F32).astype(o_ref.dtype)


def _compress(x2, col_blk0, pos, w1, w2, *, bsz, seq, g_groups):
    nc = seq // CMP_STRIDE
    full = lambda shape: pl.BlockSpec(shape, lambda b, gi: (0,) * len(shape))
    return pl.pallas_call(
        _compress_body,
        grid=(bsz, g_groups),
        in_specs=[pl.BlockSpec((seq, HEAD_DIM), lambda b, gi: (b, col_blk0 + gi)),
                  full(pos.shape), full(w1.shape), full(w2.shape)],
        out_specs=pl.BlockSpec((1, 1, nc, HEAD_DIM), lambda b, gi: (b, gi, 0, 0)),
        out_shape=jax.ShapeDtypeStruct((bsz, g_groups, nc, HEAD_DIM), CDT),
        compiler_params=_params(("arbitrary", "arbitrary")),
        name="compress",
    )(x2, pos, w1.astype(CDT), w2.astype(CDT))


def _cmp_branch(qs, kc, vct, ovt, gates_t, gi, i, *, tq, r_heads, nb):
    nc = kc.shape[0]
    cols = r_heads * tq
    s = lax.dot_general(kc, qs, (((1,), (1,)), ((), ())), preferred_element_type=F32)
    cidx = lax.broadcasted_iota(jnp.int32, (nc, cols), 0)
    tok = i * tq + (lax.broadcasted_iota(jnp.int32, (nc, cols), 1) & (tq - 1))
    valid = cidx * CMP_STRIDE + (CMP_LEN - 1) <= tok
    s = jnp.where(valid, s, NEG)
    e = jnp.exp2(s - jnp.max(s, axis=0, keepdims=True))
    p = jnp.where(valid, e, 0.0) * (1.0 / jnp.sum(e, axis=0, keepdims=True))
    o_t = jnp.dot(vct, p.astype(CDT), preferred_element_type=F32)
    rid = lax.broadcasted_iota(jnp.int32, gates_t.shape, 0)
    psum = None
    gated = []
    for r in range(r_heads):
        cs = slice(r * tq, (r + 1) * tq)
        col = (gi * r_heads + r) * 3
        grow = jnp.sum(jnp.where(rid == col, gates_t, 0.0), axis=0, keepdims=True)
        gated.append(o_t[:, cs] * grow)
        psum = p[:, cs] if psum is None else psum + p[:, cs]

    p_hi = psum.astype(CDT)
    p_lo = (psum - p_hi.astype(F32)).astype(CDT)
    imp = (jnp.dot(ovt, p_hi, preferred_element_type=F32)
           + jnp.dot(ovt, p_lo, preferred_element_type=F32))
    jj = lax.broadcasted_iota(jnp.int32, (LANES, tq), 0)
    cur = (i * tq + lax.broadcasted_iota(jnp.int32, (LANES, tq), 1)) // SEL_LEN
    forced = (jj == 0) | ((jj <= cur) & (jj > cur - N_LOCAL_FORCED))
    imp = jnp.where(jj > cur, NEG, imp + jnp.where(forced, FORCED_BONUS, 0.0))
    v = imp[:nb]
    ridx = lax.broadcasted_iota(jnp.int32, (nb, tq), 0)
    sel = jnp.zeros((nb, tq), F32)
    for _ in range(min(N_SEL, nb)):
        mx = jnp.max(v, axis=0, keepdims=True)
        first = jnp.min(jnp.where(v == mx, ridx, nb), axis=0, keepdims=True)
        hit = ridx == first
        sel = jnp.where(hit, 1.0, sel)
        v = jnp.where(hit, -jnp.inf, v)
    bias_t = jnp.where(sel > 0.0, 0.0, NEG)
    if nb < LANES:
        bias_t = jnp.concatenate([bias_t, jnp.zeros((LANES - nb, tq), F32)], axis=0)
    return jnp.concatenate(gated, axis=1), bias_t.T.astype(CDT)


ACC_ROWS = HEAD_DIM + 16


def _with_ones_row(vt):
    ones_row = jnp.where(lax.broadcasted_iota(jnp.int32, (ACC_ROWS - HEAD_DIM, vt.shape[1]), 0) == 0,
                         1.0, 0.0).astype(CDT)
    return jnp.concatenate([vt, ones_row], axis=0)


def _flash_init(m_ref, acc_ref):
    m_ref[...] = jnp.full(m_ref.shape, -jnp.inf, F32)
    acc_ref[...] = jnp.zeros(acc_ref.shape, F32)


def _nsa_body(q_ref, kc_ref, vct_ref, ovt_ref, ks_ref, e_ref, vs_ref, kw_ref, vw_ref, gate_ref, o_ref,
              qc_ref, oc_ref, s0_ref, s1_ref, cm_ref, m_ref, acc_ref, *, tq, r_heads, n_win, nb):
    s_refs = (s0_ref, s1_ref)
    gi = pl.program_id(1)
    i = pl.program_id(2)
    tk = tq
    cols = r_heads * tq
    for r in range(r_heads):
        qc_ref[r * tq:(r + 1) * tq, :HEAD_DIM] = q_ref[0, :, r * HEAD_DIM:(r + 1) * HEAD_DIM]
    _flash_init(m_ref, acc_ref)

    def local_iotas():
        key = lax.broadcasted_iota(jnp.int32, (tk, cols), 0)
        tok = lax.broadcasted_iota(jnp.int32, (tk, cols), 1) & (tq - 1)
        return key, tok

    def put(s, buf):
        s_refs[buf][...] = s
        cm_ref[buf] = jnp.max(s, axis=0, keepdims=True)

    def update(branch, vt, buf):
        m_prev = m_ref[branch]
        m_new = jnp.maximum(m_prev, cm_ref[buf])
        p = jnp.exp2(s_refs[buf][...] - m_new).astype(CDT)
        acc_ref[branch] = (jnp.exp2(m_prev - m_new) * acc_ref[branch]
                           + jnp.dot(_with_ones_row(vt), p, preferred_element_type=F32))
        m_ref[branch] = m_new

    def sel_scores(j, buf, masked):
        off = pl.multiple_of(j * tk, tk)
        kmat = jnp.concatenate([ks_ref[0, pl.ds(off, tk), :], e_ref[pl.ds(off, tk), :]], axis=1)
        s = lax.dot_general(kmat, qc_ref[...], (((1,), (1,)), ((), ())), preferred_element_type=F32)
        if masked:
            key, tok = local_iotas()
            if masked == "if_diagonal":
                tok = tok + jnp.where(j == i, 0, tk)
            s = jnp.where(key <= tok, s, NEG)
        put(s, buf)

    def sel_update(j, buf):
        update(0, vs_ref[0, 0, j], buf)

    def win_tile(dd):
        back = n_win - 1 - dd
        return back, jnp.maximum(i - back, 0)

    def win_scores(dd, buf):
        back, j = win_tile(dd)
        off = pl.multiple_of(j * tk, tk)
        s = lax.dot_general(kw_ref[0, pl.ds(off, tk), :], qc_ref[:, :HEAD_DIM], (((1,), (1,)), ((), ())),
                            preferred_element_type=F32)
        key, tok = local_iotas()
        if back * tk - (tk - 1) < 0:
            s = jnp.where(key - back * tk <= tok, s, NEG)
        if back > 0 or tq - 1 + back * tk >= WINDOW:
            reach = WINDOW - back * tk
            if back > 0:
                reach = reach - jnp.where(i >= back, 0, 2 * WINDOW + tq)
            s = jnp.where(tok < key + reach, s, NEG)
        put(s, buf)

    def win_update(dd, buf):
        update(1, vw_ref[0, 0, win_tile(dd)[1]], buf)

    win_scores(0, 0)
    if n_win > 1:
        win_scores(1, 1)
    oc_ref[...], bias = _cmp_branch(qc_ref[:, :HEAD_DIM], kc_ref[0, 0], vct_ref[0, 0], ovt_ref[...],
                                    gate_ref[0].T, gi, i, tq=tq, r_heads=r_heads, nb=nb)
    for r in range(r_heads):
        qc_ref[r * tq:(r + 1) * tq, HEAD_DIM:] = bias
    for dd in range(n_win):
        if dd == n_win - 1 and n_win % 2 == 0:
            sel_scores(0, 0, "if_diagonal")
        win_update(dd, dd % 2)
        if dd + 2 < n_win:
            win_scores(dd + 2, dd % 2)
    if n_win % 2 == 1:
        sel_scores(0, 0, "if_diagonal")

    n_pairs = jnp.maximum(i - 1, 0) // 2

    def body(jp, carry):
        j = 2 * jp
        sel_scores(j + 1, 1, False)
        sel_update(j, 0)
        sel_scores(j + 2, 0, False)
        sel_update(j + 1, 1)
        return carry

    lax.fori_loop(0, n_pairs, body, 0)
    j0 = 2 * n_pairs
    left = i - j0

    @pl.when(left == 1)
    def _():
        sel_scores(i, 1, True)
        sel_update(j0, 0)
        sel_update(i, 1)

    @pl.when(left == 2)
    def _():
        sel_scores(j0 + 1, 1, False)
        sel_update(j0, 0)
        sel_scores(i, 0, True)
        sel_update(j0 + 1, 1)
        sel_update(i, 0)

    @pl.when(left == 0)
    def _():
        sel_update(0, 0)

    gates_t = gate_ref[0].T
    rid = lax.broadcasted_iota(jnp.int32, gates_t.shape, 0)
    for r in range(r_heads):
        cs = slice(r * tq, (r + 1) * tq)
        o = oc_ref[:, cs]
        for branch in (0, 1):
            acc = acc_ref[branch]
            col = (gi * r_heads + r) * 3 + 1 + branch
            grow = jnp.sum(jnp.where(rid == col, gates_t, 0.0), axis=0, keepdims=True)
            o = o + acc[:HEAD_DIM, cs] * (grow / acc[HEAD_DIM:HEAD_DIM + 1, cs])
        o_ref[0, :, r * HEAD_DIM:(r + 1) * HEAD_DIM] = o.T.astype(o_ref.dtype)


def _nsa_attn(qkv, ks_blk0, kw_blk0, vt, gates, kc, vct, overlap_t, e_mat, *, g_groups, r_heads, seq):
    bsz = qkv.shape[0]
    qw = r_heads * HEAD_DIM
    tq = tk = vt.shape[-1]
    cols = r_heads * tq
    nc = kc.shape[2]
    kspec = lambda blk0: pl.BlockSpec((1, seq, HEAD_DIM), lambda b, g, i: (b, 0, blk0 + g))
    vspec = lambda h0: pl.BlockSpec((1, 1) + vt.shape[2:], lambda b, g, i: (b, h0 + g, 0, 0, 0))
    qspec = pl.BlockSpec((1, tq, qw), lambda b, g, i: (b, i, g))
    body = functools.partial(_nsa_body, tq=tq, r_heads=r_heads, n_win=(WINDOW + tk - 1) // tk + 1,
                             nb=seq // SEL_LEN)
    return pl.pallas_call(
        body,
        grid=(bsz, g_groups, seq // tq),
        in_specs=[qspec,
                  pl.BlockSpec((1, 1, nc, HEAD_DIM), lambda b, g, i: (b, g, 0, 0)),
                  pl.BlockSpec((1, 1, HEAD_DIM, nc), lambda b, g, i: (b, g, 0, 0)),
                  pl.BlockSpec((LANES, nc), lambda b, g, i: (0, 0)),
                  kspec(ks_blk0), pl.BlockSpec((seq, LANES), lambda b, g, i: (0, 0)), vspec(0),
                  kspec(kw_blk0), vspec(g_groups),
                  pl.BlockSpec((1, tq, LANES), lambda b, g, i: (b, i, 1))],
        out_specs=qspec,
        out_shape=jax.ShapeDtypeStruct((bsz, seq, g_groups * qw), CDT),
        scratch_shapes=[pltpu.VMEM((cols, 2 * HEAD_DIM), CDT),
                        pltpu.VMEM((HEAD_DIM, cols), F32),
                        pltpu.VMEM((tk, cols), F32), pltpu.VMEM((tk, cols), F32),
                        pltpu.VMEM((2, 1, cols), F32),
                        pltpu.VMEM((2, 1, cols), F32),
                        pltpu.VMEM((2, ACC_ROWS, cols), F32)],
        compiler_params=_params(("arbitrary",) * 3),
        name="nsa_attn",
    )(qkv, kc, vct, overlap_t, qkv, e_mat, vt, qkv, vt, gates)


def _mla_body(q_ref, kn_ref, kr_ref, vt_ref, o_ref, *rest, tq, tk, hg):
    s_refs = rest[:hg]
    cm_ref, m_ref, acc_ref = rest[hg:]
    i = pl.program_id(2)
    _flash_init(m_ref, acc_ref)
    n_full = (i * tq) // tk

    def step(j, mask, c0):
        off = pl.multiple_of(j * tk, tk)
        kr = kr_ref[0, pl.ds(off, tk), :]

        def scores(c):
            kmat = jnp.concatenate([kn_ref[0, pl.ds(off, tk), c * MLA_NOPE:(c + 1) * MLA_NOPE], kr], axis=1)
            s = lax.dot_general(kmat, q_ref[0, c0:, c * MLA_QK_PAD:(c + 1) * MLA_QK_PAD],
                                (((1,), (1,)), ((), ())), preferred_element_type=F32)
            if mask is not None:
                s = jnp.where(mask, s, NEG)
            s_refs[c][:, c0:] = s
            cm_ref[c, :, c0:] = jnp.max(s, axis=0, keepdims=True)

        def update(c):
            m_prev = m_ref[c, :, c0:]
            m_new = jnp.maximum(m_prev, cm_ref[c, :, c0:])
            p = jnp.exp2(s_refs[c][:, c0:] - m_new).astype(CDT)
            acc_ref[c, :, c0:] = (jnp.exp2(m_prev - m_new) * acc_ref[c, :, c0:]
                                  + jnp.dot(_with_ones_row(vt_ref[0, c, j]), p, preferred_element_type=F32))
            m_ref[c, :, c0:] = m_new

        scores(0)
        for c in range(1, hg):
            scores(c)
            update(c - 1)
        update(hg - 1)

    def full_tile(j, carry):
        step(j, None, 0)
        return carry

    lax.fori_loop(0, n_full, full_tile, 0)
    for dd in range(tq // tk):
        w = tq - dd * tk
        mask = (lax.broadcasted_iota(jnp.int32, (tk, w), 0) <= lax.broadcasted_iota(jnp.int32, (tk, w), 1))
        step(n_full + dd, mask, dd * tk)
    for c in range(hg):
        acc = acc_ref[c]
        o = acc[:HEAD_DIM] * (1.0 / acc[HEAD_DIM:HEAD_DIM + 1])
        o_ref[0, :, c * MLA_V:(c + 1) * MLA_V] = o.T.astype(o_ref.dtype)


def _mla_attn(q, kv, kr, vt, *, heads, seq):
    bsz = q.shape[0]
    tk = vt.shape[-1]
    tq = _tile(seq, (1024, 512, 256, 128))
    hg = 4 if heads % 4 == 0 else 1
    body = functools.partial(_mla_body, tq=tq, tk=tk, hg=hg)
    return pl.pallas_call(
        body,
        grid=(bsz, heads // hg, seq // tq),
        in_specs=[pl.BlockSpec((1, tq, hg * MLA_QK_PAD), lambda b, h, i: (b, i, h)),
                  pl.BlockSpec((1, seq, hg * MLA_NOPE), lambda b, h, i: (b, 0, h)),
                  pl.BlockSpec((1, seq, LANES), lambda b, h, i: (b, 0, 0)),
                  pl.BlockSpec((1, hg) + vt.shape[2:], lambda b, h, i: (b, h, 0, 0, 0))],
        out_specs=pl.BlockSpec((1, tq, hg * MLA_V), lambda b, h, i: (b, i, h)),
        out_shape=jax.ShapeDtypeStruct((bsz, seq, heads * MLA_V), CDT),
        scratch_shapes=[pltpu.VMEM((tk, tq), F32)] * hg + [
            pltpu.VMEM((hg, 1, tq), F32), pltpu.VMEM((hg, 1, tq), F32), pltpu.VMEM((hg, ACC_ROWS, tq), F32)],
        compiler_params=_params(("arbitrary",) * 3),
        name="mla_attn",
    )(q, kv, kr, vt)


def _rope_tables(positions, rot_dim, period):
    half = rot_dim // 2
    inv = jnp.power(ROPE_THETA, -jnp.arange(0, rot_dim, 2, dtype=F32) / rot_dim)
    ang = positions.astype(F32).reshape(-1)[:, None] * inv
    cos, sin = jnp.cos(ang), jnp.sin(ang)
    t = ang.shape[0]
    rest = period - rot_dim
    ct = jnp.concatenate([cos, cos, jnp.ones((t, rest), F32)], axis=1)
    sa = jnp.concatenate([-sin, jnp.zeros((t, half + rest), F32)], axis=1)
    sb = jnp.concatenate([jnp.zeros((t, half), F32), sin, jnp.zeros((t, rest), F32)], axis=1)
    return ct, sa, sb


def _table_extras(tables, tm):
    return [(tb, pl.BlockSpec((tm, tb.shape[1]), lambda i, j: (i, 0))) for tb in tables]


def _layer(x2, mod, positions, w_in, pos_k, pos_v, k1, k2, v1, v2, q_norm, kv_norm, w_uq, w_ukv,
           w_out, ln1_g, ln1_b, w_ff1, w_ff2, ln2_g, ln2_b, *, bsz, seq, alpha):
    t, d = x2.shape
    n_heads = d // HEAD_DIM
    nsa_h = n_heads // 2
    mla_h = n_heads - nsa_h
    q_rank = q_norm.shape[0]
    kv_rank = kv_norm.shape[0]
    d_in = w_in.shape[1]
    g_groups = (d_in - nsa_h * HEAD_DIM - nsa_h * 3 - q_rank - kv_rank - MLA_ROPE) // (6 * HEAD_DIM)
    r_heads = nsa_h // g_groups
    gw = g_groups * HEAD_DIM
    qw = nsa_h * HEAD_DIM
    sh_a, sc_a, g_a, sh_m, sc_m, g_m = jnp.split(mod, N_ADA, axis=-1)

    o_q, o_kv = 0, qw
    o_gate = o_kv + 6 * gw
    o_cq = o_gate + nsa_h * 3
    o_ckv = o_cq + q_rank
    o_kr = o_ckv + kv_rank
    kvcol = lambda idx: w_in[:, o_kv + idx * gw:o_kv + (idx + 1) * gw]
    w1c = jnp.concatenate([w_in[:, o_q:o_q + qw], kvcol(2), kvcol(4)], axis=1).astype(CDT)
    wvc = jnp.concatenate([kvcol(3), kvcol(5)], axis=1).astype(CDT)
    w2c = jnp.concatenate([w_in[:, o_cq:o_cq + q_rank], w_in[:, o_ckv:o_ckv + kv_rank], kvcol(0),
                           kvcol(1)], axis=1).astype(CDT)
    zpad = lambda n: jnp.zeros((d, n), w_in.dtype)
    w3c = jnp.concatenate([w_in[:, o_kr:o_kr + MLA_ROPE], zpad(LANES - MLA_ROPE),
                           w_in[:, o_gate:o_gate + nsa_h * 3], zpad(LANES - nsa_h * 3)],
                          axis=1).astype(CDT)

    h = _modcast(x2, sc_a, sh_a, seq)

    tm = _tile(t, (1024, 512, 256, 128))
    nsa_tabs = _rope_tables(positions, PARTIAL_ROT, LANES)
    n1 = w1c.shape[1]
    tn1 = _tile(gw, (512, 256, 128))
    scale = HEAD_DIM ** -0.5 * LOG2E
    p1 = _mm([h], w1c, tm=tm, tn=tn1, out_dtype=CDT, name="in_proj_rot",
             epilogue=_rope_epilogue(PARTIAL_ROT // 2, 0, n1 // tn1, qw // tn1, scale),
             extras=_table_extras(nsa_tabs, tm))
    tk = _tile(seq, (512, 256, 128))
    vt_nsa = _mm([h], wvc, tm=tm, tn=_tile(2 * gw, (512, 256, 128)), out_dtype=CDT, name="in_proj_vt",
                 head_tiles=tk, seq=seq)
    n2 = w2c.shape[1]
    lat = q_rank + kv_rank
    tn2 = _tile(math.gcd(lat, gw), (512, 256, 128))
    p2 = _mm([h], w2c, tm=tm, tn=tn2, out_dtype=F32, name="in_proj_lat",
             epilogue=_rope_epilogue(PARTIAL_ROT // 2, lat // tn2, (lat + gw) // tn2, 0, 1.0),
             extras=_table_extras(nsa_tabs, tm))
    kr_tabs = _rope_tables(positions, MLA_ROPE, LANES)
    p3 = _mm([h], w3c, tm=tm, tn=2 * LANES, out_dtype=F32, name="in_proj_kr_gate",
             epilogue=_krope_gate_epilogue, extras=_table_extras(kr_tabs, tm))

    qkv = p1.reshape(bsz, seq, n1)
    gates = p3.reshape(bsz, seq, 2 * LANES)

    nchunk = seq // CMP_STRIDE
    ncmp = (seq - CMP_LEN) // CMP_STRIDE + 1
    blk = lambda col: col // HEAD_DIM
    kc = _compress(p2, blk(lat), pos_k, k1, k2, bsz=bsz, seq=seq, g_groups=g_groups)
    vc = _compress(p2, blk(lat + gw), pos_v, v1, v2, bsz=bsz, seq=seq, g_groups=g_groups)

    nb = seq // SEL_LEN
    c_start = np.arange(nchunk) * CMP_STRIDE
    b_start = np.arange(LANES) * SEL_LEN
    overlap = ((c_start[:, None] < b_start[None, :] + SEL_LEN) &
               (c_start[:, None] + CMP_LEN > b_start[None, :]) &
               (np.arange(nchunk)[:, None] < ncmp) & (np.arange(LANES)[None, :] < nb))
    overlap_t = jnp.asarray(overlap.T.astype(np.float32)).astype(CDT)
    e_mat = jnp.asarray((np.arange(seq)[:, None] // SEL_LEN == np.arange(LANES)[None, :])
                        .astype(np.float32)).astype(CDT)
    o_nsa = _nsa_attn(qkv, blk(qw), blk(qw + gw), vt_nsa, gates, kc, vc.transpose(0, 1, 3, 2),
                      overlap_t, e_mat, g_groups=g_groups, r_heads=r_heads, seq=seq)

    qk_dim = MLA_NOPE + MLA_ROPE
    wq = w_uq.reshape(q_rank, mla_h, qk_dim)
    wq = jnp.concatenate([wq, jnp.zeros((q_rank, mla_h, MLA_QK_PAD - qk_dim), wq.dtype)], axis=-1)
    wq = wq.reshape(q_rank, mla_h * MLA_QK_PAD).astype(CDT)
    wkv = w_ukv.reshape(kv_rank, mla_h, MLA_NOPE + MLA_V)
    wk = wkv[:, :, :MLA_NOPE].reshape(kv_rank, mla_h * MLA_NOPE).astype(CDT)
    wv = wkv[:, :, MLA_NOPE:].reshape(kv_rank, mla_h * MLA_V).astype(CDT)
    tnq = _tile(mla_h * MLA_QK_PAD, (512, 256))
    q_mla = _mm([p2], wq, tm=tm, tn=tnq, out_dtype=CDT, name="mla_uq",
                a_specs=[pl.BlockSpec((tm, q_rank), lambda i, j: (i, 0))],
                prologue=_rms_prologue(1e-6),
                epilogue=_mla_q_epilogue(qk_dim ** -0.5 * LOG2E),
                extras=_table_extras(kr_tabs, tm)
                + [(q_norm.reshape(1, q_rank), pl.BlockSpec((1, q_rank), lambda i, j: (0, 0)))])
    tnkv = _tile(wk.shape[1], (512, 256, 128))
    ckv_args = dict(tm=tm, tn=tnkv, out_dtype=CDT, prologue=_rms_prologue(1e-6),
                    a_specs=[pl.BlockSpec((tm, kv_rank), lambda i, j: (i, q_rank // kv_rank))],
                    extras=[(kv_norm.reshape(1, kv_rank), pl.BlockSpec((1, kv_rank), lambda i, j: (0, 0)))])
    k_mla = _mm([p2], wk, name="mla_uk", **ckv_args)
    vt_mla = _mm([p2], wv, name="mla_uv", head_tiles=tk, seq=seq, **ckv_args)
    o_mla = _mla_attn(q_mla.reshape(bsz, seq, -1), k_mla.reshape(bsz, seq, -1),
                      gates[:, :, :LANES].astype(CDT), vt_mla, heads=mla_h, seq=seq)

    tno = _tile(d, (1024, 512, 256, 128))
    a = _mm([o_nsa.reshape(t, qw), o_mla.reshape(t, mla_h * MLA_V)], w_out.astype(CDT), tm=tm, tn=tno,
            out_dtype=F32, name="out_proj")
    x1, h2 = _res_ln(x2, a, g_a, ln1_g, ln1_b, seq, alpha, mod=(sc_m, sh_m))
    d_ff = w_ff1.shape[1]
    f1 = _mm([h2], w_ff1.astype(CDT), tm=tm, tn=_tile(d_ff, (1024, 512, 256, 128)), out_dtype=CDT,
             name="ff1", epilogue=_relu2_epilogue)
    f2 = _mmk(f1, w_ff2.astype(CDT), tm=tm, tn=tno, tk=_tile(d_ff, (4096, 2048, 1024, 512)), name="ff2")
    return _res_ln(x1, f2, g_m, ln2_g, ln2_b, seq, alpha)


def kernel(x, c, positions, w_ada, b_ada, w_in, nsa_pos_k, nsa_pos_v, nsa_cmp_k1, nsa_cmp_k2, nsa_cmp_v1, nsa_cmp_v2, mla_q_norm, mla_kv_norm, mla_w_uq, mla_w_ukv, w_out, ln1_g, ln1_b, w_ff1, w_ff2, ln2_g, ln2_b):
    bsz, seq, d = x.shape
    depth = w_ada.shape[0]
    alpha = (2 * depth) ** 0.25
    x2 = x.reshape(bsz * seq, d)
    for layer in range(depth):
        mod = _ada(c, w_ada[layer], b_ada[layer])
        x2 = _layer(x2, mod, positions, w_in[layer], nsa_pos_k[layer], nsa_pos_v[layer],
                    nsa_cmp_k1[layer], nsa_cmp_k2[layer], nsa_cmp_v1[layer], nsa_cmp_v2[layer],
                    mla_q_norm[layer], mla_kv_norm[layer], mla_w_uq[layer], mla_w_ukv[layer],
                    w_out[layer], ln1_g[layer], ln1_b[layer], w_ff1[layer], w_ff2[layer],
                    ln2_g[layer], ln2_b[layer], bsz=bsz, seq=seq, alpha=alpha)
    return x2.reshape(bsz, seq, d)
```

```python
import functools
import math

import numpy as np
import jax
import jax.numpy as jnp
from jax import lax
from jax.experimental import pallas as pl
from jax.experimental.pallas import tpu as pltpu

HEAD_DIM = 128
CMP_LEN = 32
CMP_STRIDE = 16
SEL_LEN = 64
N_SEL = 16
N_LOCAL_FORCED = 2
FORCED_BONUS = 1e4
WINDOW = 512
MLA_NOPE = 128
MLA_ROPE = 64
MLA_V = 128
ROPE_THETA = 500000.0
PARTIAL_ROT = HEAD_DIM // 4
NEG = -1e30
LOG2E = math.log2(math.e)
N_ADA = 6
LANES = 128
MLA_QK_PAD = 2 * LANES
VMEM_LIMIT = 56 * 1024 * 1024
MM_SUB_COLS = 256

F32 = jnp.float32
CDT = jnp.bfloat16


def _params(sem):
    return pltpu.CompilerParams(dimension_semantics=sem, vmem_limit_bytes=VMEM_LIMIT)


def _tile(n, cands):
    for c in cands:
        if n % c == 0:
            return c
    return n


def _ada_body(c_ref, w_ref, b_ref, o_ref):
    c = c_ref[...]
    cond = c * (1.0 / (1.0 + jnp.exp(-c)))
    o_ref[...] = jnp.dot(cond.astype(CDT), w_ref[...].astype(CDT),
                         preferred_element_type=F32) + b_ref[...]


def _ada(c, w, b):
    bsz, d = c.shape
    n = w.shape[1]
    tn = _tile(n, (1024, 512, 256, 128))
    return pl.pallas_call(
        _ada_body,
        grid=(n // tn,),
        in_specs=[pl.BlockSpec((bsz, d), lambda j: (0, 0)),
                  pl.BlockSpec((d, tn), lambda j: (0, j)),
                  pl.BlockSpec((1, tn), lambda j: (0, j))],
        out_specs=pl.BlockSpec((bsz, tn), lambda j: (0, j)),
        out_shape=jax.ShapeDtypeStruct((bsz, n), F32),
        compiler_params=_params(("arbitrary",)),
        name="ada",
    )(c, w, b.reshape(1, n))


def _modcast_body(x_ref, sc_ref, sh_ref, o_ref):
    o_ref[...] = (x_ref[...] * (1.0 + sc_ref[0]) + sh_ref[0]).astype(o_ref.dtype)


def _modcast(x2, sc, sh, seq):
    t, d = x2.shape
    ts = _tile(seq, (256, 128, 64, 8))
    per = seq // ts
    vec = pl.BlockSpec((1, 1, d), lambda i: (i // per, 0, 0))
    return pl.pallas_call(
        _modcast_body,
        grid=(t // ts,),
        in_specs=[pl.BlockSpec((ts, d), lambda i: (i, 0)), vec, vec],
        out_specs=pl.BlockSpec((ts, d), lambda i: (i, 0)),
        out_shape=jax.ShapeDtypeStruct((t, d), CDT),
        compiler_params=_params(("arbitrary",)),
        name="modcast",
    )(x2, sc[:, None, :], sh[:, None, :])


def _mm_body(*refs, na, nex, prologue, epilogue, tn, head_tiles):
    a_refs = refs[:na]
    b_ref = refs[na]
    ex = refs[na + 1:na + 1 + nex]
    o_ref = refs[na + 1 + nex]
    j = pl.program_id(1)
    if prologue is not None:
        a_sc = refs[na + 2 + nex]

        @pl.when(j == 0)
        def _():
            a_sc[...] = prologue(a_refs[0][...], ex).astype(a_sc.dtype)

        a_refs = (a_sc,)
    sub = MM_SUB_COLS if tn % MM_SUB_COLS == 0 else tn

    def store(s, y):
        if head_tiles is None:
            o_ref[:, s * sub:(s + 1) * sub] = y.astype(o_ref.dtype)
        else:
            for hh in range(sub // HEAD_DIM):
                for tt in range(y.shape[0] // head_tiles):
                    blk = y[tt * head_tiles:(tt + 1) * head_tiles, hh * HEAD_DIM:(hh + 1) * HEAD_DIM]
                    o_ref[0, s * (sub // HEAD_DIM) + hh, tt] = blk.T.astype(o_ref.dtype)

    def run(fn):
        for s in range(tn // sub):
            cols = slice(s * sub, (s + 1) * sub)
            acc = None
            off = 0
            for r in a_refs:
                kr = r.shape[1]
                part = jnp.dot(r[...], b_ref[off:off + kr, cols], preferred_element_type=F32)
                acc = part if acc is None else acc + part
                off += kr
            store(s, fn(acc, j, ex))

    if epilogue is None:
        run(lambda acc, j, ex: acc)
    else:
        for cond, fn in epilogue(j):
            pl.when(cond)(functools.partial(run, fn))


def _mm(a_list, b, *, tm, tn, out_dtype, name, a_specs=None, prologue=None, epilogue=None,
        extras=(), head_tiles=None, seq=None):
    m = a_list[0].shape[0]
    k, n = b.shape
    if a_specs is None:
        a_specs = [pl.BlockSpec((tm, a.shape[1]), lambda i, j: (i, 0)) for a in a_list]
    ex_arrays = [e[0] for e in extras]
    ex_specs = [e[1] for e in extras]
    scratch = [pltpu.VMEM((tm, k), CDT)] if prologue is not None else []
    body = functools.partial(_mm_body, na=len(a_list), nex=len(extras), prologue=prologue,
                             epilogue=epilogue, tn=tn, head_tiles=head_tiles)
    if head_tiles is None:
        out_specs = pl.BlockSpec((tm, tn), lambda i, j: (i, j))
        out_shape = jax.ShapeDtypeStruct((m, n), out_dtype)
    else:
        per = seq // tm
        out_specs = pl.BlockSpec((1, tn // HEAD_DIM, tm // head_tiles, HEAD_DIM, head_tiles),
                                 lambda i, j: (i // per, j, i % per, 0, 0))
        out_shape = jax.ShapeDtypeStruct((m // seq, n // HEAD_DIM, seq // head_tiles, HEAD_DIM, head_tiles),
                                         out_dtype)
    return pl.pallas_call(
        body,
        grid=(m // tm, n // tn),
        in_specs=a_specs + [pl.BlockSpec((k, tn), lambda i, j: (0, j))] + ex_specs,
        out_specs=out_specs,
        out_shape=out_shape,
        scratch_shapes=scratch,
        compiler_params=_params(("arbitrary", "arbitrary")),
        name=name,
    )(*a_list, b, *ex_arrays)


def _rot(acc, cos, sa, sb, shift):
    tn = acc.shape[1]
    reps = tn // cos.shape[1]

    def rep(t):
        return t if reps == 1 else jnp.concatenate([t] * reps, axis=1)

    return (acc * rep(cos) + pltpu.roll(acc, tn - shift, 1) * rep(sa)
            + pltpu.roll(acc, shift, 1) * rep(sb))


def _rope_epilogue(shift, lo, hi, n_scaled, scale):
    def roped(acc, j, ex):
        y = _rot(acc, ex[0][...], ex[1][...], ex[2][...], shift)
        return y * jnp.where(j < n_scaled, scale, 1.0).astype(F32)

    def epi(j):
        inside = (j >= lo) & (j < hi)
        return [(inside, roped), (jnp.logical_not(inside), lambda acc, j, ex: acc)]

    return epi


def _mla_q_epilogue(scale):
    assert MM_SUB_COLS == MLA_QK_PAD

    def fn(acc, j, ex):
        rot = _rot(acc[:, LANES:], ex[0][...], ex[1][...], ex[2][...], MLA_ROPE // 2)
        return jnp.concatenate([acc[:, :LANES], rot], axis=1) * scale

    return lambda j: [(j >= 0, fn)]


def _krope_gate_epilogue(j):
    def fn(acc, j, ex):
        y = _rot(acc, ex[0][...], ex[1][...], ex[2][...], MLA_ROPE // 2)
        lane = lax.broadcasted_iota(jnp.int32, acc.shape, 1)
        return jnp.where(lane < LANES, y, 1.0 / (1.0 + jnp.exp(-acc)))

    return [(j >= 0, fn)]


def _rms_prologue(eps):
    def pro(a, ex):
        g = ex[-1][...]
        y = a * lax.rsqrt(jnp.mean(a * a, axis=-1, keepdims=True) + eps)
        return y * g

    return pro


def _relu2_epilogue(j):
    def fn(acc, j, ex):
        r = jnp.maximum(acc, 0.0)
        return r * r

    return [(j >= 0, fn)]


def _mmk_body(a_ref, b_ref, o_ref):
    kk = pl.program_id(2)

    @pl.when(kk == 0)
    def _():
        o_ref[...] = jnp.dot(a_ref[...], b_ref[...], preferred_element_type=F32)

    @pl.when(kk > 0)
    def _():
        o_ref[...] += jnp.dot(a_ref[...], b_ref[...], preferred_element_type=F32)


def _mmk(a, b, *, tm, tn, tk, name):
    m, k = a.shape
    n = b.shape[1]
    return pl.pallas_call(
        _mmk_body,
        grid=(m // tm, n // tn, k // tk),
        in_specs=[pl.BlockSpec((tm, tk), lambda i, j, kk: (i, kk)),
                  pl.BlockSpec((tk, tn), lambda i, j, kk: (kk, j))],
        out_specs=pl.BlockSpec((tm, tn), lambda i, j, kk: (i, j)),
        out_shape=jax.ShapeDtypeStruct((m, n), F32),
        compiler_params=_params(("arbitrary", "arbitrary", "arbitrary")),
        name=name,
    )(a, b)


def _ln_body(x_ref, a_ref, gate_ref, g_ref, b_ref, *rest, alpha, with_mod):
    y = alpha * x_ref[...] + (1.0 + gate_ref[0]) * a_ref[...]
    mu = jnp.mean(y, axis=-1, keepdims=True)
    yc = y - mu
    var = jnp.mean(yc * yc, axis=-1, keepdims=True)
    out = yc * lax.rsqrt(var + 1e-5) * g_ref[...] + b_ref[...]
    if with_mod:
        sc_ref, sh_ref, o_ref, h_ref = rest
        o_ref[...] = out
        h_ref[...] = (out * (1.0 + sc_ref[0]) + sh_ref[0]).astype(h_ref.dtype)
    else:
        (o_ref,) = rest
        o_ref[...] = out


def _res_ln(x2, a2, gate, g, b, seq, alpha, mod=None):
    t, d = x2.shape
    ts = _tile(seq, (256, 128, 64, 8))
    per = seq // ts
    row = pl.BlockSpec((ts, d), lambda i: (i, 0))
    vec = pl.BlockSpec((1, 1, d), lambda i: (i // per, 0, 0))
    par = pl.BlockSpec((1, d), lambda i: (0, 0))
    ins = [x2, a2, gate[:, None, :], g.reshape(1, d), b.reshape(1, d)]
    specs = [row, row, vec, par, par]
    if mod is None:
        out_shape = jax.ShapeDtypeStruct((t, d), F32)
        out_specs = row
    else:
        ins += [mod[0][:, None, :], mod[1][:, None, :]]
        specs += [vec, vec]
        out_shape = (jax.ShapeDtypeStruct((t, d), F32), jax.ShapeDtypeStruct((t, d), CDT))
        out_specs = (row, row)
    return pl.pallas_call(
        functools.partial(_ln_body, alpha=alpha, with_mod=mod is not None),
        grid=(t // ts,),
        in_specs=specs,
        out_specs=out_specs,
        out_shape=out_shape,
        compiler_params=_params(("arbitrary",)),
        name="res_ln",
    )(*ins)


def _compress_body(x_ref, pos_ref, w1_ref, w2_ref, o_ref):
    nc = o_ref.shape[2]
    acc_a = None
    acc_b = None
    for l in range(CMP_STRIDE):
        x = x_ref[pl.ds(l, nc, stride=CMP_STRIDE), :]
        xa = (x + pos_ref[l:l + 1, :]).astype(CDT)
        xb = (x + pos_ref[CMP_STRIDE + l:CMP_STRIDE + l + 1, :]).astype(CDT)
        wa = w1_ref[l * HEAD_DIM:(l + 1) * HEAD_DIM, :]
        wb = w1_ref[(CMP_STRIDE + l) * HEAD_DIM:(CMP_STRIDE + l + 1) * HEAD_DIM, :]
        pa = jnp.dot(xa, wa, preferred_element_type=F32)
        pb = jnp.dot(xb, wb, preferred_element_type=F32)
        acc_a = pa if acc_a is None else acc_a + pa
        acc_b = pb if acc_b is None else acc_b + pb
    pre = acc_a + pltpu.roll(acc_b, nc - 1, 0)
    hid = 0.5 * pre * (1.0 + jnp.tanh(math.sqrt(2.0 / math.pi) * (pre + 0.044715 * pre * pre * pre)))
    o_ref[0, 0] = jnp.dot(hid.astype(CDT), w2_ref[...], preferred_element_type=F32).astype(o_ref.dtype)


def _compress(x2, col_blk0, pos, w1, w2, *, bsz, seq, g_groups):
    nc = seq // CMP_STRIDE
    full = lambda shape: pl.BlockSpec(shape, lambda b, gi: (0,) * len(shape))
    return pl.pallas_call(
        _compress_body,
        grid=(bsz, g_groups),
        in_specs=[pl.BlockSpec((seq, HEAD_DIM), lambda b, gi: (b, col_blk0 + gi)),
                  full(pos.shape), full(w1.shape), full(w2.shape)],
        out_specs=pl.BlockSpec((1, 1, nc, HEAD_DIM), lambda b, gi: (b, gi, 0, 0)),
        out_shape=jax.ShapeDtypeStruct((bsz, g_groups, nc, HEAD_DIM), CDT),
        compiler_params=_params(("arbitrary", "arbitrary")),
        name="compress",
    )(x2, pos, w1.astype(CDT), w2.astype(CDT))


def _cmp_branch(qs, kc, vct, ovt, gates_t, gi, i, *, tq, r_heads, nb):
    nc = kc.shape[0]
    cols = r_heads * tq
    s = lax.dot_general(kc, qs, (((1,), (1,)), ((), ())), preferred_element_type=F32)
    cidx = lax.broadcasted_iota(jnp.int32, (nc, cols), 0)
    tok = i * tq + (lax.broadcasted_iota(jnp.int32, (nc, cols), 1) & (tq - 1))
    valid = cidx * CMP_STRIDE + (CMP_LEN - 1) <= tok
    s = jnp.where(valid, s, NEG)
    e = jnp.exp2(s - jnp.max(s, axis=0, keepdims=True))
    p = jnp.where(valid, e, 0.0) * (1.0 / jnp.sum(e, axis=0, keepdims=True))
    o_t = jnp.dot(vct, p.astype(CDT), preferred_element_type=F32)
    rid = lax.broadcasted_iota(jnp.int32, gates_t.shape, 0)
    psum = None
    gated = []
    for r in range(r_heads):
        cs = slice(r * tq, (r + 1) * tq)
        col = (gi * r_heads + r) * 3
        grow = jnp.sum(jnp.where(rid == col, gates_t, 0.0), axis=0, keepdims=True)
        gated.append(o_t[:, cs] * grow)
        psum = p[:, cs] if psum is None else psum + p[:, cs]

    p_hi = psum.astype(CDT)
    p_lo = (psum - p_hi.astype(F32)).astype(CDT)
    imp = (jnp.dot(ovt, p_hi, preferred_element_type=F32)
           + jnp.dot(ovt, p_lo, preferred_element_type=F32))
    jj = lax.broadcasted_iota(jnp.int32, (LANES, tq), 0)
    cur = (i * tq + lax.broadcasted_iota(jnp.int32, (LANES, tq), 1)) // SEL_LEN
    forced = (jj == 0) | ((jj <= cur) & (jj > cur - N_LOCAL_FORCED))
    imp = jnp.where(jj > cur, NEG, imp + jnp.where(forced, FORCED_BONUS, 0.0))
    v = imp[:nb]
    ridx = lax.broadcasted_iota(jnp.int32, (nb, tq), 0)
    sel = jnp.zeros((nb, tq), F32)
    for _ in range(min(N_SEL, nb)):
        mx = jnp.max(v, axis=0, keepdims=True)
        first = jnp.min(jnp.where(v == mx, ridx, nb), axis=0, keepdims=True)
        hit = ridx == first
        sel = jnp.where(hit, 1.0, sel)
        v = jnp.where(hit, -jnp.inf, v)
    bias_t = jnp.where(sel > 0.0, 0.0, NEG)
    if nb < LANES:
        bias_t = jnp.concatenate([bias_t, jnp.zeros((LANES - nb, tq), F32)], axis=0)
    return jnp.concatenate(gated, axis=1), bias_t.T.astype(CDT)


ACC_ROWS = HEAD_DIM + 16


def _with_ones_row(vt):
    ones_row = jnp.where(lax.broadcasted_iota(jnp.int32, (ACC_ROWS - HEAD_DIM, vt.shape[1]), 0) == 0,
                         1.0, 0.0).astype(CDT)
    return jnp.concatenate([vt, ones_row], axis=0)


def _flash_init(m_ref, acc_ref):
    m_ref[...] = jnp.full(m_ref.shape, -jnp.inf, F32)
    acc_ref[...] = jnp.zeros(acc_ref.shape, F32)


def _nsa_body(q_ref, kc_ref, vct_ref, ovt_ref, ks_ref, e_ref, vs_ref, kw_ref, vw_ref, gate_ref, o_ref,
              qc_ref, oc_ref, s0_ref, s1_ref, cm_ref, m_ref, acc_ref, *, tq, r_heads, n_win, nb):
    s_refs = (s0_ref, s1_ref)
    gi = pl.program_id(1)
    i = pl.program_id(2)
    tk = tq
    cols = r_heads * tq
    for r in range(r_heads):
        qc_ref[r * tq:(r + 1) * tq, :HEAD_DIM] = q_ref[0, :, r * HEAD_DIM:(r + 1) * HEAD_DIM]
    _flash_init(m_ref, acc_ref)

    def local_iotas():
        key = lax.broadcasted_iota(jnp.int32, (tk, cols), 0)
        tok = lax.broadcasted_iota(jnp.int32, (tk, cols), 1) & (tq - 1)
        return key, tok

    def put(s, buf):
        s_refs[buf][...] = s
        cm_ref[buf] = jnp.max(s, axis=0, keepdims=True)

    def update(branch, vt, buf):
        m_prev = m_ref[branch]
        m_new = jnp.maximum(m_prev, cm_ref[buf])
        p = jnp.exp2(s_refs[buf][...] - m_new).astype(CDT)
        acc_ref[branch] = (jnp.exp2(m_prev - m_new) * acc_ref[branch]
                           + jnp.dot(_with_ones_row(vt), p, preferred_element_type=F32))
        m_ref[branch] = m_new

    def sel_scores(j, buf, masked):
        off = pl.multiple_of(j * tk, tk)
        kmat = jnp.concatenate([ks_ref[0, pl.ds(off, tk), :], e_ref[pl.ds(off, tk), :]], axis=1)
        s = lax.dot_general(kmat, qc_ref[...], (((1,), (1,)), ((), ())), preferred_element_type=F32)
        if masked:
            key, tok = local_iotas()
            if masked == "if_diagonal":
                tok = tok + jnp.where(j == i, 0, tk)
            s = jnp.where(key <= tok, s, NEG)
        put(s, buf)

    def sel_update(j, buf):
        update(0, vs_ref[0, 0, j], buf)

    def win_tile(dd):
        back = n_win - 1 - dd
        return back, jnp.maximum(i - back, 0)

    def win_scores(dd, buf):
        back, j = win_tile(dd)
        off = pl.multiple_of(j * tk, tk)
        s = lax.dot_general(kw_ref[0, pl.ds(off, tk), :], qc_ref[:, :HEAD_DIM], (((1,), (1,)), ((), ())),
                            preferred_element_type=F32)
        key, tok = local_iotas()
        if back * tk - (tk - 1) < 0:
            s = jnp.where(key - back * tk <= tok, s, NEG)
        if back > 0 or tq - 1 + back * tk >= WINDOW:
            reach = WINDOW - back * tk
            if back > 0:
                reach = reach - jnp.where(i >= back, 0, 2 * WINDOW + tq)
            s = jnp.where(tok < key + reach, s, NEG)
        put(s, buf)

    def win_update(dd, buf):
        update(1, vw_ref[0, 0, win_tile(dd)[1]], buf)

    win_scores(0, 0)
    if n_win > 1:
        win_scores(1, 1)
    oc_ref[...], bias = _cmp_branch(qc_ref[:, :HEAD_DIM], kc_ref[0, 0], vct_ref[0, 0], ovt_ref[...],
                                    gate_ref[0].T, gi, i, tq=tq, r_heads=r_heads, nb=nb)
    for r in range(r_heads):
        qc_ref[r * tq:(r + 1) * tq, HEAD_DIM:] = bias
    for dd in range(n_win):
        if dd == n_win - 1 and n_win % 2 == 0:
            sel_scores(0, 0, "if_diagonal")
        win_update(dd, dd % 2)
        if dd + 2 < n_win:
            win_scores(dd + 2, dd % 2)
    if n_win % 2 == 1:
        sel_scores(0, 0, "if_diagonal")

    n_pairs = jnp.maximum(i - 1, 0) // 2

    def body(jp, carry):
        j = 2 * jp
        sel_scores(j + 1, 1, False)
        sel_update(j, 0)
        sel_scores(j + 2, 0, False)
        sel_update(j + 1, 1)
        return carry

    lax.fori_loop(0, n_pairs, body, 0)
    j0 = 2 * n_pairs
    left = i - j0

    @pl.when(left == 1)
    def _():
        sel_scores(i, 1, True)
        sel_update(j0, 0)
        sel_update(i, 1)

    @pl.when(left == 2)
    def _():
        sel_scores(j0 + 1, 1, False)
        sel_update(j0, 0)
        sel_scores(i, 0, True)
        sel_update(j0 + 1, 1)
        sel_update(i, 0)

    @pl.when(left == 0)
    def _():
        sel_update(0, 0)

    gates_t = gate_ref[0].T
    rid = lax.broadcasted_iota(jnp.int32, gates_t.shape, 0)
    for r in range(r_heads):
        cs = slice(r * tq, (r + 1) * tq)
        o = oc_ref[:, cs]
        for branch in (0, 1):
            acc = acc_ref[branch]
            col = (gi * r_heads + r) * 3 + 1 + branch
            grow = jnp.sum(jnp.where(rid == col, gates_t, 0.0), axis=0, keepdims=True)
            o = o + acc[:HEAD_DIM, cs] * (grow / acc[HEAD_DIM:HEAD_DIM + 1, cs])
        o_ref[0, :, r * HEAD_DIM:(r + 1) * HEAD_DIM] = o.T.astype(o_ref.dtype)


def _nsa_attn(qkv, ks_blk0, kw_blk0, vt, gates, kc, vct, overlap_t, e_mat, *, g_groups, r_heads, seq):
    bsz = qkv.shape[0]
    qw = r_heads * HEAD_DIM
    tq = tk = vt.shape[-1]
    cols = r_heads * tq
    nc = kc.shape[2]
    kspec = lambda blk0: pl.BlockSpec((1, seq, HEAD_DIM), lambda b, g, i: (b, 0, blk0 + g))
    vspec = lambda h0: pl.BlockSpec((1, 1) + vt.shape[2:], lambda b, g, i: (b, h0 + g, 0, 0, 0))
    qspec = pl.BlockSpec((1, tq, qw), lambda b, g, i: (b, i, g))
    body = functools.partial(_nsa_body, tq=tq, r_heads=r_heads, n_win=(WINDOW + tk - 1) // tk + 1,
                             nb=seq // SEL_LEN)
    return pl.pallas_call(
        body,
        grid=(bsz, g_groups, seq // tq),
        in_specs=[qspec,
                  pl.BlockSpec((1, 1, nc, HEAD_DIM), lambda b, g, i: (b, g, 0, 0)),
                  pl.BlockSpec((1, 1, HEAD_DIM, nc), lambda b, g, i: (b, g, 0, 0)),
                  pl.BlockSpec((LANES, nc), lambda b, g, i: (0, 0)),
                  kspec(ks_blk0), pl.BlockSpec((seq, LANES), lambda b, g, i: (0, 0)), vspec(0),
                  kspec(kw_blk0), vspec(g_groups),
                  pl.BlockSpec((1, tq, LANES), lambda b, g, i: (b, i, 1))],
        out_specs=qspec,
        out_shape=jax.ShapeDtypeStruct((bsz, seq, g_groups * qw), CDT),
        scratch_shapes=[pltpu.VMEM((cols, 2 * HEAD_DIM), CDT),
                        pltpu.VMEM((HEAD_DIM, cols), F32),
                        pltpu.VMEM((tk, cols), F32), pltpu.VMEM((tk, cols), F32),
                        pltpu.VMEM((2, 1, cols), F32),
                        pltpu.VMEM((2, 1, cols), F32),
                        pltpu.VMEM((2, ACC_ROWS, cols), F32)],
        compiler_params=_params(("arbitrary",) * 3),
        name="nsa_attn",
    )(qkv, kc, vct, overlap_t, qkv, e_mat, vt, qkv, vt, gates)


def _mla_body(q_ref, kn_ref, kr_ref, vt_ref, o_ref, *rest, tq, tk, hg):
    s_refs = rest[:hg]
    cm_ref, m_ref, acc_ref = rest[hg:]
    i = pl.program_id(2)
    _flash_init(m_ref, acc_ref)
    n_full = (i * tq) // tk

    def step(j, mask, c0):
        off = pl.multiple_of(j * tk, tk)
        kr = kr_ref[0, pl.ds(off, tk), :]

        def scores(c):
            kmat = jnp.concatenate([kn_ref[0, pl.ds(off, tk), c * MLA_NOPE:(c + 1) * MLA_NOPE], kr], axis=1)
            s = lax.dot_general(kmat, q_ref[0, c0:, c * MLA_QK_PAD:(c + 1) * MLA_QK_PAD],
                                (((1,), (1,)), ((), ())), preferred_element_type=F32)
            if mask is not None:
                s = jnp.where(mask, s, NEG)
            s_refs[c][:, c0:] = s
            cm_ref[c, :, c0:] = jnp.max(s, axis=0, keepdims=True)

        def update(c):
            m_prev = m_ref[c, :, c0:]
            m_new = jnp.maximum(m_prev, cm_ref[c, :, c0:])
            p = jnp.exp2(s_refs[c][:, c0:] - m_new).astype(CDT)
            acc_ref[c, :, c0:] = (jnp.exp2(m_prev - m_new) * acc_ref[c, :, c0:]
                                  + jnp.dot(_with_ones_row(vt_ref[0, c, j]), p, preferred_element_type=F32))
            m_ref[c, :, c0:] = m_new

        scores(0)
        for c in range(1, hg):
            scores(c)
            update(c - 1)
        update(hg - 1)

    def full_tile(j, carry):
        step(j, None, 0)
        return carry

    lax.fori_loop(0, n_full, full_tile, 0)
    for dd in range(tq // tk):
        w = tq - dd * tk
        mask = (lax.broadcasted_iota(jnp.int32, (tk, w), 0) <= lax.broadcasted_iota(jnp.int32, (tk, w), 1))
        step(n_full + dd, mask, dd * tk)
    for c in range(hg):
        acc = acc_ref[c]
        o = acc[:HEAD_DIM] * (1.0 / acc[HEAD_DIM:HEAD_DIM + 1])
        o_ref[0, :, c * MLA_V:(c + 1) * MLA_V] = o.T.astype(o_ref.dtype)


def _mla_attn(q, kv, kr, vt, *, heads, seq):
    bsz = q.shape[0]
    tk = vt.shape[-1]
    tq = _tile(seq, (1024, 512, 256, 128))
    hg = 4 if heads % 4 == 0 else 1
    body = functools.partial(_mla_body, tq=tq, tk=tk, hg=hg)
    return pl.pallas_call(
        body,
        grid=(bsz, heads // hg, seq // tq),
        in_specs=[pl.BlockSpec((1, tq, hg * MLA_QK_PAD), lambda b, h, i: (b, i, h)),
                  pl.BlockSpec((1, seq, hg * MLA_NOPE), lambda b, h, i: (b, 0, h)),
                  pl.BlockSpec((1, seq, LANES), lambda b, h, i: (b, 0, 0)),
                  pl.BlockSpec((1, hg) + vt.shape[2:], lambda b, h, i: (b, h, 0, 0, 0))],
        out_specs=pl.BlockSpec((1, tq, hg * MLA_V), lambda b, h, i: (b, i, h)),
        out_shape=jax.ShapeDtypeStruct((bsz, seq, heads * MLA_V), CDT),
        scratch_shapes=[pltpu.VMEM((tk, tq), F32)] * hg + [
            pltpu.VMEM((hg, 1, tq), F32), pltpu.VMEM((hg, 1, tq), F32), pltpu.VMEM((hg, ACC_ROWS, tq), F32)],
        compiler_params=_params(("arbitrary",) * 3),
        name="mla_attn",
    )(q, kv, kr, vt)


def _rope_tables(positions, rot_dim, period):
    half = rot_dim // 2
    inv = jnp.power(ROPE_THETA, -jnp.arange(0, rot_dim, 2, dtype=F32) / rot_dim)
    ang = positions.astype(F32).reshape(-1)[:, None] * inv
    cos, sin = jnp.cos(ang), jnp.sin(ang)
    t = ang.shape[0]
    rest = period - rot_dim
    ct = jnp.concatenate([cos, cos, jnp.ones((t, rest), F32)], axis=1)
    sa = jnp.concatenate([-sin, jnp.zeros((t, half + rest), F32)], axis=1)
    sb = jnp.concatenate([jnp.zeros((t, half), F32), sin, jnp.zeros((t, rest), F32)], axis=1)
    return ct, sa, sb


def _table_extras(tables, tm):
    return [(tb, pl.BlockSpec((tm, tb.shape[1]), lambda i, j: (i, 0))) for tb in tables]


def _layer(x2, mod, positions, w_in, pos_k, pos_v, k1, k2, v1, v2, q_norm, kv_norm, w_uq, w_ukv,
           w_out, ln1_g, ln1_b, w_ff1, w_ff2, ln2_g, ln2_b, *, bsz, seq, alpha):
    t, d = x2.shape
    n_heads = d // HEAD_DIM
    nsa_h = n_heads // 2
    mla_h = n_heads - nsa_h
    q_rank = q_norm.shape[0]
    kv_rank = kv_norm.shape[0]
    d_in = w_in.shape[1]
    g_groups = (d_in - nsa_h * HEAD_DIM - nsa_h * 3 - q_rank - kv_rank - MLA_ROPE) // (6 * HEAD_DIM)
    r_heads = nsa_h // g_groups
    gw = g_groups * HEAD_DIM
    qw = nsa_h * HEAD_DIM
    sh_a, sc_a, g_a, sh_m, sc_m, g_m = jnp.split(mod, N_ADA, axis=-1)

    o_q, o_kv = 0, qw
    o_gate = o_kv + 6 * gw
    o_cq = o_gate + nsa_h * 3
    o_ckv = o_cq + q_rank
    o_kr = o_ckv + kv_rank
    kvcol = lambda idx: w_in[:, o_kv + idx * gw:o_kv + (idx + 1) * gw]
    w1c = jnp.concatenate([w_in[:, o_q:o_q + qw], kvcol(2), kvcol(4)], axis=1).astype(CDT)
    wvc = jnp.concatenate([kvcol(3), kvcol(5)], axis=1).astype(CDT)
    w2c = jnp.concatenate([w_in[:, o_cq:o_cq + q_rank], w_in[:, o_ckv:o_ckv + kv_rank], kvcol(0),
                           kvcol(1)], axis=1).astype(CDT)
    zpad = lambda n: jnp.zeros((d, n), w_in.dtype)
    w3c = jnp.concatenate([w_in[:, o_kr:o_kr + MLA_ROPE], zpad(LANES - MLA_ROPE),
                           w_in[:, o_gate:o_gate + nsa_h * 3], zpad(LANES - nsa_h * 3)],
                          axis=1).astype(CDT)

    h = _modcast(x2, sc_a, sh_a, seq)

    tm = _tile(t, (1024, 512, 256, 128))
    nsa_tabs = _rope_tables(positions, PARTIAL_ROT, LANES)
    n1 = w1c.shape[1]
    tn1 = _tile(math.gcd(qw, 2 * gw), (1024, 512, 256, 128))
    scale = HEAD_DIM ** -0.5 * LOG2E
    p1 = _mm([h], w1c, tm=tm, tn=tn1, out_dtype=CDT, name="in_proj_rot",
             epilogue=_rope_epilogue(PARTIAL_ROT // 2, 0, n1 // tn1, qw // tn1, scale),
             extras=_table_extras(nsa_tabs, tm))
    tk = _tile(seq, (512, 256, 128))
    vt_nsa = _mm([h], wvc, tm=tm, tn=_tile(2 * gw, (1024, 512, 256, 128)), out_dtype=CDT, name="in_proj_vt",
                 head_tiles=tk, seq=seq)
    n2 = w2c.shape[1]
    lat = q_rank + kv_rank
    tn2 = _tile(math.gcd(lat, gw), (512, 256, 128))
    p2 = _mm([h], w2c, tm=tm, tn=tn2, out_dtype=F32, name="in_proj_lat",
             epilogue=_rope_epilogue(PARTIAL_ROT // 2, lat // tn2, (lat + gw) // tn2, 0, 1.0),
             extras=_table_extras(nsa_tabs, tm))
    kr_tabs = _rope_tables(positions, MLA_ROPE, LANES)
    p3 = _mm([h], w3c, tm=tm, tn=2 * LANES, out_dtype=F32, name="in_proj_kr_gate",
             epilogue=_krope_gate_epilogue, extras=_table_extras(kr_tabs, tm))

    qkv = p1.reshape(bsz, seq, n1)
    gates = p3.reshape(bsz, seq, 2 * LANES)

    nchunk = seq // CMP_STRIDE
    ncmp = (seq - CMP_LEN) // CMP_STRIDE + 1
    blk = lambda col: col // HEAD_DIM
    kc = _compress(p2, blk(lat), pos_k, k1, k2, bsz=bsz, seq=seq, g_groups=g_groups)
    vc = _compress(p2, blk(lat + gw), pos_v, v1, v2, bsz=bsz, seq=seq, g_groups=g_groups)

    nb = seq // SEL_LEN
    c_start = np.arange(nchunk) * CMP_STRIDE
    b_start = np.arange(LANES) * SEL_LEN
    overlap = ((c_start[:, None] < b_start[None, :] + SEL_LEN) &
               (c_start[:, None] + CMP_LEN > b_start[None, :]) &
               (np.arange(nchunk)[:, None] < ncmp) & (np.arange(LANES)[None, :] < nb))
    overlap_t = jnp.asarray(overlap.T.astype(np.float32)).astype(CDT)
    e_mat = jnp.asarray((np.arange(seq)[:, None] // SEL_LEN == np.arange(LANES)[None, :])
                        .astype(np.float32)).astype(CDT)
    o_nsa = _nsa_attn(qkv, blk(qw), blk(qw + gw), vt_nsa, gates, kc, vc.transpose(0, 1, 3, 2),
                      overlap_t, e_mat, g_groups=g_groups, r_heads=r_heads, seq=seq)

    qk_dim = MLA_NOPE + MLA_ROPE
    wq = w_uq.reshape(q_rank, mla_h, qk_dim)
    wq = jnp.concatenate([wq, jnp.zeros((q_rank, mla_h, MLA_QK_PAD - qk_dim), wq.dtype)], axis=-1)
    wq = wq.reshape(q_rank, mla_h * MLA_QK_PAD).astype(CDT)
    wkv = w_ukv.reshape(kv_rank, mla_h, MLA_NOPE + MLA_V)
    wk = wkv[:, :, :MLA_NOPE].reshape(kv_rank, mla_h * MLA_NOPE).astype(CDT)
    wv = wkv[:, :, MLA_NOPE:].reshape(kv_rank, mla_h * MLA_V).astype(CDT)
    tnq = _tile(mla_h * MLA_QK_PAD, (1024, 512, 256))
    q_mla = _mm([p2], wq, tm=tm, tn=tnq, out_dtype=CDT, name="mla_uq",
                a_specs=[pl.BlockSpec((tm, q_rank), lambda i, j: (i, 0))],
                prologue=_rms_prologue(1e-6),
                epilogue=_mla_q_epilogue(qk_dim ** -0.5 * LOG2E),
                extras=_table_extras(kr_tabs, tm)
                + [(q_norm.reshape(1, q_rank), pl.BlockSpec((1, q_rank), lambda i, j: (0, 0)))])
    tnkv = _tile(wk.shape[1], (1024, 512, 256, 128))
    ckv_args = dict(tm=tm, tn=tnkv, out_dtype=CDT, prologue=_rms_prologue(1e-6),
                    a_specs=[pl.BlockSpec((tm, kv_rank), lambda i, j: (i, q_rank // kv_rank))],
                    extras=[(kv_norm.reshape(1, kv_rank), pl.BlockSpec((1, kv_rank), lambda i, j: (0, 0)))])
    k_mla = _mm([p2], wk, name="mla_uk", **ckv_args)
    vt_mla = _mm([p2], wv, name="mla_uv", head_tiles=tk, seq=seq, **ckv_args)
    o_mla = _mla_attn(q_mla.reshape(bsz, seq, -1), k_mla.reshape(bsz, seq, -1),
                      gates[:, :, :LANES].astype(CDT), vt_mla, heads=mla_h, seq=seq)

    tno = _tile(d, (1024, 512, 256, 128))
    a = _mm([o_nsa.reshape(t, qw), o_mla.reshape(t, mla_h * MLA_V)], w_out.astype(CDT), tm=tm, tn=tno,
            out_dtype=F32, name="out_proj")
    x1, h2 = _res_ln(x2, a, g_a, ln1_g, ln1_b, seq, alpha, mod=(sc_m, sh_m))
    d_ff = w_ff1.shape[1]
    f1 = _mm([h2], w_ff1.astype(CDT), tm=tm, tn=_tile(d_ff, (1024, 512, 256, 128)), out_dtype=CDT,
             name="ff1", epilogue=_relu2_epilogue)
    f2 = _mmk(f1, w_ff2.astype(CDT), tm=tm, tn=tno, tk=_tile(d_ff, (4096, 2048, 1024, 512)), name="ff2")
    return _res_ln(x1, f2, g_m, ln2_g, ln2_b, seq, alpha)


def kernel(x, c, positions, w_ada, b_ada, w_in, nsa_pos_k, nsa_pos_v, nsa_cmp_k1, nsa_cmp_k2, nsa_cmp_v1, nsa_cmp_v2, mla_q_norm, mla_kv_norm, mla_w_uq, mla_w_ukv, w_out, ln1_g, ln1_b, w_ff1, w_ff2, ln2_g, ln2_b):
    bsz, seq, d = x.shape
    depth = w_ada.shape[0]
    alpha = (2 * depth) ** 0.25
    x2 = x.reshape(bsz * seq, d)
    for layer in range(depth):
        mod = _ada(c, w_ada[layer], b_ada[layer])
        x2 = _layer(x2, mod, positions, w_in[layer], nsa_pos_k[layer], nsa_pos_v[layer],
                    nsa_cmp_k1[layer], nsa_cmp_k2[layer], nsa_cmp_v1[layer], nsa_cmp_v2[layer],
                    mla_q_norm[layer], mla_kv_norm[layer], mla_w_uq[layer], mla_w_ukv[layer],
                    w_out[layer], ln1_g[layer], ln1_b[layer], w_ff1[layer], w_ff2[layer],
                    ln2_g[layer], ln2_b[layer], bsz=bsz, seq=seq, alpha=alpha)
    return x2.reshape(bsz, seq, d)
```

```python
import functools
import math

import numpy as np
import jax
import jax.numpy as jnp
from jax import lax
from jax.experimental import pallas as pl
from jax.experimental.pallas import tpu as pltpu

HEAD_DIM = 128
CMP_LEN = 32
CMP_STRIDE = 16
SEL_LEN = 64
N_SEL = 16
N_LOCAL_FORCED = 2
FORCED_BONUS = 1e4
WINDOW = 512
MLA_NOPE = 128
MLA_ROPE = 64
MLA_V = 128
ROPE_THETA = 500000.0
PARTIAL_ROT = HEAD_DIM // 4
NEG = -1e30
LOG2E = math.log2(math.e)
N_ADA = 6
LANES = 128
MLA_QK_PAD = 2 * LANES
VMEM_LIMIT = 56 * 1024 * 1024
MM_SUB_COLS = 256

F32 = jnp.float32
CDT = jnp.bfloat16


def _params(sem):
    return pltpu.CompilerParams(dimension_semantics=sem, vmem_limit_bytes=VMEM_LIMIT)


def _tile(n, cands):
    for c in cands:
        if n % c == 0:
            return c
    return n


def _ada_body(c_ref, w_ref, b_ref, o_ref):
    c = c_ref[...]
    cond = c * (1.0 / (1.0 + jnp.exp(-c)))
    o_ref[...] = jnp.dot(cond.astype(CDT), w_ref[...].astype(CDT),
                         preferred_element_type=F32) + b_ref[...]


def _ada(c, w, b):
    bsz, d = c.shape
    n = w.shape[1]
    tn = _tile(n, (1024, 512, 256, 128))
    return pl.pallas_call(
        _ada_body,
        grid=(n // tn,),
        in_specs=[pl.BlockSpec((bsz, d), lambda j: (0, 0)),
                  pl.BlockSpec((d, tn), lambda j: (0, j)),
                  pl.BlockSpec((1, tn), lambda j: (0, j))],
        out_specs=pl.BlockSpec((bsz, tn), lambda j: (0, j)),
        out_shape=jax.ShapeDtypeStruct((bsz, n), F32),
        compiler_params=_params(("arbitrary",)),
        name="ada",
    )(c, w, b.reshape(1, n))


def _modcast_body(x_ref, sc_ref, sh_ref, o_ref):
    o_ref[...] = (x_ref[...] * (1.0 + sc_ref[0]) + sh_ref[0]).astype(o_ref.dtype)


def _modcast(x2, sc, sh, seq):
    t, d = x2.shape
    ts = _tile(seq, (256, 128, 64, 8))
    per = seq // ts
    vec = pl.BlockSpec((1, 1, d), lambda i: (i // per, 0, 0))
    return pl.pallas_call(
        _modcast_body,
        grid=(t // ts,),
        in_specs=[pl.BlockSpec((ts, d), lambda i: (i, 0)), vec, vec],
        out_specs=pl.BlockSpec((ts, d), lambda i: (i, 0)),
        out_shape=jax.ShapeDtypeStruct((t, d), CDT),
        compiler_params=_params(("arbitrary",)),
        name="modcast",
    )(x2, sc[:, None, :], sh[:, None, :])


def _mm_body(*refs, na, nex, prologue, epilogue, tn, head_tiles):
    a_refs = refs[:na]
    b_ref = refs[na]
    ex = refs[na + 1:na + 1 + nex]
    o_ref = refs[na + 1 + nex]
    j = pl.program_id(1)
    if prologue is not None:
        a_sc = refs[na + 2 + nex]

        @pl.when(j == 0)
        def _():
            a_sc[...] = prologue(a_refs[0][...], ex).astype(a_sc.dtype)

        a_refs = (a_sc,)
    sub = MM_SUB_COLS if tn % MM_SUB_COLS == 0 else tn

    def store(s, y):
        if head_tiles is None:
            o_ref[:, s * sub:(s + 1) * sub] = y.astype(o_ref.dtype)
        else:
            for hh in range(sub // HEAD_DIM):
                for tt in range(y.shape[0] // head_tiles):
                    blk = y[tt * head_tiles:(tt + 1) * head_tiles, hh * HEAD_DIM:(hh + 1) * HEAD_DIM]
                    o_ref[0, s * (sub // HEAD_DIM) + hh, tt] = blk.T.astype(o_ref.dtype)

    def run(fn):
        for s in range(tn // sub):
            cols = slice(s * sub, (s + 1) * sub)
            acc = None
            off = 0
            for r in a_refs:
                kr = r.shape[1]
                part = jnp.dot(r[...], b_ref[off:off + kr, cols], preferred_element_type=F32)
                acc = part if acc is None else acc + part
                off += kr
            store(s, fn(acc, j, ex, s))

    if epilogue is None:
        run(lambda acc, j, ex, s: acc)
    else:
        for cond, fn in epilogue(j):
            pl.when(cond)(functools.partial(run, fn))


def _mm(a_list, b, *, tm, tn, out_dtype, name, a_specs=None, prologue=None, epilogue=None,
        extras=(), head_tiles=None, seq=None):
    m = a_list[0].shape[0]
    k, n = b.shape
    if a_specs is None:
        a_specs = [pl.BlockSpec((tm, a.shape[1]), lambda i, j: (i, 0)) for a in a_list]
    ex_arrays = [e[0] for e in extras]
    ex_specs = [e[1] for e in extras]
    scratch = [pltpu.VMEM((tm, k), CDT)] if prologue is not None else []
    body = functools.partial(_mm_body, na=len(a_list), nex=len(extras), prologue=prologue,
                             epilogue=epilogue, tn=tn, head_tiles=head_tiles)
    if head_tiles is None:
        out_specs = pl.BlockSpec((tm, tn), lambda i, j: (i, j))
        out_shape = jax.ShapeDtypeStruct((m, n), out_dtype)
    else:
        per = seq // tm
        out_specs = pl.BlockSpec((1, tn // HEAD_DIM, tm // head_tiles, HEAD_DIM, head_tiles),
                                 lambda i, j: (i // per, j, i % per, 0, 0))
        out_shape = jax.ShapeDtypeStruct((m // seq, n // HEAD_DIM, seq // head_tiles, HEAD_DIM, head_tiles),
                                         out_dtype)
    return pl.pallas_call(
        body,
        grid=(m // tm, n // tn),
        in_specs=a_specs + [pl.BlockSpec((k, tn), lambda i, j: (0, j))] + ex_specs,
        out_specs=out_specs,
        out_shape=out_shape,
        scratch_shapes=scratch,
        compiler_params=_params(("arbitrary", "arbitrary")),
        name=name,
    )(*a_list, b, *ex_arrays)


def _rot(acc, cos, sa, sb, shift):
    tn = acc.shape[1]
    reps = tn // cos.shape[1]

    def rep(t):
        return t if reps == 1 else jnp.concatenate([t] * reps, axis=1)

    return (acc * rep(cos) + pltpu.roll(acc, tn - shift, 1) * rep(sa)
            + pltpu.roll(acc, shift, 1) * rep(sb))


def _rope_epilogue(shift, lo, hi, n_scaled, scale, part=None):
    def roped(acc, j, ex, s):
        y = _rot(acc, ex[0][...], ex[1][...], ex[2][...], shift)
        return y * jnp.where(j < n_scaled, scale, 1.0).astype(F32)

    def plain(acc, j, ex, s):
        return acc

    def mixed(acc, j, ex, s):
        return roped(acc, j, ex, s) if s < part[1] else acc

    def epi(j):
        inside = (j >= lo) & (j < hi)
        if part is None:
            return [(inside, roped), (jnp.logical_not(inside), plain)]
        is_part = j == part[0]
        return [(inside, roped), (is_part, mixed),
                (jnp.logical_not(inside | is_part), plain)]

    return epi


def _mla_q_epilogue(scale):
    assert MM_SUB_COLS == MLA_QK_PAD

    def fn(acc, j, ex, s):
        rot = _rot(acc[:, LANES:], ex[0][...], ex[1][...], ex[2][...], MLA_ROPE // 2)
        return jnp.concatenate([acc[:, :LANES], rot], axis=1) * scale

    return lambda j: [(j >= 0, fn)]


def _krope_gate_epilogue(j):
    def fn(acc, j, ex, s):
        y = _rot(acc, ex[0][...], ex[1][...], ex[2][...], MLA_ROPE // 2)
        lane = lax.broadcasted_iota(jnp.int32, acc.shape, 1)
        return jnp.where(lane < LANES, y, 1.0 / (1.0 + jnp.exp(-acc)))

    return [(j >= 0, fn)]


def _rms_prologue(eps):
    def pro(a, ex):
        g = ex[-1][...]
        y = a * lax.rsqrt(jnp.mean(a * a, axis=-1, keepdims=True) + eps)
        return y * g

    return pro


def _relu2_epilogue(j):
    def fn(acc, j, ex, s):
        r = jnp.maximum(acc, 0.0)
        return r * r

    return [(j >= 0, fn)]


def _mmk_body(a_ref, b_ref, o_ref):
    kk = pl.program_id(2)

    @pl.when(kk == 0)
    def _():
        o_ref[...] = jnp.dot(a_ref[...], b_ref[...], preferred_element_type=F32)

    @pl.when(kk > 0)
    def _():
        o_ref[...] += jnp.dot(a_ref[...], b_ref[...], preferred_element_type=F32)


def _mmk(a, b, *, tm, tn, tk, name):
    m, k = a.shape
    n = b.shape[1]
    return pl.pallas_call(
        _mmk_body,
        grid=(m // tm, n // tn, k // tk),
        in_specs=[pl.BlockSpec((tm, tk), lambda i, j, kk: (i, kk)),
                  pl.BlockSpec((tk, tn), lambda i, j, kk: (kk, j))],
        out_specs=pl.BlockSpec((tm, tn), lambda i, j, kk: (i, j)),
        out_shape=jax.ShapeDtypeStruct((m, n), F32),
        compiler_params=_params(("arbitrary", "arbitrary", "arbitrary")),
        name=name,
    )(a, b)


def _ln_body(x_ref, a_ref, gate_ref, g_ref, b_ref, *rest, alpha, with_mod):
    y = alpha * x_ref[...] + (1.0 + gate_ref[0]) * a_ref[...]
    mu = jnp.mean(y, axis=-1, keepdims=True)
    yc = y - mu
    var = jnp.mean(yc * yc, axis=-1, keepdims=True)
    out = yc * lax.rsqrt(var + 1e-5) * g_ref[...] + b_ref[...]
    if with_mod:
        sc_ref, sh_ref, o_ref, h_ref = rest
        o_ref[...] = out
        h_ref[...] = (out * (1.0 + sc_ref[0]) + sh_ref[0]).astype(h_ref.dtype)
    else:
        (o_ref,) = rest
        o_ref[...] = out


def _res_ln(x2, a2, gate, g, b, seq, alpha, mod=None):
    t, d = x2.shape
    ts = _tile(seq, (256, 128, 64, 8))
    per = seq // ts
    row = pl.BlockSpec((ts, d), lambda i: (i, 0))
    vec = pl.BlockSpec((1, 1, d), lambda i: (i // per, 0, 0))
    par = pl.BlockSpec((1, d), lambda i: (0, 0))
    ins = [x2, a2, gate[:, None, :], g.reshape(1, d), b.reshape(1, d)]
    specs = [row, row, vec, par, par]
    if mod is None:
        out_shape = jax.ShapeDtypeStruct((t, d), F32)
        out_specs = row
    else:
        ins += [mod[0][:, None, :], mod[1][:, None, :]]
        specs += [vec, vec]
        out_shape = (jax.ShapeDtypeStruct((t, d), F32), jax.ShapeDtypeStruct((t, d), CDT))
        out_specs = (row, row)
    return pl.pallas_call(
        functools.partial(_ln_body, alpha=alpha, with_mod=mod is not None),
        grid=(t // ts,),
        in_specs=specs,
        out_specs=out_specs,
        out_shape=out_shape,
        compiler_params=_params(("arbitrary",)),
        name="res_ln",
    )(*ins)


def _compress_body(x_ref, pos_ref, w1_ref, w2_ref, o_ref):
    nc = o_ref.shape[2]
    acc_a = None
    acc_b = None
    for l in range(CMP_STRIDE):
        x = x_ref[pl.ds(l, nc, stride=CMP_STRIDE), :]
        xa = (x + pos_ref[l:l + 1, :]).astype(CDT)
        xb = (x + pos_ref[CMP_STRIDE + l:CMP_STRIDE + l + 1, :]).astype(CDT)
        wa = w1_ref[l * HEAD_DIM:(l + 1) * HEAD_DIM, :]
        wb = w1_ref[(CMP_STRIDE + l) * HEAD_DIM:(CMP_STRIDE + l + 1) * HEAD_DIM, :]
        pa = jnp.dot(xa, wa, preferred_element_type=F32)
        pb = jnp.dot(xb, wb, preferred_element_type=F32)
        acc_a = pa if acc_a is None else acc_a + pa
        acc_b = pb if acc_b is None else acc_b + pb
    pre = acc_a + pltpu.roll(acc_b, nc - 1, 0)
    hid = 0.5 * pre * (1.0 + jnp.tanh(math.sqrt(2.0 / math.pi) * (pre + 0.044715 * pre * pre * pre)))
    o_ref[0, 0] = jnp.dot(hid.astype(CDT), w2_ref[...], preferred_element_type=F32).astype(o_ref.dtype)


def _compress(x2, col_blk0, pos, w1, w2, *, bsz, seq, g_groups):
    nc = seq // CMP_STRIDE
    full = lambda shape: pl.BlockSpec(shape, lambda b, gi: (0,) * len(shape))
    return pl.pallas_call(
        _compress_body,
        grid=(bsz, g_groups),
        in_specs=[pl.BlockSpec((seq, HEAD_DIM), lambda b, gi: (b, col_blk0 + gi)),
                  full(pos.shape), full(w1.shape), full(w2.shape)],
        out_specs=pl.BlockSpec((1, 1, nc, HEAD_DIM), lambda b, gi: (b, gi, 0, 0)),
        out_shape=jax.ShapeDtypeStruct((bsz, g_groups, nc, HEAD_DIM), CDT),
        compiler_params=_params(("arbitrary", "arbitrary")),
        name="compress",
    )(x2, pos, w1.astype(CDT), w2.astype(CDT))


def _cmp_branch(qs, kc, vct, ovt, gates_t, gi, i, *, tq, r_heads, nb):
    nc = kc.shape[0]
    cols = r_heads * tq
    s = lax.dot_general(kc, qs, (((1,), (1,)), ((), ())), preferred_element_type=F32)
    cidx = lax.broadcasted_iota(jnp.int32, (nc, cols), 0)
    tok = i * tq + (lax.broadcasted_iota(jnp.int32, (nc, cols), 1) & (tq - 1))
    valid = cidx * CMP_STRIDE + (CMP_LEN - 1) <= tok
    s = jnp.where(valid, s, NEG)
    e = jnp.exp2(s - jnp.max(s, axis=0, keepdims=True))
    p = jnp.where(valid, e, 0.0) * (1.0 / jnp.sum(e, axis=0, keepdims=True))
    o_t = jnp.dot(vct, p.astype(CDT), preferred_element_type=F32)
    rid = lax.broadcasted_iota(jnp.int32, gates_t.shape, 0)
    psum = None
    gated = []
    for r in range(r_heads):
        cs = slice(r * tq, (r + 1) * tq)
        col = (gi * r_heads + r) * 3
        grow = jnp.sum(jnp.where(rid == col, gates_t, 0.0), axis=0, keepdims=True)
        gated.append(o_t[:, cs] * grow)
        psum = p[:, cs] if psum is None else psum + p[:, cs]

    p_hi = psum.astype(CDT)
    p_lo = (psum - p_hi.astype(F32)).astype(CDT)
    imp = (jnp.dot(ovt, p_hi, preferred_element_type=F32)
           + jnp.dot(ovt, p_lo, preferred_element_type=F32))
    jj = lax.broadcasted_iota(jnp.int32, (LANES, tq), 0)
    cur = (i * tq + lax.broadcasted_iota(jnp.int32, (LANES, tq), 1)) // SEL_LEN
    forced = (jj == 0) | ((jj <= cur) & (jj > cur - N_LOCAL_FORCED))
    imp = jnp.where(jj > cur, NEG, imp + jnp.where(forced, FORCED_BONUS, 0.0))
    v = imp[:nb]
    ridx = lax.broadcasted_iota(jnp.int32, (nb, tq), 0)
    sel = jnp.zeros((nb, tq), F32)
    for _ in range(min(N_SEL, nb)):
        mx = jnp.max(v, axis=0, keepdims=True)
        first = jnp.min(jnp.where(v == mx, ridx, nb), axis=0, keepdims=True)
        hit = ridx == first
        sel = jnp.where(hit, 1.0, sel)
        v = jnp.where(hit, -jnp.inf, v)
    bias_t = jnp.where(sel > 0.0, 0.0, NEG)
    if nb < LANES:
        bias_t = jnp.concatenate([bias_t, jnp.zeros((LANES - nb, tq), F32)], axis=0)
    return jnp.concatenate(gated, axis=1), bias_t.T.astype(CDT)


ACC_ROWS = HEAD_DIM + 16


def _with_ones_row(vt):
    ones_row = jnp.where(lax.broadcasted_iota(jnp.int32, (ACC_ROWS - HEAD_DIM, vt.shape[1]), 0) == 0,
                         1.0, 0.0).astype(CDT)
    return jnp.concatenate([vt, ones_row], axis=0)


def _flash_init(m_ref, acc_ref):
    m_ref[...] = jnp.full(m_ref.shape, -jnp.inf, F32)
    acc_ref[...] = jnp.zeros(acc_ref.shape, F32)


def _nsa_body(q_ref, kc_ref, vct_ref, ovt_ref, ks_ref, e_ref, vs_ref, kw_ref, vw_ref, gate_ref, o_ref,
              qc_ref, oc_ref, s0_ref, s1_ref, cm_ref, m_ref, acc_ref, *, tq, r_heads, n_win, nb):
    s_refs = (s0_ref, s1_ref)
    gi = pl.program_id(1)
    i = pl.program_id(2)
    tk = tq
    cols = r_heads * tq
    for r in range(r_heads):
        qc_ref[r * tq:(r + 1) * tq, :HEAD_DIM] = q_ref[0, :, r * HEAD_DIM:(r + 1) * HEAD_DIM]
    _flash_init(m_ref, acc_ref)

    def local_iotas():
        key = lax.broadcasted_iota(jnp.int32, (tk, cols), 0)
        tok = lax.broadcasted_iota(jnp.int32, (tk, cols), 1) & (tq - 1)
        return key, tok

    def put(s, buf):
        s_refs[buf][...] = s
        cm_ref[buf] = jnp.max(s, axis=0, keepdims=True)

    def update(branch, vt, buf):
        m_prev = m_ref[branch]
        m_new = jnp.maximum(m_prev, cm_ref[buf])
        p = jnp.exp2(s_refs[buf][...] - m_new).astype(CDT)
        acc_ref[branch] = (jnp.exp2(m_prev - m_new) * acc_ref[branch]
                           + jnp.dot(_with_ones_row(vt), p, preferred_element_type=F32))
        m_ref[branch] = m_new

    def sel_scores(j, buf, masked):
        off = pl.multiple_of(j * tk, tk)
        kmat = jnp.concatenate([ks_ref[0, pl.ds(off, tk), :], e_ref[pl.ds(off, tk), :]], axis=1)
        s = lax.dot_general(kmat, qc_ref[...], (((1,), (1,)), ((), ())), preferred_element_type=F32)
        if masked:
            key, tok = local_iotas()
            if masked == "if_diagonal":
                tok = tok + jnp.where(j == i, 0, tk)
            s = jnp.where(key <= tok, s, NEG)
        put(s, buf)

    def sel_update(j, buf):
        update(0, vs_ref[0, 0, j], buf)

    def win_tile(dd):
        back = n_win - 1 - dd
        return back, jnp.maximum(i - back, 0)

    def win_scores(dd, buf):
        back, j = win_tile(dd)
        off = pl.multiple_of(j * tk, tk)
        s = lax.dot_general(kw_ref[0, pl.ds(off, tk), :], qc_ref[:, :HEAD_DIM], (((1,), (1,)), ((), ())),
                            preferred_element_type=F32)
        key, tok = local_iotas()
        if back * tk - (tk - 1) < 0:
            s = jnp.where(key - back * tk <= tok, s, NEG)
        if back > 0 or tq - 1 + back * tk >= WINDOW:
            reach = WINDOW - back * tk
            if back > 0:
                reach = reach - jnp.where(i >= back, 0, 2 * WINDOW + tq)
            s = jnp.where(tok < key + reach, s, NEG)
        put(s, buf)

    def win_update(dd, buf):
        update(1, vw_ref[0, 0, win_tile(dd)[1]], buf)

    win_scores(0, 0)
    if n_win > 1:
        win_scores(1, 1)
    oc_ref[...], bias = _cmp_branch(qc_ref[:, :HEAD_DIM], kc_ref[0, 0], vct_ref[0, 0], ovt_ref[...],
                                    gate_ref[0].T, gi, i, tq=tq, r_heads=r_heads, nb=nb)
    for r in range(r_heads):
        qc_ref[r * tq:(r + 1) * tq, HEAD_DIM:] = bias
    for dd in range(n_win):
        if dd == n_win - 1 and n_win % 2 == 0:
            sel_scores(0, 0, "if_diagonal")
        win_update(dd, dd % 2)
        if dd + 2 < n_win:
            win_scores(dd + 2, dd % 2)
    if n_win % 2 == 1:
        sel_scores(0, 0, "if_diagonal")

    n_pairs = jnp.maximum(i - 1, 0) // 2

    def body(jp, carry):
        j = 2 * jp
        sel_scores(j + 1, 1, False)
        sel_update(j, 0)
        sel_scores(j + 2, 0, False)
        sel_update(j + 1, 1)
        return carry

    lax.fori_loop(0, n_pairs, body, 0)
    j0 = 2 * n_pairs
    left = i - j0

    @pl.when(left == 1)
    def _():
        sel_scores(i, 1, True)
        sel_update(j0, 0)
        sel_update(i, 1)

    @pl.when(left == 2)
    def _():
        sel_scores(j0 + 1, 1, False)
        sel_update(j0, 0)
        sel_scores(i, 0, True)
        sel_update(j0 + 1, 1)
        sel_update(i, 0)

    @pl.when(left == 0)
    def _():
        sel_update(0, 0)

    gates_t = gate_ref[0].T
    rid = lax.broadcasted_iota(jnp.int32, gates_t.shape, 0)
    for r in range(r_heads):
        cs = slice(r * tq, (r + 1) * tq)
        o = oc_ref[:, cs]
        for branch in (0, 1):
            acc = acc_ref[branch]
            col = (gi * r_heads + r) * 3 + 1 + branch
            grow = jnp.sum(jnp.where(rid == col, gates_t, 0.0), axis=0, keepdims=True)
            o = o + acc[:HEAD_DIM, cs] * (grow / acc[HEAD_DIM:HEAD_DIM + 1, cs])
        o_ref[0, :, r * HEAD_DIM:(r + 1) * HEAD_DIM] = o.T.astype(o_ref.dtype)


def _nsa_attn(qkv, ks_blk0, kw_blk0, vt, gates, kc, vct, overlap_t, e_mat, *, g_groups, r_heads, seq):
    bsz = qkv.shape[0]
    qw = r_heads * HEAD_DIM
    tq = tk = vt.shape[-1]
    cols = r_heads * tq
    nc = kc.shape[2]
    kspec = lambda blk0: pl.BlockSpec((1, seq, HEAD_DIM), lambda b, g, i: (b, 0, blk0 + g))
    vspec = lambda h0: pl.BlockSpec((1, 1) + vt.shape[2:], lambda b, g, i: (b, h0 + g, 0, 0, 0))
    qspec = pl.BlockSpec((1, tq, qw), lambda b, g, i: (b, i, g))
    body = functools.partial(_nsa_body, tq=tq, r_heads=r_heads, n_win=(WINDOW + tk - 1) // tk + 1,
                             nb=seq // SEL_LEN)
    return pl.pallas_call(
        body,
        grid=(bsz, g_groups, seq // tq),
        in_specs=[qspec,
                  pl.BlockSpec((1, 1, nc, HEAD_DIM), lambda b, g, i: (b, g, 0, 0)),
                  pl.BlockSpec((1, 1, HEAD_DIM, nc), lambda b, g, i: (b, g, 0, 0)),
                  pl.BlockSpec((LANES, nc), lambda b, g, i: (0, 0)),
                  kspec(ks_blk0), pl.BlockSpec((seq, LANES), lambda b, g, i: (0, 0)), vspec(0),
                  kspec(kw_blk0), vspec(g_groups),
                  pl.BlockSpec((1, tq, LANES), lambda b, g, i: (b, i, 1))],
        out_specs=qspec,
        out_shape=jax.ShapeDtypeStruct((bsz, seq, g_groups * qw), CDT),
        scratch_shapes=[pltpu.VMEM((cols, 2 * HEAD_DIM), CDT),
                        pltpu.VMEM((HEAD_DIM, cols), F32),
                        pltpu.VMEM((tk, cols), F32), pltpu.VMEM((tk, cols), F32),
                        pltpu.VMEM((2, 1, cols), F32),
                        pltpu.VMEM((2, 1, cols), F32),
                        pltpu.VMEM((2, ACC_ROWS, cols), F32)],
        compiler_params=_params(("arbitrary",) * 3),
        name="nsa_attn",
    )(qkv, kc, vct, overlap_t, qkv, e_mat, vt, qkv, vt, gates)


def _mla_body(q_ref, kn_ref, kr_ref, vt_ref, o_ref, *rest, tq, tk, hg):
    s_refs = rest[:hg]
    cm_ref, m_ref, acc_ref = rest[hg:]
    i = pl.program_id(2)
    _flash_init(m_ref, acc_ref)
    n_full = (i * tq) // tk

    def step(j, mask, c0):
        off = pl.multiple_of(j * tk, tk)
        kr = kr_ref[0, pl.ds(off, tk), :]

        def scores(c):
            kmat = jnp.concatenate([kn_ref[0, pl.ds(off, tk), c * MLA_NOPE:(c + 1) * MLA_NOPE], kr], axis=1)
            s = lax.dot_general(kmat, q_ref[0, c0:, c * MLA_QK_PAD:(c + 1) * MLA_QK_PAD],
                                (((1,), (1,)), ((), ())), preferred_element_type=F32)
            if mask is not None:
                s = jnp.where(mask, s, NEG)
            s_refs[c][:, c0:] = s
            cm_ref[c, :, c0:] = jnp.max(s, axis=0, keepdims=True)

        def update(c):
            m_prev = m_ref[c, :, c0:]
            m_new = jnp.maximum(m_prev, cm_ref[c, :, c0:])
            p = jnp.exp2(s_refs[c][:, c0:] - m_new).astype(CDT)
            acc_ref[c, :, c0:] = (jnp.exp2(m_prev - m_new) * acc_ref[c, :, c0:]
                                  + jnp.dot(_with_ones_row(vt_ref[0, c, j]), p, preferred_element_type=F32))
            m_ref[c, :, c0:] = m_new

        scores(0)
        for c in range(1, hg):
            scores(c)
            update(c - 1)
        update(hg - 1)

    def full_tile(j, carry):
        step(j, None, 0)
        return carry

    lax.fori_loop(0, n_full, full_tile, 0)
    for dd in range(tq // tk):
        w = tq - dd * tk
        mask = (lax.broadcasted_iota(jnp.int32, (tk, w), 0) <= lax.broadcasted_iota(jnp.int32, (tk, w), 1))
        step(n_full + dd, mask, dd * tk)
    for c in range(hg):
        acc = acc_ref[c]
        o = acc[:HEAD_DIM] * (1.0 / acc[HEAD_DIM:HEAD_DIM + 1])
        o_ref[0, :, c * MLA_V:(c + 1) * MLA_V] = o.T.astype(o_ref.dtype)


def _mla_attn(q, kv, kr, vt, *, heads, seq):
    bsz = q.shape[0]
    tk = vt.shape[-1]
    tq = _tile(seq, (1024, 512, 256, 128))
    hg = 4 if heads % 4 == 0 else 1
    body = functools.partial(_mla_body, tq=tq, tk=tk, hg=hg)
    return pl.pallas_call(
        body,
        grid=(bsz, heads // hg, seq // tq),
        in_specs=[pl.BlockSpec((1, tq, hg * MLA_QK_PAD), lambda b, h, i: (b, i, h)),
                  pl.BlockSpec((1, seq, hg * MLA_NOPE), lambda b, h, i: (b, 0, h)),
                  pl.BlockSpec((1, seq, LANES), lambda b, h, i: (b, 0, 0)),
                  pl.BlockSpec((1, hg) + vt.shape[2:], lambda b, h, i: (b, h, 0, 0, 0))],
        out_specs=pl.BlockSpec((1, tq, hg * MLA_V), lambda b, h, i: (b, i, h)),
        out_shape=jax.ShapeDtypeStruct((bsz, seq, heads * MLA_V), CDT),
        scratch_shapes=[pltpu.VMEM((tk, tq), F32)] * hg + [
            pltpu.VMEM((hg, 1, tq), F32), pltpu.VMEM((hg, 1, tq), F32), pltpu.VMEM((hg, ACC_ROWS, tq), F32)],
        compiler_params=_params(("arbitrary",) * 3),
        name="mla_attn",
    )(q, kv, kr, vt)


def _rope_tables(positions, rot_dim, period):
    half = rot_dim // 2
    inv = jnp.power(ROPE_THETA, -jnp.arange(0, rot_dim, 2, dtype=F32) / rot_dim)
    ang = positions.astype(F32).reshape(-1)[:, None] * inv
    cos, sin = jnp.cos(ang), jnp.sin(ang)
    t = ang.shape[0]
    rest = period - rot_dim
    ct = jnp.concatenate([cos, cos, jnp.ones((t, rest), F32)], axis=1)
    sa = jnp.concatenate([-sin, jnp.zeros((t, half + rest), F32)], axis=1)
    sb = jnp.concatenate([jnp.zeros((t, half), F32), sin, jnp.zeros((t, rest), F32)], axis=1)
    return ct, sa, sb


def _table_extras(tables, tm):
    return [(tb, pl.BlockSpec((tm, tb.shape[1]), lambda i, j: (i, 0))) for tb in tables]


def _layer(x2, mod, positions, w_in, pos_k, pos_v, k1, k2, v1, v2, q_norm, kv_norm, w_uq, w_ukv,
           w_out, ln1_g, ln1_b, w_ff1, w_ff2, ln2_g, ln2_b, *, bsz, seq, alpha):
    t, d = x2.shape
    n_heads = d // HEAD_DIM
    nsa_h = n_heads // 2
    mla_h = n_heads - nsa_h
    q_rank = q_norm.shape[0]
    kv_rank = kv_norm.shape[0]
    d_in = w_in.shape[1]
    g_groups = (d_in - nsa_h * HEAD_DIM - nsa_h * 3 - q_rank - kv_rank - MLA_ROPE) // (6 * HEAD_DIM)
    r_heads = nsa_h // g_groups
    gw = g_groups * HEAD_DIM
    qw = nsa_h * HEAD_DIM
    sh_a, sc_a, g_a, sh_m, sc_m, g_m = jnp.split(mod, N_ADA, axis=-1)

    o_q, o_kv = 0, qw
    o_gate = o_kv + 6 * gw
    o_cq = o_gate + nsa_h * 3
    o_ckv = o_cq + q_rank
    o_kr = o_ckv + kv_rank
    kvcol = lambda idx: w_in[:, o_kv + idx * gw:o_kv + (idx + 1) * gw]
    w1c = jnp.concatenate([w_in[:, o_q:o_q + qw], kvcol(2), kvcol(4)], axis=1).astype(CDT)
    wvc = jnp.concatenate([kvcol(3), kvcol(5)], axis=1).astype(CDT)
    w2c = jnp.concatenate([w_in[:, o_cq:o_cq + q_rank], w_in[:, o_ckv:o_ckv + kv_rank], kvcol(0),
                           kvcol(1)], axis=1).astype(CDT)
    zpad = lambda n: jnp.zeros((d, n), w_in.dtype)
    w3c = jnp.concatenate([w_in[:, o_kr:o_kr + MLA_ROPE], zpad(LANES - MLA_ROPE),
                           w_in[:, o_gate:o_gate + nsa_h * 3], zpad(LANES - nsa_h * 3)],
                          axis=1).astype(CDT)

    h = _modcast(x2, sc_a, sh_a, seq)

    tm = _tile(t, (1024, 512, 256, 128))
    nsa_tabs = _rope_tables(positions, PARTIAL_ROT, LANES)
    n1 = w1c.shape[1]
    tn1 = _tile(math.gcd(qw, 2 * gw), (1024, 512, 256, 128))
    scale = HEAD_DIM ** -0.5 * LOG2E
    p1 = _mm([h], w1c, tm=tm, tn=tn1, out_dtype=CDT, name="in_proj_rot",
             epilogue=_rope_epilogue(PARTIAL_ROT // 2, 0, n1 // tn1, qw // tn1, scale),
             extras=_table_extras(nsa_tabs, tm))
    tk = _tile(seq, (512, 256, 128))
    vt_nsa = _mm([h], wvc, tm=tm, tn=_tile(2 * gw, (1024, 512, 256, 128)), out_dtype=CDT, name="in_proj_vt",
                 head_tiles=tk, seq=seq)
    n2 = w2c.shape[1]
    lat = q_rank + kv_rank
    tn2 = _tile(math.gcd(lat, n2), (1024, 512, 256, 128))
    assert gw <= tn2 and gw % MM_SUB_COLS == 0
    p2 = _mm([h], w2c, tm=tm, tn=tn2, out_dtype=F32, name="in_proj_lat",
             epilogue=_rope_epilogue(PARTIAL_ROT // 2, 0, 0, 0, 1.0, part=(lat // tn2, gw // MM_SUB_COLS)),
             extras=_table_extras(nsa_tabs, tm))
    kr_tabs = _rope_tables(positions, MLA_ROPE, LANES)
    p3 = _mm([h], w3c, tm=tm, tn=2 * LANES, out_dtype=F32, name="in_proj_kr_gate",
             epilogue=_krope_gate_epilogue, extras=_table_extras(kr_tabs, tm))

    qkv = p1.reshape(bsz, seq, n1)
    gates = p3.reshape(bsz, seq, 2 * LANES)

    nchunk = seq // CMP_STRIDE
    ncmp = (seq - CMP_LEN) // CMP_STRIDE + 1
    blk = lambda col: col // HEAD_DIM
    kc = _compress(p2, blk(lat), pos_k, k1, k2, bsz=bsz, seq=seq, g_groups=g_groups)
    vc = _compress(p2, blk(lat + gw), pos_v, v1, v2, bsz=bsz, seq=seq, g_groups=g_groups)

    nb = seq // SEL_LEN
    c_start = np.arange(nchunk) * CMP_STRIDE
    b_start = np.arange(LANES) * SEL_LEN
    overlap = ((c_start[:, None] < b_start[None, :] + SEL_LEN) &
               (c_start[:, None] + CMP_LEN > b_start[None, :]) &
               (np.arange(nchunk)[:, None] < ncmp) & (np.arange(LANES)[None, :] < nb))
    overlap_t = jnp.asarray(overlap.T.astype(np.float32)).astype(CDT)
    e_mat = jnp.asarray((np.arange(seq)[:, None] // SEL_LEN == np.arange(LANES)[None, :])
                        .astype(np.float32)).astype(CDT)
    o_nsa = _nsa_attn(qkv, blk(qw), blk(qw + gw), vt_nsa, gates, kc, vc.transpose(0, 1, 3, 2),
                      overlap_t, e_mat, g_groups=g_groups, r_heads=r_heads, seq=seq)

    qk_dim = MLA_NOPE + MLA_ROPE
    wq = w_uq.reshape(q_rank, mla_h, qk_dim)
    wq = jnp.concatenate([wq, jnp.zeros((q_rank, mla_h, MLA_QK_PAD - qk_dim), wq.dtype)], axis=-1)
    wq = wq.reshape(q_rank, mla_h * MLA_QK_PAD).astype(CDT)
    wkv = w_ukv.reshape(kv_rank, mla_h, MLA_NOPE + MLA_V)
    wk = wkv[:, :, :MLA_NOPE].reshape(kv_rank, mla_h * MLA_NOPE).astype(CDT)
    wv = wkv[:, :, MLA_NOPE:].reshape(kv_rank, mla_h * MLA_V).astype(CDT)
    tnq = _tile(mla_h * MLA_QK_PAD, (1024, 512, 256))
    q_mla = _mm([p2], wq, tm=tm, tn=tnq, out_dtype=CDT, name="mla_uq",
                a_specs=[pl.BlockSpec((tm, q_rank), lambda i, j: (i, 0))],
                prologue=_rms_prologue(1e-6),
                epilogue=_mla_q_epilogue(qk_dim ** -0.5 * LOG2E),
                extras=_table_extras(kr_tabs, tm)
                + [(q_norm.reshape(1, q_rank), pl.BlockSpec((1, q_rank), lambda i, j: (0, 0)))])
    tnkv = _tile(wk.shape[1], (1024, 512, 256, 128))
    ckv_args = dict(tm=tm, tn=tnkv, out_dtype=CDT, prologue=_rms_prologue(1e-6),
                    a_specs=[pl.BlockSpec((tm, kv_rank), lambda i, j: (i, q_rank // kv_rank))],
                    extras=[(kv_norm.reshape(1, kv_rank), pl.BlockSpec((1, kv_rank), lambda i, j: (0, 0)))])
    k_mla = _mm([p2], wk, name="mla_uk", **ckv_args)
    vt_mla = _mm([p2], wv, name="mla_uv", head_tiles=tk, seq=seq, **ckv_args)
    o_mla = _mla_attn(q_mla.reshape(bsz, seq, -1), k_mla.reshape(bsz, seq, -1),
                      gates[:, :, :LANES].astype(CDT), vt_mla, heads=mla_h, seq=seq)

    tno = _tile(d, (1024, 512, 256, 128))
    a = _mm([o_nsa.reshape(t, qw), o_mla.reshape(t, mla_h * MLA_V)], w_out.astype(CDT), tm=tm, tn=tno,
            out_dtype=F32, name="out_proj")
    x1, h2 = _res_ln(x2, a, g_a, ln1_g, ln1_b, seq, alpha, mod=(sc_m, sh_m))
    d_ff = w_ff1.shape[1]
    f1 = _mm([h2], w_ff1.astype(CDT), tm=tm, tn=_tile(d_ff, (1024, 512, 256, 128)), out_dtype=CDT,
             name="ff1", epilogue=_relu2_epilogue)
    f2 = _mmk(f1, w_ff2.astype(CDT), tm=tm, tn=tno, tk=_tile(d_ff, (4096, 2048, 1024, 512)), name="ff2")
    return _res_ln(x1, f2, g_m, ln2_g, ln2_b, seq, alpha)


def kernel(x, c, positions, w_ada, b_ada, w_in, nsa_pos_k, nsa_pos_v, nsa_cmp_k1, nsa_cmp_k2, nsa_cmp_v1, nsa_cmp_v2, mla_q_norm, mla_kv_norm, mla_w_uq, mla_w_ukv, w_out, ln1_g, ln1_b, w_ff1, w_ff2, ln2_g, ln2_b):
    bsz, seq, d = x.shape
    depth = w_ada.shape[0]
    alpha = (2 * depth) ** 0.25
    x2 = x.reshape(bsz * seq, d)
    for layer in range(depth):
        mod = _ada(c, w_ada[layer], b_ada[layer])
        x2 = _layer(x2, mod, positions, w_in[layer], nsa_pos_k[layer], nsa_pos_v[layer],
                    nsa_cmp_k1[layer], nsa_cmp_k2[layer], nsa_cmp_v1[layer], nsa_cmp_v2[layer],
                    mla_q_norm[layer], mla_kv_norm[layer], mla_w_uq[layer], mla_w_ukv[layer],
                    w_out[layer], ln1_g[layer], ln1_b[layer], w_ff1[layer], w_ff2[layer],
                    ln2_g[layer], ln2_b[layer], bsz=bsz, seq=seq, alpha=alpha)
    return x2.reshape(bsz, seq, d)
```
